```python
import math
import jax, jax.numpy as jnp
from jax import lax
import numpy as np

D_MODEL = 2048
BATCH = 4
SEQ = 2048
DEPTH = 1
DEC_BATCH = 128
DEC_SEQ = 1
PAST_LEN = 16384
PAGE_SIZE = 128

D_MIX = D_MODEL
D_RWKV = D_MIX // 2
D_CONV = D_MIX - D_RWKV
HEAD_DIM = 64
N_HEADS_RWKV = D_RWKV // HEAD_DIM
CONV_GROUP = 64
N_CONV_GROUPS = D_CONV // CONV_GROUP
DECAY_LORA = 64
ICLR_LORA = 64
GATE_LORA = 160
CONV_WIDTH = 31
N_MEM = 256
N_MEM_HEADS = 4
MEM_HEAD_DIM = D_MODEL // N_MEM_HEADS
D_FF = 5632
ALPHA = (2.0 * DEPTH) ** 0.25
BETA = (8.0 * DEPTH) ** -0.25
LN_EPS = 1e-5
GN_EPS = 64e-5
SHIFT_COLS = 3 * D_RWKV + DECAY_LORA + ICLR_LORA + GATE_LORA
IN_COLS = SHIFT_COLS + 2 * D_CONV
SPLITS = [D_RWKV, 2 * D_RWKV, 3 * D_RWKV, 3 * D_RWKV + DECAY_LORA, 3 * D_RWKV + DECAY_LORA + ICLR_LORA]

kernel_name = "hymba_rwkv7_conformer_macaron_deepnorm_step"


def layer_norm(x, g, b, eps=LN_EPS):
    xf = x.astype(jnp.float32)
    mu = jnp.mean(xf, axis=-1, keepdims=True)
    var = jnp.mean(jnp.square(xf - mu), axis=-1, keepdims=True)
    return ((xf - mu) * lax.rsqrt(var + eps) * g.astype(jnp.float32) + b.astype(jnp.float32)).astype(x.dtype)


def swiglu(x, w1, w3, w2):
    return (jax.nn.silu(x @ w1) * (x @ w3)) @ w2


def rwkv7_recurrence(state, r, decay, k, v, kk, a):
    def step(S, inp):
        r_t, w_t, k_t, v_t, kk_t, a_t = inp
        sa = jnp.einsum('bhvk,bhk->bhv', S, -kk_t)
        S = (S * w_t[:, :, None, :]
             + sa[..., None] * (kk_t * a_t)[:, :, None, :]
             + v_t[..., None] * k_t[:, :, None, :])
        y_t = jnp.einsum('bhvk,bhk->bhv', S, r_t)
        return S, y_t
    xs = tuple(jnp.swapaxes(t, 0, 1) for t in (r, decay, k, v, kk, a))
    S, ys = lax.scan(step, state, xs)
    return jnp.swapaxes(ys, 0, 1), S


def parallel_mixer(h, shift_prev, conv_prev, wkv_prev, p):
    B, T, _ = h.shape
    dt = h.dtype
    f32 = jnp.float32
    proj = h @ p['w_in']
    ps, pc = proj[..., :SHIFT_COLS], proj[..., SHIFT_COLS:]
    ps_prev = jnp.concatenate([shift_prev[:, None, :].astype(dt), ps[:, :-1]], axis=1)
    pm = ps + p['mu_shift'] * (ps_prev - ps)
    r, k, v, wd, ad, gd = jnp.split(pm, SPLITS, axis=-1)

    w_log = -jax.nn.softplus(-(p['w0'] + jnp.tanh(wd) @ p['w2_decay']).astype(f32)) - 0.5
    decay = jnp.exp(-jnp.exp(w_log))
    a = jax.nn.sigmoid((p['a0'] + ad @ p['a2_iclr']).astype(f32))
    g = (jax.nn.sigmoid(gd) @ p['g2_gate']).astype(f32)
    rf, kf, vf = r.astype(f32), k.astype(f32), v.astype(f32)
    kk = (kf * p['k_k'].astype(f32)).reshape(B, T, N_HEADS_RWKV, HEAD_DIM)
    kk = kk / jnp.maximum(jnp.linalg.norm(kk, axis=-1, keepdims=True), 1e-12)
    kf = kf * (1.0 + (a - 1.0) * p['k_a'].astype(f32))
    hd = lambda t: t.reshape(B, T, N_HEADS_RWKV, HEAD_DIM)
    rh, kh, vh, wh, ah = hd(rf), hd(kf), hd(vf), hd(decay), hd(a)
    y, wkv_new = rwkv7_recurrence(wkv_prev.astype(f32), rh, wh, kh, vh, kk, ah)
    mu = jnp.mean(y, axis=-1, keepdims=True)
    var = jnp.mean(jnp.square(y - mu), axis=-1, keepdims=True)
    yn = ((y - mu) * lax.rsqrt(var + GN_EPS)).reshape(B, T, D_RWKV)
    yn = yn * p['gn_g'].astype(f32) + p['gn_b'].astype(f32)
    bonus = (jnp.sum(rh * kh * p['r_k'].astype(f32), axis=-1, keepdims=True) * vh).reshape(B, T, D_RWKV)
    o_a = ((yn + bonus) * g).astype(dt)

    u = pc[..., :D_CONV] * jax.nn.sigmoid(pc[..., D_CONV:])
    ubuf = jnp.concatenate([conv_prev.astype(dt), u], axis=1)
    c = lax.conv_general_dilated(ubuf, p['conv_w'], window_strides=(1,), padding='VALID',
                                 dimension_numbers=('NWC', 'WIO', 'NWC'),
                                 feature_group_count=D_CONV) + p['conv_b']
    o_b = jax.nn.silu(layer_norm(c, p['conv_ln_g'], p['conv_ln_b']))

    merged = jnp.concatenate([o_a * p['beta_rwkv'], o_b * p['beta_conv']], axis=-1) @ p['w_out']
    return merged, ps[:, -1], ubuf[:, -(CONV_WIDTH - 1):], wkv_new


def mem_kv(mem, w_mk, w_mv):
    B = mem.shape[0]
    mk = (mem @ w_mk).reshape(B, N_MEM, N_MEM_HEADS, MEM_HEAD_DIM)
    mv = (mem @ w_mv).reshape(B, N_MEM, N_MEM_HEADS, MEM_HEAD_DIM)
    return mk, mv


def mem_attend(h, mk, mv, w_mq, w_mo):
    B, T, _ = h.shape
    q = (h @ w_mq).reshape(B, T, N_MEM_HEADS, MEM_HEAD_DIM)
    s = jnp.einsum('bthd,bshd->bhts', q, mk.astype(h.dtype)).astype(jnp.float32) / math.sqrt(MEM_HEAD_DIM)
    pr = jax.nn.softmax(s, axis=-1).astype(h.dtype)
    o = jnp.einsum('bhts,bshd->bthd', pr, mv.astype(h.dtype)).reshape(B, T, D_MODEL)
    return o @ w_mo


def trunk_layer(x, shift_prev, conv_prev, wkv_prev, mk, mv, p):
    x = layer_norm(ALPHA * x + 0.5 * swiglu(x, p['ffn1_w1'], p['ffn1_w3'], p['ffn1_w2']), p['ln1_g'], p['ln1_b'])
    mix, sh, cv, wkv = parallel_mixer(x, shift_prev, conv_prev, wkv_prev, p)
    x = layer_norm(ALPHA * x + mix, p['ln2_g'], p['ln2_b'])
    x = layer_norm(ALPHA * x + mem_attend(x, mk, mv, p['w_mq'], p['w_mo']), p['ln3_g'], p['ln3_b'])
    x = layer_norm(ALPHA * x + 0.5 * swiglu(x, p['ffn2_w1'], p['ffn2_w3'], p['ffn2_w2']), p['ln4_g'], p['ln4_b'])
    return x, sh, cv, wkv


def setup_inputs(seed: int = 0) -> dict:
    key = jax.random.key(seed)
    ks = iter(jax.random.split(key, 64))
    nrm = lambda shape, s=1.0: jax.random.normal(next(ks), shape, jnp.float32) * s
    gain = lambda n: 1.0 + nrm((DEPTH, n), 0.02)
    bias = lambda n: nrm((DEPTH, n), 0.02)
    L = DEPTH
    d = {}
    d['x_prompt'] = nrm((BATCH, SEQ, D_MODEL))
    d['x_sample'] = nrm((DEC_BATCH, DEC_SEQ, D_MODEL))
    d['mem_prompt'] = nrm((BATCH, N_MEM, D_MODEL))
    d['state_shift'] = nrm((L, DEC_BATCH, SHIFT_COLS))
    d['state_conv'] = nrm((L, DEC_BATCH, CONV_WIDTH - 1, D_CONV), 0.5)
    d['state_wkv'] = nrm((L, DEC_BATCH, N_HEADS_RWKV, HEAD_DIM, HEAD_DIM), 0.5)
    d['cache_mem_k'] = nrm((L, DEC_BATCH, N_MEM, N_MEM_HEADS, MEM_HEAD_DIM))
    d['cache_mem_v'] = nrm((L, DEC_BATCH, N_MEM, N_MEM_HEADS, MEM_HEAD_DIM))
    d['ffn1_w1'] = nrm((L, D_MODEL, D_FF), D_MODEL ** -0.5)
    d['ffn1_w3'] = nrm((L, D_MODEL, D_FF), D_MODEL ** -0.5)
    d['ffn1_w2'] = nrm((L, D_FF, D_MODEL), BETA * D_FF ** -0.5)
    d['ln1_g'] = gain(D_MODEL)
    d['ln1_b'] = bias(D_MODEL)
    d['w_in'] = nrm((L, D_MODEL, IN_COLS), D_MODEL ** -0.5)
    d['mu_shift'] = jax.random.uniform(next(ks), (L, SHIFT_COLS), jnp.float32)
    d['w0'] = jax.random.uniform(next(ks), (L, D_RWKV), jnp.float32, -6.0, -0.5)
    d['w2_decay'] = nrm((L, DECAY_LORA, D_RWKV), 0.1 * DECAY_LORA ** -0.5)
    d['a0'] = nrm((L, D_RWKV), 0.1)
    d['a2_iclr'] = nrm((L, ICLR_LORA, D_RWKV), 0.5 * ICLR_LORA ** -0.5)
    d['g2_gate'] = nrm((L, GATE_LORA, D_RWKV), GATE_LORA ** -0.5)
    d['k_k'] = 0.85 + nrm((L, D_RWKV), 0.05)
    d['k_a'] = 1.0 + nrm((L, D_RWKV), 0.05)
    d['r_k'] = nrm((L, N_HEADS_RWKV, HEAD_DIM), 0.1)
    d['gn_g'] = gain(D_RWKV)
    d['gn_b'] = bias(D_RWKV)
    d['conv_w'] = nrm((L, CONV_WIDTH, 1, D_CONV), CONV_WIDTH ** -0.5)
    d['conv_b'] = bias(D_CONV)
    d['conv_ln_g'] = gain(D_CONV)
    d['conv_ln_b'] = bias(D_CONV)
    d['beta_rwkv'] = gain(D_RWKV)
    d['beta_conv'] = gain(D_CONV)
    d['w_out'] = nrm((L, D_MIX, D_MODEL), BETA * D_MIX ** -0.5)
    d['ln2_g'] = gain(D_MODEL)
    d['ln2_b'] = bias(D_MODEL)
    d['w_mq'] = nrm((L, D_MODEL, D_MODEL), D_MODEL ** -0.5)
    d['w_mk'] = nrm((L, D_MODEL, D_MODEL), D_MODEL ** -0.5)
    d['w_mv'] = nrm((L, D_MODEL, D_MODEL), D_MODEL ** -0.5)
    d['w_mo'] = nrm((L, D_MODEL, D_MODEL), BETA * D_MODEL ** -0.5)
    d['ln3_g'] = gain(D_MODEL)
    d['ln3_b'] = bias(D_MODEL)
    d['ffn2_w1'] = nrm((L, D_MODEL, D_FF), D_MODEL ** -0.5)
    d['ffn2_w3'] = nrm((L, D_MODEL, D_FF), D_MODEL ** -0.5)
    d['ffn2_w2'] = nrm((L, D_FF, D_MODEL), BETA * D_FF ** -0.5)
    d['ln4_g'] = gain(D_MODEL)
    d['ln4_b'] = bias(D_MODEL)
    return d


def reference(x_prompt, x_sample, mem_prompt, state_shift, state_conv, state_wkv, cache_mem_k, cache_mem_v,
              ffn1_w1, ffn1_w3, ffn1_w2, ln1_g, ln1_b, w_in, mu_shift, w0, w2_decay, a0, a2_iclr, g2_gate,
              k_k, k_a, r_k, gn_g, gn_b, conv_w, conv_b, conv_ln_g, conv_ln_b, beta_rwkv, beta_conv, w_out,
              ln2_g, ln2_b, w_mq, w_mk, w_mv, w_mo, ln3_g, ln3_b, ffn2_w1, ffn2_w3, ffn2_w2, ln4_g, ln4_b):
    B = x_prompt.shape[0]
    dt = x_prompt.dtype
    hp, hs = x_prompt, x_sample
    sh_p_all, cv_p_all, wkv_p_all, mk_p_all, mv_p_all = [], [], [], [], []
    sh_s_all, cv_s_all, wkv_s_all = [], [], []
    for l in range(DEPTH):
        p = dict(ffn1_w1=ffn1_w1[l], ffn1_w3=ffn1_w3[l], ffn1_w2=ffn1_w2[l], ln1_g=ln1_g[l], ln1_b=ln1_b[l],
                 w_in=w_in[l], mu_shift=mu_shift[l], w0=w0[l], w2_decay=w2_decay[l], a0=a0[l],
                 a2_iclr=a2_iclr[l], g2_gate=g2_gate[l], k_k=k_k[l], k_a=k_a[l], r_k=r_k[l],
                 gn_g=gn_g[l], gn_b=gn_b[l], conv_w=conv_w[l], conv_b=conv_b[l], conv_ln_g=conv_ln_g[l],
                 conv_ln_b=conv_ln_b[l], beta_rwkv=beta_rwkv[l], beta_conv=beta_conv[l], w_out=w_out[l],
                 ln2_g=ln2_g[l], ln2_b=ln2_b[l], w_mq=w_mq[l], w_mo=w_mo[l], ln3_g=ln3_g[l], ln3_b=ln3_b[l],
                 ffn2_w1=ffn2_w1[l], ffn2_w3=ffn2_w3[l], ffn2_w2=ffn2_w2[l], ln4_g=ln4_g[l], ln4_b=ln4_b[l])
        mk_p, mv_p = mem_kv(mem_prompt, w_mk[l], w_mv[l])
        hp, sh_p, cv_p, wkv_p = trunk_layer(
            hp, jnp.zeros((B, SHIFT_COLS), dt), jnp.zeros((B, CONV_WIDTH - 1, D_CONV), dt),
            jnp.zeros((B, N_HEADS_RWKV, HEAD_DIM, HEAD_DIM), jnp.float32), mk_p, mv_p, p)
        hs, sh_s, cv_s, wkv_s = trunk_layer(
            hs, state_shift[l], state_conv[l], state_wkv[l], cache_mem_k[l], cache_mem_v[l], p)
        sh_p_all.append(sh_p); cv_p_all.append(cv_p); wkv_p_all.append(wkv_p)
        mk_p_all.append(mk_p); mv_p_all.append(mv_p)
        sh_s_all.append(sh_s); cv_s_all.append(cv_s); wkv_s_all.append(wkv_s)
    return (hp, hs,
            jnp.stack(sh_p_all), jnp.stack(cv_p_all), jnp.stack(wkv_p_all),
            jnp.stack(mk_p_all), jnp.stack(mv_p_all),
            jnp.stack(sh_s_all), jnp.stack(cv_s_all), jnp.stack(wkv_s_all))
```

```python
import functools
import math

import jax
import jax.numpy as jnp
from jax import lax
from jax.experimental import pallas as pl
from jax.experimental.pallas import tpu as pltpu

F32 = jnp.float32
BF16 = jnp.bfloat16

V7X_LANES = 128
V7X_SUBLANES = 8
V7X_VMEM_LIMIT_BYTES = 56 * 1024 * 1024

HEAD_DIM = 64
CONV_WIDTH = 31
DECAY_LORA = 64
ICLR_LORA = 64
GATE_LORA = 160
N_MEM_HEADS = 4
LN_EPS = 1e-5
GN_EPS = 64e-5


def _params(*sem):
    return pltpu.CompilerParams(dimension_semantics=sem, vmem_limit_bytes=V7X_VMEM_LIMIT_BYTES)


def _resident(block_shape, index_map):
    return pl.BlockSpec(block_shape, index_map, pipeline_mode=pl.Buffered(1))


def _ln_rows(y, g, b, eps):
    mu = jnp.mean(y, axis=-1, keepdims=True)
    yc = y - mu
    var = jnp.mean(yc * yc, axis=-1, keepdims=True)
    return yc * lax.rsqrt(var + eps) * g + b


def _dot(a, b):
    return jnp.dot(a, b, preferred_element_type=F32)


def _split3_dot(x, g):
    hi = x.astype(BF16)
    r1 = x - hi.astype(F32)
    mid = r1.astype(BF16)
    lo = (r1 - mid.astype(F32)).astype(BF16)
    return _dot(hi, g) + _dot(mid, g) + _dot(lo, g)


def _head_sum(x, gsel, gselt):
    return _split3_dot(_split3_dot(x, gsel), gselt)


def _ffn_ln_kernel(x_ref, w1_ref, w3_ref, w2_ref, g_ref, b_ref, o_ref, xb_ref, *, alpha, n_chunk, row_chunk):
    j = pl.program_id(1)
    tm, d = o_ref.shape

    @pl.when(j == 0)
    def _():
        xb_ref[...] = x_ref[...].astype(BF16)

    xb = xb_ref[...]
    h1 = _dot(xb, w1_ref[...].astype(BF16))
    h3 = _dot(xb, w3_ref[...].astype(BF16))
    hb = (h1 * jax.nn.sigmoid(h1) * h3).astype(BF16)
    w2b = w2_ref[...].astype(BF16)
    cw = d // n_chunk
    for c in range(n_chunk):
        part = _dot(hb, w2b[:, c * cw:(c + 1) * cw])

        @pl.when(j == 0)
        def _():
            o_ref[:, c * cw:(c + 1) * cw] = part

        @pl.when(j > 0)
        def _():
            o_ref[:, c * cw:(c + 1) * cw] += part

    @pl.when(j == pl.num_programs(1) - 1)
    def _():
        def body(r, carry):
            rows = pl.ds(pl.multiple_of(r * row_chunk, row_chunk), row_chunk)
            y = alpha * x_ref[rows, :] + 0.5 * o_ref[rows, :]
            o_ref[rows, :] = _ln_rows(y, g_ref[...], b_ref[...], LN_EPS)
            return carry

        lax.fori_loop(0, tm // row_chunk, body, 0)


def _ffn_ln(x, w1, w3, w2, g, b, alpha, tm, tf):
    m, d = x.shape
    f = w1.shape[1]
    kern = functools.partial(_ffn_ln_kernel, alpha=alpha, n_chunk=4, row_chunk=min(tm, 128))
    return pl.pallas_call(
        kern,
        out_shape=jax.ShapeDtypeStruct((m, d), F32),
        grid=(m // tm, f // tf),
        in_specs=[
            _resident((tm, d), lambda i, j: (i, 0)),
            pl.BlockSpec((d, tf), lambda i, j: (0, j)),
            pl.BlockSpec((d, tf), lambda i, j: (0, j)),
            pl.BlockSpec((tf, d), lambda i, j: (j, 0)),
            pl.BlockSpec((1, d), lambda i, j: (0, 0)),
            pl.BlockSpec((1, d), lambda i, j: (0, 0)),
        ],
        out_specs=pl.BlockSpec((tm, d), lambda i, j: (i, 0)),
        scratch_shapes=[pltpu.VMEM((tm, d), BF16)],
        compiler_params=_params("parallel", "arbitrary"),
        name="ffn_ln",
    )(x, w1, w3, w2, g, b)


def _mm_kernel(x_ref, w_ref, o_ref, xb_ref):
    @pl.when(pl.program_id(1) == 0)
    def _():
        xb_ref[...] = x_ref[...].astype(BF16)

    o_ref[...] = _dot(xb_ref[...], w_ref[...].astype(BF16))


def _mm(x, w, tm, tn):
    m, k = x.shape
    n = w.shape[1]
    return pl.pallas_call(
        _mm_kernel,
        out_shape=jax.ShapeDtypeStruct((m, n), F32),
        grid=(m // tm, n // tn),
        in_specs=[
            _resident((tm, k), lambda i, j: (i, 0)),
            pl.BlockSpec((k, tn), lambda i, j: (0, j)),
        ],
        out_specs=pl.BlockSpec((tm, tn), lambda i, j: (i, j)),
        scratch_shapes=[pltpu.VMEM((tm, k), BF16)],
        compiler_params=_params("parallel", "arbitrary"),
        name="mm",
    )(x, w)


def _mm_res_ln_kernel(*refs, n_a, alpha, row_chunk):
    a_refs = refs[:n_a]
    w_refs = refs[n_a:2 * n_a]
    res_ref, g_ref, b_ref, o_ref = refs[2 * n_a:2 * n_a + 4]
    ab_refs = refs[2 * n_a + 4:3 * n_a + 4]
    acc_ref = refs[3 * n_a + 4]
    j = pl.program_id(1)
    nj, tm, tn = acc_ref.shape

    @pl.when(j == 0)
    def _():
        for a_ref, ab_ref in zip(a_refs, ab_refs):
            ab_ref[...] = a_ref[...].astype(BF16)

    acc = _dot(ab_refs[0][...], w_refs[0][...].astype(BF16))
    for ab_ref, w_ref in zip(ab_refs[1:], w_refs[1:]):
        acc += _dot(ab_ref[...], w_ref[...].astype(BF16))
    acc_ref[j] = acc

    @pl.when(j == nj - 1)
    def _():
        d = nj * tn

        def body(r, carry):
            rows = pl.ds(pl.multiple_of(r * row_chunk, row_chunk), row_chunk)
            ys = [alpha * res_ref[rows, c * tn:(c + 1) * tn] + acc_ref[c, rows, :] for c in range(nj)]
            mu = sum(jnp.sum(y, axis=-1, keepdims=True) for y in ys) / d
            ycs = [y - mu for y in ys]
            var = sum(jnp.sum(yc * yc, axis=-1, keepdims=True) for yc in ycs) / d
            inv = lax.rsqrt(var + LN_EPS)
            for c in range(nj):
                cols = slice(c * tn, (c + 1) * tn)
                o_ref[rows, cols] = ycs[c] * inv * g_ref[:, cols] + b_ref[:, cols]
            return carry

        lax.fori_loop(0, tm // row_chunk, body, 0)


def _mm_res_ln(a_list, w, res, g, b, alpha, tm, tn):
    m, d = res.shape
    n_a = len(a_list)
    ka = a_list[0].shape[1]
    nj = d // tn
    kern = functools.partial(_mm_res_ln_kernel, n_a=n_a, alpha=alpha, row_chunk=min(tm, 128))
    in_specs = [_resident((tm, ka), lambda i, j: (i, 0)) for _ in range(n_a)]
    in_specs += [pl.BlockSpec((ka, tn), functools.partial(lambda i, j, r: (r, j), r=r)) for r in range(n_a)]
    in_specs += [
        _resident((tm, d), lambda i, j: (i, 0)),
        pl.BlockSpec((1, d), lambda i, j: (0, 0)),
        pl.BlockSpec((1, d), lambda i, j: (0, 0)),
    ]
    return pl.pallas_call(
        kern,
        out_shape=jax.ShapeDtypeStruct((m, d), F32),
        grid=(m // tm, nj),
        in_specs=in_specs,
        out_specs=pl.BlockSpec((tm, d), lambda i, j: (i, 0)),
        scratch_shapes=[pltpu.VMEM((tm, ka), BF16) for _ in range(n_a)] + [pltpu.VMEM((nj, tm, tn), F32)],
        compiler_params=_params("parallel", "arbitrary"),
        name="mm_res_ln",
    )(*a_list, *([w] * n_a), res, g, b)


def _shift_rows(cur, first_row):
    rolled = pltpu.roll(cur, 1, 0)
    row_id = lax.broadcasted_iota(jnp.int32, cur.shape, 0)
    return jnp.where(row_id == 0, first_row, rolled)


def _rwkv_prep_math(cur_rkv, prev_rkv, cur_lo, prev_lo, mu_rkv, mu_lo, w0, a0, k_k, k_a, r_k,
                    w2d, a2, g2, gsel, gselt, d_rwkv):
    pm = cur_rkv + mu_rkv * (prev_rkv - cur_rkv)
    pl_ = cur_lo + mu_lo * (prev_lo - cur_lo)
    r = pm[:, :d_rwkv]
    k = pm[:, d_rwkv:2 * d_rwkv]
    v = pm[:, 2 * d_rwkv:]
    wd = pl_[:, :DECAY_LORA]
    ad = pl_[:, DECAY_LORA:DECAY_LORA + ICLR_LORA]
    gd = pl_[:, DECAY_LORA + ICLR_LORA:]
    z = w0 + _dot(jnp.tanh(wd).astype(BF16), w2d.astype(BF16))
    w_log = -jax.nn.softplus(-z) - 0.5
    decay = jnp.exp(-jnp.exp(w_log))
    a = jax.nn.sigmoid(a0 + _dot(ad.astype(BF16), a2.astype(BF16)))
    g = _dot(jax.nn.sigmoid(gd).astype(BF16), g2.astype(BF16))
    kk = k * k_k
    norm = jnp.sqrt(_head_sum(kk * kk, gsel, gselt))
    kk = kk / jnp.maximum(norm, 1e-12)
    kmod = k * (1.0 + (a - 1.0) * k_a)
    bonus = _head_sum(r * kmod * r_k, gsel, gselt) * v
    return r, decay, kmod, v, kk, kk * a, g, bonus


def _rwkv_prep_seq_kernel(cur_rkv_ref, prev8_rkv_ref, cur_lo_ref, prev8_lo_ref, sp_rkv_ref, sp_lo_ref,
                          mu_rkv_ref, mu_lo_ref, w0_ref, a0_ref, kk_ref, ka_ref, rk_ref,
                          w2d_ref, a2_ref, g2_ref, gsel_ref, gselt_ref,
                          r_o, w_o, k_o, v_o, kk_o, b_o, g_o, bonus_o):
    first = pl.program_id(1) == 0
    cur_rkv = cur_rkv_ref[...]
    cur_lo = cur_lo_ref[...]
    last = V7X_SUBLANES - 1
    first_rkv = jnp.where(first, sp_rkv_ref[...], prev8_rkv_ref[last:last + 1, :])
    first_lo = jnp.where(first, sp_lo_ref[...], prev8_lo_ref[last:last + 1, :])
    outs = _rwkv_prep_math(
        cur_rkv, _shift_rows(cur_rkv, first_rkv), cur_lo, _shift_rows(cur_lo, first_lo),
        mu_rkv_ref[...], mu_lo_ref[...], w0_ref[...], a0_ref[...], kk_ref[...], ka_ref[...], rk_ref[...],
        w2d_ref[...], a2_ref[...], g2_ref[...], gsel_ref[...], gselt_ref[...], r_o.shape[-1])
    for o_ref, val in zip((r_o, w_o, k_o, v_o, kk_o, b_o, g_o, bonus_o), outs):
        o_ref[...] = val


def _rwkv_prep_step_kernel(cur_rkv_ref, prev_rkv_ref, cur_lo_ref, prev_lo_ref,
                           mu_rkv_ref, mu_lo_ref, w0_ref, a0_ref, kk_ref, ka_ref, rk_ref,
                           w2d_ref, a2_ref, g2_ref, gsel_ref, gselt_ref,
                           r_o, w_o, k_o, v_o, kk_o, b_o, g_o, bonus_o):
    outs = _rwkv_prep_math(
        cur_rkv_ref[...], prev_rkv_ref[...], cur_lo_ref[...], prev_lo_ref[...],
        mu_rkv_ref[...], mu_lo_ref[...], w0_ref[...], a0_ref[...], kk_ref[...], ka_ref[...], rk_ref[...],
        w2d_ref[...], a2_ref[...], g2_ref[...], gsel_ref[...], gselt_ref[...], r_o.shape[-1])
    for o_ref, val in zip((r_o, w_o, k_o, v_o, kk_o, b_o, g_o, bonus_o), outs):
        o_ref[...] = val


def _const_spec(arr):
    nd = arr.ndim
    return pl.BlockSpec(arr.shape, lambda *_: (0,) * nd)


def _rwkv_prep_seq(p_rkv, p_lo, sp_rkv, sp_lo, consts, batch, seq, tt):
    m, n_rkv = p_rkv.shape
    n_lo = p_lo.shape[1]
    d_rwkv = n_rkv // 3
    nt = seq // tt
    sub = V7X_SUBLANES

    def cur_map(b, i):
        return (b * nt + i, 0)

    def prev_map(b, i):
        return (jnp.maximum((b * seq + i * tt) // sub - 1, 0), 0)

    out_sds = jax.ShapeDtypeStruct((m, d_rwkv), F32)
    out_spec = pl.BlockSpec((tt, d_rwkv), cur_map)
    return pl.pallas_call(
        _rwkv_prep_seq_kernel,
        out_shape=[out_sds] * 8,
        grid=(batch, nt),
        in_specs=[
            pl.BlockSpec((tt, n_rkv), cur_map),
            pl.BlockSpec((sub, n_rkv), prev_map),
            pl.BlockSpec((tt, n_lo), cur_map),
            pl.BlockSpec((sub, n_lo), prev_map),
            pl.BlockSpec((None, 1, n_rkv), lambda b, i: (b, 0, 0)),
            pl.BlockSpec((None, 1, n_lo), lambda b, i: (b, 0, 0)),
        ] + [_const_spec(c) for c in consts],
        out_specs=[out_spec] * 8,
        compiler_params=_params("parallel", "arbitrary"),
        name="rwkv_prep_seq",
    )(p_rkv, p_rkv, p_lo, p_lo, sp_rkv, sp_lo, *consts)


def _rwkv_prep_step(p_rkv, prev_rkv, p_lo, prev_lo, consts, tb):
    m, n_rkv = p_rkv.shape
    n_lo = p_lo.shape[1]
    d_rwkv = n_rkv // 3
    out_sds = jax.ShapeDtypeStruct((m, d_rwkv), F32)
    row = lambda i: (i, 0)
    return pl.pallas_call(
        _rwkv_prep_step_kernel,
        out_shape=[out_sds] * 8,
        grid=(m // tb,),
        in_specs=[
            pl.BlockSpec((tb, n_rkv), row), pl.BlockSpec((tb, n_rkv), row),
            pl.BlockSpec((tb, n_lo), row), pl.BlockSpec((tb, n_lo), row),
        ] + [_const_spec(c) for c in consts],
        out_specs=[pl.BlockSpec((tb, d_rwkv), row)] * 8,
        compiler_params=_params("parallel"),
        name="rwkv_prep_step",
    )(p_rkv, prev_rkv, p_lo, prev_lo, *consts)


def _rwkv_seq_kernel(r_ref, w_ref, k_ref, kk_ref, b_ref, v_ref, y_ref, s_ref):
    tb, n_k2, lanes = r_ref.shape
    half = lanes // 2

    @pl.when(pl.program_id(0) == 0)
    def _():
        s_ref[...] = jnp.zeros_like(s_ref)

    def step(t, carry):
        v_t = v_ref[t]
        sa = s_ref[0] * kk_ref[t, pl.ds(0, 1), :]
        for k2 in range(1, n_k2):
            sa += s_ref[k2] * kk_ref[t, pl.ds(k2, 1), :]
        sa = sa + pltpu.roll(sa, half, 1)
        y = None
        for k2 in range(n_k2):
            row = pl.ds(k2, 1)
            s_new = s_ref[k2] * w_ref[t, row, :] - b_ref[t, row, :] * sa + k_ref[t, row, :] * v_t
            s_ref[k2] = s_new
            term = s_new * r_ref[t, row, :]
            y = term if y is None else y + term
        y_ref[t] = y + pltpu.roll(y, half, 1)
        return carry

    lax.fori_loop(0, tb, step, 0)


def _rwkv_seq(r_t, w_t, k_t, kk_t, b_t, v_t, tb):
    seq, n_k2, lanes = r_t.shape
    n_v = v_t.shape[1]
    kspec = pl.BlockSpec((tb, n_k2, lanes), lambda i: (i, 0, 0))
    vspec = pl.BlockSpec((tb, n_v, lanes), lambda i: (i, 0, 0))
    return pl.pallas_call(
        _rwkv_seq_kernel,
        out_shape=[jax.ShapeDtypeStruct((seq, n_v, lanes), F32),
                   jax.ShapeDtypeStruct((n_k2, n_v, lanes), F32)],
        grid=(seq // tb,),
        in_specs=[kspec, kspec, kspec, kspec, kspec, vspec],
        out_specs=[vspec, pl.BlockSpec((n_k2, n_v, lanes), lambda i: (0, 0, 0))],
        compiler_params=_params("arbitrary"),
        name="rwkv_seq",
    )(r_t, w_t, k_t, kk_t, b_t, v_t)


def _rwkv_step_kernel(s_ref, r_ref, w_ref, k_ref, kk_ref, b_ref, v_ref, y_ref, so_ref):
    n = s_ref.shape[-1]
    eye = (lax.broadcasted_iota(jnp.int32, (n, n), 0) == lax.broadcasted_iota(jnp.int32, (n, n), 1)).astype(F32)
    s = s_ref[...]
    v_col = jnp.sum(eye * v_ref[...], axis=-1, keepdims=True)
    sa = jnp.sum(s * kk_ref[...], axis=-1, keepdims=True)
    s_new = s * w_ref[...] - sa * b_ref[...] + v_col * k_ref[...]
    so_ref[...] = s_new
    y_col = jnp.sum(s_new * r_ref[...], axis=-1, keepdims=True)
    y_ref[...] = jnp.sum(eye * y_col, axis=-2, keepdims=True)


def _rwkv_step(state, r, w, k, kk, b, v, sb):
    n_s, n_h, n, _ = state.shape
    sspec = pl.BlockSpec((sb, n_h, n, n), lambda i: (i, 0, 0, 0))
    vspec = pl.BlockSpec((sb, n_h, 1, n), lambda i: (i, 0, 0, 0))
    return pl.pallas_call(
        _rwkv_step_kernel,
        out_shape=[jax.ShapeDtypeStruct((n_s, n_h, 1, n), F32), jax.ShapeDtypeStruct(state.shape, F32)],
        grid=(n_s // sb,),
        in_specs=[sspec] + [vspec] * 6,
        out_specs=[vspec, sspec],
        compiler_params=_params("parallel"),
        name="rwkv_step",
    )(state, r, w, k, kk, b, v)


def _rwkv_post_kernel(y_ref, g_ref, bonus_ref, gng_ref, gnb_ref, beta_ref, gsel_ref, gselt_ref, o_ref):
    y = y_ref[...]
    gsel = gsel_ref[...]
    gselt = gselt_ref[...]
    mu = _head_sum(y, gsel, gselt) * (1.0 / HEAD_DIM)
    yc = y - mu
    var = _head_sum(yc * yc, gsel, gselt) * (1.0 / HEAD_DIM)
    yn = yc * lax.rsqrt(var + GN_EPS) * gng_ref[...] + gnb_ref[...]
    o_ref[...] = (yn + bonus_ref[...]) * g_ref[...] * beta_ref[...]


def _rwkv_post(y, g, bonus, gn_g, gn_b, beta, gsel, gselt, tt):
    m, d = y.shape
    row = pl.BlockSpec((tt, d), lambda i: (i, 0))
    consts = (gn_g, gn_b, beta, gsel, gselt)
    return pl.pallas_call(
        _rwkv_post_kernel,
        out_shape=jax.ShapeDtypeStruct((m, d), F32),
        grid=(m // tt,),
        in_specs=[row, row, row] + [_const_spec(c) for c in consts],
        out_specs=row,
        compiler_params=_params("parallel"),
        name="rwkv_post",
    )(y, g, bonus, *consts)


CONV_HALO = 32


def _glu(pc, d_conv):
    return pc[:, :d_conv] * jax.nn.sigmoid(pc[:, d_conv:])


def _conv_seq_kernel(pc_ref, halo_ref, cw_ref, cb_ref, lng_ref, lnb_ref, beta_ref, o_ref, st_ref, ubuf_ref,
                     *, row_chunk):
    tt, d_conv = o_ref.shape
    i = pl.program_id(1)
    n_hist = CONV_WIDTH - 1
    pad = CONV_HALO - n_hist
    u_halo = _glu(halo_ref[...], d_conv)
    ubuf_ref[0:CONV_HALO, :] = jnp.where(i == 0, jnp.zeros_like(u_halo), u_halo)
    ubuf_ref[CONV_HALO:, :] = _glu(pc_ref[...], d_conv)

    for base in range(0, tt, row_chunk):
        acc = ubuf_ref[base + pad:base + pad + row_chunk, :] * cw_ref[0:1, :]
        for w in range(1, CONV_WIDTH):
            acc += ubuf_ref[base + pad + w:base + pad + w + row_chunk, :] * cw_ref[w:w + 1, :]
        c = _ln_rows(acc + cb_ref[...], lng_ref[...], lnb_ref[...], LN_EPS)
        o_ref[base:base + row_chunk, :] = c * jax.nn.sigmoid(c) * beta_ref[...]

    @pl.when(i == pl.num_programs(1) - 1)
    def _():
        st_ref[...] = ubuf_ref[CONV_HALO + tt - n_hist:, :]


def _conv_seq(pc, cw, cb, ln_g, ln_b, beta, batch, seq, tt):
    m, two_d = pc.shape
    d_conv = two_d // 2
    nt = seq // tt
    hb = tt // CONV_HALO
    consts = (cw, cb, ln_g, ln_b, beta)
    kern = functools.partial(_conv_seq_kernel, row_chunk=16)
    return pl.pallas_call(
        kern,
        out_shape=[jax.ShapeDtypeStruct((m, d_conv), F32),
                   jax.ShapeDtypeStruct((batch, CONV_WIDTH - 1, d_conv), F32)],
        grid=(batch, nt),
        in_specs=[
            pl.BlockSpec((tt, two_d), lambda b, i: (b * nt + i, 0)),
            pl.BlockSpec((CONV_HALO, two_d), lambda b, i: (jnp.maximum((b * nt + i) * hb - 1, 0), 0)),
        ] + [_const_spec(c) for c in consts],
        out_specs=[pl.BlockSpec((tt, d_conv), lambda b, i: (b * nt + i, 0)),
                   pl.BlockSpec((None, CONV_WIDTH - 1, d_conv), lambda b, i: (b, 0, 0))],
        scratch_shapes=[pltpu.VMEM((CONV_HALO + tt, d_conv), F32)],
        compiler_params=_params("parallel", "arbitrary"),
        name="conv_seq",
    )(pc, pc, *consts)


def _conv_step_kernel(pc_ref, st_ref, cw_ref, cb_ref, lng_ref, lnb_ref, beta_ref, o_ref, so_ref):
    d_conv = o_ref.shape[-1]
    n_hist = CONV_WIDTH - 1
    pc = pc_ref[...]
    u = pc[:, :, :d_conv] * jax.nn.sigmoid(pc[:, :, d_conv:])
    st = st_ref[...]
    c = (jnp.sum(st * cw_ref[0:n_hist, :], axis=1, keepdims=True)
         + u * cw_ref[n_hist:CONV_WIDTH, :] + cb_ref[...])
    c = _ln_rows(c, lng_ref[...], lnb_ref[...], LN_EPS)
    o_ref[...] = c * jax.nn.sigmoid(c) * beta_ref[...]
    so_ref[:, 0:n_hist - 1, :] = st[:, 1:, :]
    so_ref[:, n_hist - 1:n_hist, :] = u


def _conv_step(pc, state, cw, cb, ln_g, ln_b, beta, sb):
    n_s, _, two_d = pc.shape
    d_conv = two_d // 2
    consts = (cw, cb, ln_g, ln_b, beta)
    sspec = pl.BlockSpec((sb, CONV_WIDTH - 1, d_conv), lambda i: (i, 0, 0))
    return pl.pallas_call(
        _conv_step_kernel,
        out_shape=[jax.ShapeDtypeStruct((n_s, 1, d_conv), F32), jax.ShapeDtypeStruct(state.shape, F32)],
        grid=(n_s // sb,),
        in_specs=[pl.BlockSpec((sb, 1, two_d), lambda i: (i, 0, 0)), sspec] + [_const_spec(c) for c in consts],
        out_specs=[pl.BlockSpec((sb, 1, d_conv), lambda i: (i, 0, 0)), sspec],
        compiler_params=_params("parallel"),
        name="conv_step",
    )(pc, state, *consts)


def _attn_seq_kernel(q_ref, k_ref, v_ref, o_ref, *, scale):
    d = q_ref.shape[-1]
    dh = d // N_MEM_HEADS
    for h in range(N_MEM_HEADS):
        cols = slice(h * dh, (h + 1) * dh)
        q = q_ref[:, cols].astype(BF16)
        k = k_ref[:, cols].astype(BF16)
        s = lax.dot_general(q, k, (((1,), (1,)), ((), ())), preferred_element_type=F32) * scale
        s = s - jnp.max(s, axis=-1, keepdims=True)
        e = jnp.exp(s)
        p = e / jnp.sum(e, axis=-1, keepdims=True)
        o_ref[:, cols] = _dot(p.astype(BF16), v_ref[:, cols].astype(BF16))


def _attn_seq(q, mk, mv, batch, seq, tq):
    m, d = q.shape
    n_mem = mk.shape[1]
    nt = seq // tq
    kern = functools.partial(_attn_seq_kernel, scale=1.0 / math.sqrt(d // N_MEM_HEADS))
    kv_spec = pl.BlockSpec((None, n_mem, d), lambda b, i: (b, 0, 0))
    return pl.pallas_call(
        kern,
        out_shape=jax.ShapeDtypeStruct((m, d), F32),
        grid=(batch, nt),
        in_specs=[pl.BlockSpec((tq, d), lambda b, i: (b * nt + i, 0)), kv_spec, kv_spec],
        out_specs=pl.BlockSpec((tq, d), lambda b, i: (b * nt + i, 0)),
        compiler_params=_params("parallel", "arbitrary"),
        name="attn_seq",
    )(q, mk, mv)


def _attn_step_kernel(q_ref, k_ref, v_ref, o_ref, *, scale):
    d = q_ref.shape[-1]
    dh = d // N_MEM_HEADS
    for h in range(N_MEM_HEADS):
        cols = slice(h * dh, (h + 1) * dh)
        q = q_ref[:, :, cols]
        s = jnp.sum(k_ref[:, :, cols] * q, axis=-1, keepdims=True) * scale
        s = s - jnp.max(s, axis=1, keepdims=True)
        e = jnp.exp(s)
        p = e / jnp.sum(e, axis=1, keepdims=True)
        o_ref[:, :, cols] = jnp.sum(p * v_ref[:, :, cols], axis=1, keepdims=True)


def _attn_step(q, mk, mv, sb):
    n_s, _, d = q.shape
    n_mem = mk.shape[1]
    kern = functools.partial(_attn_step_kernel, scale=1.0 / math.sqrt(d // N_MEM_HEADS))
    qspec = pl.BlockSpec((sb, 1, d), lambda i: (i, 0, 0))
    kvspec = pl.BlockSpec((sb, n_mem, d), lambda i: (i, 0, 0))
    return pl.pallas_call(
        kern,
        out_shape=jax.ShapeDtypeStruct((n_s, 1, d), F32),
        grid=(n_s // sb,),
        in_specs=[qspec, kvspec, kvspec],
        out_specs=qspec,
        compiler_params=_params("parallel"),
        name="attn_step",
    )(q, mk, mv)


def _head_selectors(d_rwkv):
    n_heads = d_rwkv // HEAD_DIM
    ch = jnp.arange(d_rwkv)[:, None] // HEAD_DIM
    col = jnp.arange(V7X_LANES)[None, :]
    gsel = (ch == col).astype(BF16)
    assert n_heads <= V7X_LANES
    return gsel, gsel.T


def kernel(x_prompt, x_sample, mem_prompt, state_shift, state_conv, state_wkv, cache_mem_k, cache_mem_v,
           ffn1_w1, ffn1_w3, ffn1_w2, ln1_g, ln1_b, w_in, mu_shift, w0, w2_decay, a0, a2_iclr, g2_gate,
           k_k, k_a, r_k, gn_g, gn_b, conv_w, conv_b, conv_ln_g, conv_ln_b, beta_rwkv, beta_conv, w_out,
           ln2_g, ln2_b, w_mq, w_mk, w_mv, w_mo, ln3_g, ln3_b, ffn2_w1, ffn2_w3, ffn2_w2, ln4_g, ln4_b):
    depth = ffn1_w1.shape[0]
    assert depth == 1
    batch, seq, d = x_prompt.shape
    n_s = x_sample.shape[0]
    assert x_sample.shape[1] == 1
    n_mem = mem_prompt.shape[1]
    d_rwkv = w0.shape[-1]
    d_conv = conv_b.shape[-1]
    n_heads = d_rwkv // HEAD_DIM
    n_rkv = 3 * d_rwkv
    shift_cols = mu_shift.shape[-1]
    alpha = (2.0 * depth) ** 0.25
    l = 0

    w_rkv = w_in[l, :, :n_rkv]
    w_lo = w_in[l, :, n_rkv:shift_cols]
    w_cv = w_in[l, :, shift_cols:]
    mu_rkv = mu_shift[l:l + 1, :n_rkv]
    mu_lo = mu_shift[l:l + 1, n_rkv:]
    gsel, gselt = _head_selectors(d_rwkv)
    prep_consts = (mu_rkv, mu_lo, w0[l:l + 1], a0[l:l + 1], k_k[l:l + 1], k_a[l:l + 1],
                   r_k[l].reshape(1, d_rwkv), w2_decay[l], a2_iclr[l], g2_gate[l], gsel, gselt)
    cw = conv_w[l].reshape(CONV_WIDTH, d_conv)
    conv_consts = (cw, conv_b[l:l + 1], conv_ln_g[l:l + 1], conv_ln_b[l:l + 1], beta_conv[l:l + 1])
    post_consts = (gn_g[l:l + 1], gn_b[l:l + 1], beta_rwkv[l:l + 1], gsel, gselt)

    def trunk_front(x, tm):
        x1 = _ffn_ln(x, ffn1_w1[l], ffn1_w3[l], ffn1_w2[l], ln1_g[l:l + 1], ln1_b[l:l + 1], alpha, tm, 256)
        p_rkv = _mm(x1, w_rkv, tm, 512)
        p_lo = _mm(x1, w_lo, tm, w_lo.shape[1])
        p_cv = _mm(x1, w_cv, tm, 512)
        return x1, p_rkv, p_lo, p_cv

    def trunk_back(x1, o_a, o_b, attend, tm):
        x2 = _mm_res_ln([o_a, o_b], w_out[l], x1, ln2_g[l:l + 1], ln2_b[l:l + 1], alpha, tm, 512)
        q = _mm(x2, w_mq[l], tm, 512)
        att = attend(q)
        x3 = _mm_res_ln([att], w_mo[l], x2, ln3_g[l:l + 1], ln3_b[l:l + 1], alpha, tm, 512)
        return _ffn_ln(x3, ffn2_w1[l], ffn2_w3[l], ffn2_w2[l], ln4_g[l:l + 1], ln4_b[l:l + 1], alpha, tm, 256)

    m_p = batch * seq
    tm_p = 1024
    x1, p_rkv, p_lo, p_cv = trunk_front(x_prompt.reshape(m_p, d), tm_p)
    zeros_rkv = jnp.zeros((batch, 1, n_rkv), F32)
    zeros_lo = jnp.zeros((batch, 1, shift_cols - n_rkv), F32)
    r, w, k, v, kk, bb, g, bonus = _rwkv_prep_seq(p_rkv, p_lo, zeros_rkv, zeros_lo, prep_consts, batch, seq, 256)

    n_k2 = HEAD_DIM // 2

    def to_lanes_k(t):
        t = t.reshape(batch, seq, n_heads, 2, n_k2)
        return t.transpose(1, 4, 3, 0, 2).reshape(seq, n_k2, 2 * batch * n_heads)

    def to_lanes_v(t):
        t = t.reshape(batch, seq, n_heads, HEAD_DIM).transpose(1, 3, 0, 2).reshape(seq, HEAD_DIM, batch * n_heads)
        return jnp.concatenate([t, t], axis=-1)

    assert 2 * batch * n_heads == V7X_LANES
    y_t, s_t = _rwkv_seq(to_lanes_k(r), to_lanes_k(w), to_lanes_k(k), to_lanes_k(kk), to_lanes_k(bb),
                         to_lanes_v(v), 64)
    y = y_t[:, :, :batch * n_heads].reshape(seq, HEAD_DIM, batch, n_heads).transpose(2, 0, 3, 1).reshape(m_p, d_rwkv)
    wkv_p = s_t.reshape(n_k2, HEAD_DIM, 2, batch, n_heads).transpose(3, 4, 1, 2, 0).reshape(
        batch, n_heads, HEAD_DIM, HEAD_DIM)
    o_a = _rwkv_post(y, g, bonus, *post_consts, 256)
    o_b, conv_p = _conv_seq(p_cv, *conv_consts, batch, seq, 256)

    mem2 = mem_prompt.reshape(batch * n_mem, d)
    mk_p = _mm(mem2, w_mk[l], batch * n_mem, 512)
    mv_p = _mm(mem2, w_mv[l], batch * n_mem, 512)
    attend_p = lambda q: _attn_seq(q, mk_p.reshape(batch, n_mem, d), mv_p.reshape(batch, n_mem, d), batch, seq, 512)
    y_prompt = trunk_back(x1, o_a, o_b, attend_p, tm_p).reshape(batch, seq, d)
    shift_p = jnp.concatenate([p_rkv.reshape(batch, seq, n_rkv)[:, -1], p_lo.reshape(batch, seq, -1)[:, -1]], axis=-1)
    dh = d // N_MEM_HEADS

    xs1, ps_rkv, ps_lo, ps_cv = trunk_front(x_sample.reshape(n_s, d), n_s)
    prev = state_shift[l]
    rs, ws, ks, vs, kks, bs, gs, bonus_s = _rwkv_prep_step(ps_rkv, prev[:, :n_rkv], ps_lo, prev[:, n_rkv:],
                                                          prep_consts, n_s)
    hd = lambda t: t.reshape(n_s, n_heads, 1, HEAD_DIM)
    ys, wkv_s = _rwkv_step(state_wkv[l], hd(rs), hd(ws), hd(ks), hd(kks), hd(bs), hd(vs), 8)
    o_as = _rwkv_post(ys.reshape(n_s, d_rwkv), gs, bonus_s, *post_consts, n_s)
    o_bs, conv_s = _conv_step(ps_cv.reshape(n_s, 1, 2 * d_conv), state_conv[l], *conv_consts, 8)
    o_bs = o_bs.reshape(n_s, d_conv)
    attend_s = lambda q: _attn_step(q.reshape(n_s, 1, d), cache_mem_k[l].reshape(n_s, n_mem, d),
                                    cache_mem_v[l].reshape(n_s, n_mem, d), 2).reshape(n_s, d)
    y_sample = trunk_back(xs1, o_as, o_bs, attend_s, n_s).reshape(n_s, 1, d)
    shift_s = jnp.concatenate([ps_rkv, ps_lo], axis=-1)

    return (y_prompt, y_sample,
            shift_p[None], conv_p[None], wkv_p[None],
            mk_p.reshape(1, batch, n_mem, N_MEM_HEADS, dh), mv_p.reshape(1, batch, n_mem, N_MEM_HEADS, dh),
            shift_s[None], conv_s[None], wkv_s[None])
```

```python
import functools
import math

import jax
import jax.numpy as jnp
from jax import lax
from jax.experimental import pallas as pl
from jax.experimental.pallas import tpu as pltpu

F32 = jnp.float32
BF16 = jnp.bfloat16

V7X_LANES = 128
V7X_SUBLANES = 8
V7X_VMEM_LIMIT_BYTES = 56 * 1024 * 1024

HEAD_DIM = 64
CONV_WIDTH = 31
DECAY_LORA = 64
ICLR_LORA = 64
GATE_LORA = 160
N_MEM_HEADS = 4
LN_EPS = 1e-5
GN_EPS = 64e-5


def _params(*sem):
    return pltpu.CompilerParams(dimension_semantics=sem, vmem_limit_bytes=V7X_VMEM_LIMIT_BYTES)


def _resident(block_shape, index_map):
    return pl.BlockSpec(block_shape, index_map, pipeline_mode=pl.Buffered(1))


def _ln_rows(y, g, b, eps):
    mu = jnp.mean(y, axis=-1, keepdims=True)
    yc = y - mu
    var = jnp.mean(yc * yc, axis=-1, keepdims=True)
    return yc * lax.rsqrt(var + eps) * g + b


def _dot(a, b):
    return jnp.dot(a, b, preferred_element_type=F32)


def _split3_dot(x, g):
    hi = x.astype(BF16)
    r1 = x - hi.astype(F32)
    mid = r1.astype(BF16)
    lo = (r1 - mid.astype(F32)).astype(BF16)
    return _dot(hi, g) + _dot(mid, g) + _dot(lo, g)


def _head_sum(x, gsel, gselt):
    return _split3_dot(_split3_dot(x, gsel), gselt)


def _ffn_ln_kernel(x_ref, *refs, alpha, n_chunk, row_chunk, cast_x):
    if cast_x:
        w1_ref, w3_ref, w2_ref, g_ref, b_ref, o_ref, ob_ref, xb_ref = refs
    else:
        xb_ref, w1_ref, w3_ref, w2_ref, g_ref, b_ref, o_ref, ob_ref = refs
    j = pl.program_id(1)
    tm, d = o_ref.shape

    if cast_x:
        @pl.when(j == 0)
        def _():
            xb_ref[...] = x_ref[...].astype(BF16)

    xb = xb_ref[...]
    h1 = _dot(xb, w1_ref[...].astype(BF16))
    h3 = _dot(xb, w3_ref[...].astype(BF16))
    hb = (h1 * jax.nn.sigmoid(h1) * h3).astype(BF16)
    w2b = w2_ref[...].astype(BF16)
    cw = d // n_chunk
    for c in range(n_chunk):
        part = _dot(hb, w2b[:, c * cw:(c + 1) * cw])

        @pl.when(j == 0)
        def _():
            o_ref[:, c * cw:(c + 1) * cw] = part

        @pl.when(j > 0)
        def _():
            o_ref[:, c * cw:(c + 1) * cw] += part

    @pl.when(j == pl.num_programs(1) - 1)
    def _():
        def body(r, carry):
            rows = pl.ds(pl.multiple_of(r * row_chunk, row_chunk), row_chunk)
            y = alpha * x_ref[rows, :] + 0.5 * o_ref[rows, :]
            y = _ln_rows(y, g_ref[...], b_ref[...], LN_EPS)
            o_ref[rows, :] = y
            ob_ref[rows, :] = y.astype(BF16)
            return carry

        lax.fori_loop(0, tm // row_chunk, body, 0)


def _ffn_ln(x, xb, w1, w3, w2, g, b, alpha, tm, tf):
    m, d = x.shape
    f = w1.shape[1]
    cast_x = xb is None
    kern = functools.partial(_ffn_ln_kernel, alpha=alpha, n_chunk=4, row_chunk=min(tm, 128), cast_x=cast_x)
    row_tile = lambda i, j: (i, 0)
    return pl.pallas_call(
        kern,
        out_shape=[jax.ShapeDtypeStruct((m, d), F32), jax.ShapeDtypeStruct((m, d), BF16)],
        grid=(m // tm, f // tf),
        in_specs=[_resident((tm, d), row_tile)] * (1 if cast_x else 2) + [
            pl.BlockSpec((d, tf), lambda i, j: (0, j)),
            pl.BlockSpec((d, tf), lambda i, j: (0, j)),
            pl.BlockSpec((tf, d), lambda i, j: (j, 0)),
            pl.BlockSpec((1, d), lambda i, j: (0, 0)),
            pl.BlockSpec((1, d), lambda i, j: (0, 0)),
        ],
        out_specs=[_resident((tm, d), row_tile), _resident((tm, d), row_tile)],
        scratch_shapes=[pltpu.VMEM((tm, d), BF16)] if cast_x else [],
        compiler_params=_params("parallel", "arbitrary"),
        name="ffn_ln",
    )(*((x,) if cast_x else (x, xb)), w1, w3, w2, g, b)


def _mm_kernel(x_ref, w_ref, o_ref):
    o_ref[...] = _dot(x_ref[...].astype(BF16), w_ref[...].astype(BF16)).astype(o_ref.dtype)


def _mm(x, w, tm, tn, out_dtype=F32):
    m, k = x.shape
    n = w.shape[1]
    return pl.pallas_call(
        _mm_kernel,
        out_shape=jax.ShapeDtypeStruct((m, n), out_dtype),
        grid=(m // tm, n // tn),
        in_specs=[
            pl.BlockSpec((tm, k), lambda i, j: (i, 0)),
            pl.BlockSpec((k, tn), lambda i, j: (0, j)),
        ],
        out_specs=pl.BlockSpec((tm, tn), lambda i, j: (i, j)),
        compiler_params=_params("parallel", "arbitrary"),
        name="mm",
    )(x, w)


def _mm_res_ln_kernel(*refs, n_a, alpha, row_chunk):
    a_refs = refs[:n_a]
    w_refs = refs[n_a:2 * n_a]
    res_ref, g_ref, b_ref, o_ref, ob_ref, acc_ref = refs[2 * n_a:]
    j = pl.program_id(1)
    nj, tm, tn = acc_ref.shape

    acc = _dot(a_refs[0][...], w_refs[0][...].astype(BF16))
    for a_ref, w_ref in zip(a_refs[1:], w_refs[1:]):
        acc += _dot(a_ref[...], w_ref[...].astype(BF16))
    acc_ref[j] = acc

    @pl.when(j == nj - 1)
    def _():
        d = nj * tn

        def body(r, carry):
            rows = pl.ds(pl.multiple_of(r * row_chunk, row_chunk), row_chunk)
            ys = [alpha * res_ref[rows, c * tn:(c + 1) * tn] + acc_ref[c, rows, :] for c in range(nj)]
            mu = sum(jnp.sum(y, axis=-1, keepdims=True) for y in ys) / d
            ycs = [y - mu for y in ys]
            var = sum(jnp.sum(yc * yc, axis=-1, keepdims=True) for yc in ycs) / d
            inv = lax.rsqrt(var + LN_EPS)
            for c in range(nj):
                cols = slice(c * tn, (c + 1) * tn)
                y = ycs[c] * inv * g_ref[:, cols] + b_ref[:, cols]
                o_ref[rows, cols] = y
                ob_ref[rows, cols] = y.astype(BF16)
            return carry

        lax.fori_loop(0, tm // row_chunk, body, 0)


def _mm_res_ln(a_list, w, res, g, b, alpha, tm, tn):
    m, d = res.shape
    n_a = len(a_list)
    ka = a_list[0].shape[1]
    nj = d // tn
    kern = functools.partial(_mm_res_ln_kernel, n_a=n_a, alpha=alpha, row_chunk=min(tm, 128))
    row_tile = lambda i, j: (i, 0)
    in_specs = [pl.BlockSpec((tm, ka), row_tile) for _ in range(n_a)]
    in_specs += [pl.BlockSpec((ka, tn), functools.partial(lambda i, j, r: (r, j), r=r)) for r in range(n_a)]
    in_specs += [
        _resident((tm, d), row_tile),
        pl.BlockSpec((1, d), lambda i, j: (0, 0)),
        pl.BlockSpec((1, d), lambda i, j: (0, 0)),
    ]
    return pl.pallas_call(
        kern,
        out_shape=[jax.ShapeDtypeStruct((m, d), F32), jax.ShapeDtypeStruct((m, d), BF16)],
        grid=(m // tm, nj),
        in_specs=in_specs,
        out_specs=[_resident((tm, d), row_tile), _resident((tm, d), row_tile)],
        scratch_shapes=[pltpu.VMEM((nj, tm, tn), F32)],
        compiler_params=_params("parallel", "arbitrary"),
        name="mm_res_ln",
    )(*a_list, *([w] * n_a), res, g, b)


def _shift_rows(cur, first_row):
    rolled = pltpu.roll(cur, 1, 0)
    row_id = lax.broadcasted_iota(jnp.int32, cur.shape, 0)
    return jnp.where(row_id == 0, first_row, rolled)


def _rwkv_prep_math(cur_rkv, prev_rkv, cur_lo, prev_lo, mu_rkv, mu_lo, w0, a0, k_k, k_a, r_k,
                    w2d, a2, g2, gsel, gselt, d_rwkv):
    pm = cur_rkv + mu_rkv * (prev_rkv - cur_rkv)
    pl_ = cur_lo + mu_lo * (prev_lo - cur_lo)
    r = pm[:, :d_rwkv]
    k = pm[:, d_rwkv:2 * d_rwkv]
    v = pm[:, 2 * d_rwkv:]
    wd = pl_[:, :DECAY_LORA]
    ad = pl_[:, DECAY_LORA:DECAY_LORA + ICLR_LORA]
    gd = pl_[:, DECAY_LORA + ICLR_LORA:]
    z = w0 + _dot(jnp.tanh(wd).astype(BF16), w2d.astype(BF16))
    w_log = -jax.nn.softplus(-z) - 0.5
    decay = jnp.exp(-jnp.exp(w_log))
    a = jax.nn.sigmoid(a0 + _dot(ad.astype(BF16), a2.astype(BF16)))
    g = _dot(jax.nn.sigmoid(gd).astype(BF16), g2.astype(BF16))
    kk = k * k_k
    norm = jnp.sqrt(_head_sum(kk * kk, gsel, gselt))
    kk = kk / jnp.maximum(norm, 1e-12)
    kmod = k * (1.0 + (a - 1.0) * k_a)
    bonus = _head_sum(r * kmod * r_k, gsel, gselt) * v
    return r, decay, kmod, v, kk, kk * a, g, bonus


def _rwkv_prep_seq_kernel(cur_rkv_ref, prev8_rkv_ref, cur_lo_ref, prev8_lo_ref, sp_rkv_ref, sp_lo_ref,
                          mu_rkv_ref, mu_lo_ref, w0_ref, a0_ref, kk_ref, ka_ref, rk_ref,
                          w2d_ref, a2_ref, g2_ref, gsel_ref, gselt_ref,
                          r_o, w_o, k_o, v_o, kk_o, b_o, g_o, bonus_o):
    first = pl.program_id(1) == 0
    cur_rkv = cur_rkv_ref[...]
    cur_lo = cur_lo_ref[...]
    last = V7X_SUBLANES - 1
    first_rkv = jnp.where(first, sp_rkv_ref[...], prev8_rkv_ref[last:last + 1, :])
    first_lo = jnp.where(first, sp_lo_ref[...], prev8_lo_ref[last:last + 1, :])
    outs = _rwkv_prep_math(
        cur_rkv, _shift_rows(cur_rkv, first_rkv), cur_lo, _shift_rows(cur_lo, first_lo),
        mu_rkv_ref[...], mu_lo_ref[...], w0_ref[...], a0_ref[...], kk_ref[...], ka_ref[...], rk_ref[...],
        w2d_ref[...], a2_ref[...], g2_ref[...], gsel_ref[...], gselt_ref[...], r_o.shape[-1])
    for o_ref, val in zip((r_o, w_o, k_o, v_o, kk_o, b_o, g_o, bonus_o), outs):
        o_ref[...] = val


def _rwkv_prep_step_kernel(cur_rkv_ref, prev_rkv_ref, cur_lo_ref, prev_lo_ref,
                           mu_rkv_ref, mu_lo_ref, w0_ref, a0_ref, kk_ref, ka_ref, rk_ref,
                           w2d_ref, a2_ref, g2_ref, gsel_ref, gselt_ref,
                           r_o, w_o, k_o, v_o, kk_o, b_o, g_o, bonus_o):
    outs = _rwkv_prep_math(
        cur_rkv_ref[...], prev_rkv_ref[...], cur_lo_ref[...], prev_lo_ref[...],
        mu_rkv_ref[...], mu_lo_ref[...], w0_ref[...], a0_ref[...], kk_ref[...], ka_ref[...], rk_ref[...],
        w2d_ref[...], a2_ref[...], g2_ref[...], gsel_ref[...], gselt_ref[...], r_o.shape[-1])
    for o_ref, val in zip((r_o, w_o, k_o, v_o, kk_o, b_o, g_o, bonus_o), outs):
        o_ref[...] = val


def _const_spec(arr):
    nd = arr.ndim
    return pl.BlockSpec(arr.shape, lambda *_: (0,) * nd)


def _rwkv_prep_seq(p_rkv, p_lo, sp_rkv, sp_lo, consts, batch, seq, tt):
    m, n_rkv = p_rkv.shape
    n_lo = p_lo.shape[1]
    d_rwkv = n_rkv // 3
    nt = seq // tt
    sub = V7X_SUBLANES

    def cur_map(b, i):
        return (b * nt + i, 0)

    def prev_map(b, i):
        return (jnp.maximum((b * seq + i * tt) // sub - 1, 0), 0)

    out_sds = jax.ShapeDtypeStruct((m, d_rwkv), F32)
    out_spec = pl.BlockSpec((tt, d_rwkv), cur_map)
    return pl.pallas_call(
        _rwkv_prep_seq_kernel,
        out_shape=[out_sds] * 8,
        grid=(batch, nt),
        in_specs=[
            pl.BlockSpec((tt, n_rkv), cur_map),
            pl.BlockSpec((sub, n_rkv), prev_map),
            pl.BlockSpec((tt, n_lo), cur_map),
            pl.BlockSpec((sub, n_lo), prev_map),
            pl.BlockSpec((None, 1, n_rkv), lambda b, i: (b, 0, 0)),
            pl.BlockSpec((None, 1, n_lo), lambda b, i: (b, 0, 0)),
        ] + [_const_spec(c) for c in consts],
        out_specs=[out_spec] * 8,
        compiler_params=_params("parallel", "arbitrary"),
        name="rwkv_prep_seq",
    )(p_rkv, p_rkv, p_lo, p_lo, sp_rkv, sp_lo, *consts)


def _rwkv_prep_step(p_rkv, prev_rkv, p_lo, prev_lo, consts, tb):
    m, n_rkv = p_rkv.shape
    n_lo = p_lo.shape[1]
    d_rwkv = n_rkv // 3
    out_sds = jax.ShapeDtypeStruct((m, d_rwkv), F32)
    row = lambda i: (i, 0)
    return pl.pallas_call(
        _rwkv_prep_step_kernel,
        out_shape=[out_sds] * 8,
        grid=(m // tb,),
        in_specs=[
            pl.BlockSpec((tb, n_rkv), row), pl.BlockSpec((tb, n_rkv), row),
            pl.BlockSpec((tb, n_lo), row), pl.BlockSpec((tb, n_lo), row),
        ] + [_const_spec(c) for c in consts],
        out_specs=[pl.BlockSpec((tb, d_rwkv), row)] * 8,
        compiler_params=_params("parallel"),
        name="rwkv_prep_step",
    )(p_rkv, prev_rkv, p_lo, prev_lo, *consts)


def _rwkv_seq_kernel(r_ref, w_ref, k_ref, kk_ref, b_ref, v_ref, y_ref, s_ref):
    tb, n_k2, lanes = r_ref.shape
    half = lanes // 2

    @pl.when(pl.program_id(0) == 0)
    def _():
        s_ref[...] = jnp.zeros_like(s_ref)

    def step(t, carry):
        v_t = v_ref[t]
        sa = s_ref[0] * kk_ref[t, pl.ds(0, 1), :]
        for k2 in range(1, n_k2):
            sa += s_ref[k2] * kk_ref[t, pl.ds(k2, 1), :]
        sa = sa + pltpu.roll(sa, half, 1)
        y = None
        for k2 in range(n_k2):
            row = pl.ds(k2, 1)
            s_new = s_ref[k2] * w_ref[t, row, :] - b_ref[t, row, :] * sa + k_ref[t, row, :] * v_t
            s_ref[k2] = s_new
            term = s_new * r_ref[t, row, :]
            y = term if y is None else y + term
        y_ref[t] = y + pltpu.roll(y, half, 1)
        return carry

    lax.fori_loop(0, tb, step, 0)


def _rwkv_seq(r_t, w_t, k_t, kk_t, b_t, v_t, tb):
    seq, n_k2, lanes = r_t.shape
    n_v = v_t.shape[1]
    kspec = pl.BlockSpec((tb, n_k2, lanes), lambda i: (i, 0, 0))
    vspec = pl.BlockSpec((tb, n_v, lanes), lambda i: (i, 0, 0))
    return pl.pallas_call(
        _rwkv_seq_kernel,
        out_shape=[jax.ShapeDtypeStruct((seq, n_v, lanes), F32),
                   jax.ShapeDtypeStruct((n_k2, n_v, lanes), F32)],
        grid=(seq // tb,),
        in_specs=[kspec, kspec, kspec, kspec, kspec, vspec],
        out_specs=[vspec, pl.BlockSpec((n_k2, n_v, lanes), lambda i: (0, 0, 0))],
        compiler_params=_params("arbitrary"),
        name="rwkv_seq",
    )(r_t, w_t, k_t, kk_t, b_t, v_t)


def _rwkv_step_kernel(s_ref, r_ref, w_ref, k_ref, kk_ref, b_ref, v_ref, y_ref, so_ref):
    n = s_ref.shape[-1]
    eye = (lax.broadcasted_iota(jnp.int32, (n, n), 0) == lax.broadcasted_iota(jnp.int32, (n, n), 1)).astype(F32)
    s = s_ref[...]
    v_col = jnp.sum(eye * v_ref[...], axis=-1, keepdims=True)
    sa = jnp.sum(s * kk_ref[...], axis=-1, keepdims=True)
    s_new = s * w_ref[...] - sa * b_ref[...] + v_col * k_ref[...]
    so_ref[...] = s_new
    y_col = jnp.sum(s_new * r_ref[...], axis=-1, keepdims=True)
    y_ref[...] = jnp.sum(eye * y_col, axis=-2, keepdims=True)


def _rwkv_step(state, r, w, k, kk, b, v, sb):
    n_s, n_h, n, _ = state.shape
    sspec = pl.BlockSpec((sb, n_h, n, n), lambda i: (i, 0, 0, 0))
    vspec = pl.BlockSpec((sb, n_h, 1, n), lambda i: (i, 0, 0, 0))
    return pl.pallas_call(
        _rwkv_step_kernel,
        out_shape=[jax.ShapeDtypeStruct((n_s, n_h, 1, n), F32), jax.ShapeDtypeStruct(state.shape, F32)],
        grid=(n_s // sb,),
        in_specs=[sspec] + [vspec] * 6,
        out_specs=[vspec, sspec],
        compiler_params=_params("parallel"),
        name="rwkv_step",
    )(state, r, w, k, kk, b, v)


def _rwkv_post_kernel(y_ref, g_ref, bonus_ref, gng_ref, gnb_ref, beta_ref, gsel_ref, gselt_ref, o_ref):
    y = y_ref[...]
    gsel = gsel_ref[...]
    gselt = gselt_ref[...]
    mu = _head_sum(y, gsel, gselt) * (1.0 / HEAD_DIM)
    yc = y - mu
    var = _head_sum(yc * yc, gsel, gselt) * (1.0 / HEAD_DIM)
    yn = yc * lax.rsqrt(var + GN_EPS) * gng_ref[...] + gnb_ref[...]
    o_ref[...] = ((yn + bonus_ref[...]) * g_ref[...] * beta_ref[...]).astype(o_ref.dtype)


def _rwkv_post(y, g, bonus, gn_g, gn_b, beta, gsel, gselt, tt):
    m, d = y.shape
    row = pl.BlockSpec((tt, d), lambda i: (i, 0))
    consts = (gn_g, gn_b, beta, gsel, gselt)
    return pl.pallas_call(
        _rwkv_post_kernel,
        out_shape=jax.ShapeDtypeStruct((m, d), BF16),
        grid=(m // tt,),
        in_specs=[row, row, row] + [_const_spec(c) for c in consts],
        out_specs=row,
        compiler_params=_params("parallel"),
        name="rwkv_post",
    )(y, g, bonus, *consts)


CONV_HALO = 32


def _glu(pc, d_conv):
    return pc[:, :d_conv] * jax.nn.sigmoid(pc[:, d_conv:])


def _conv_seq_kernel(pc_ref, halo_ref, cw_ref, cb_ref, lng_ref, lnb_ref, beta_ref, o_ref, st_ref, ubuf_ref,
                     *, row_chunk):
    tt, d_conv = o_ref.shape
    i = pl.program_id(1)
    n_hist = CONV_WIDTH - 1
    pad = CONV_HALO - n_hist
    u_halo = _glu(halo_ref[...], d_conv)
    ubuf_ref[0:CONV_HALO, :] = jnp.where(i == 0, jnp.zeros_like(u_halo), u_halo)
    ubuf_ref[CONV_HALO:, :] = _glu(pc_ref[...], d_conv)

    for base in range(0, tt, row_chunk):
        acc = ubuf_ref[base + pad:base + pad + row_chunk, :] * cw_ref[0:1, :]
        for w in range(1, CONV_WIDTH):
            acc += ubuf_ref[base + pad + w:base + pad + w + row_chunk, :] * cw_ref[w:w + 1, :]
        c = _ln_rows(acc + cb_ref[...], lng_ref[...], lnb_ref[...], LN_EPS)
        o_ref[base:base + row_chunk, :] = (c * jax.nn.sigmoid(c) * beta_ref[...]).astype(o_ref.dtype)

    @pl.when(i == pl.num_programs(1) - 1)
    def _():
        st_ref[...] = ubuf_ref[CONV_HALO + tt - n_hist:, :]


def _conv_seq(pc, cw, cb, ln_g, ln_b, beta, batch, seq, tt):
    m, two_d = pc.shape
    d_conv = two_d // 2
    nt = seq // tt
    hb = tt // CONV_HALO
    consts = (cw, cb, ln_g, ln_b, beta)
    kern = functools.partial(_conv_seq_kernel, row_chunk=16)
    return pl.pallas_call(
        kern,
        out_shape=[jax.ShapeDtypeStruct((m, d_conv), BF16),
                   jax.ShapeDtypeStruct((batch, CONV_WIDTH - 1, d_conv), F32)],
        grid=(batch, nt),
        in_specs=[
            pl.BlockSpec((tt, two_d), lambda b, i: (b * nt + i, 0)),
            pl.BlockSpec((CONV_HALO, two_d), lambda b, i: (jnp.maximum((b * nt + i) * hb - 1, 0), 0)),
        ] + [_const_spec(c) for c in consts],
        out_specs=[pl.BlockSpec((tt, d_conv), lambda b, i: (b * nt + i, 0)),
                   pl.BlockSpec((None, CONV_WIDTH - 1, d_conv), lambda b, i: (b, 0, 0))],
        scratch_shapes=[pltpu.VMEM((CONV_HALO + tt, d_conv), F32)],
        compiler_params=_params("parallel", "arbitrary"),
        name="conv_seq",
    )(pc, pc, *consts)


def _conv_step_kernel(pc_ref, st_ref, cw_ref, cb_ref, lng_ref, lnb_ref, beta_ref, o_ref, so_ref):
    d_conv = o_ref.shape[-1]
    n_hist = CONV_WIDTH - 1
    pc = pc_ref[...]
    u = pc[:, :, :d_conv] * jax.nn.sigmoid(pc[:, :, d_conv:])
    st = st_ref[...]
    c = (jnp.sum(st * cw_ref[0:n_hist, :], axis=1, keepdims=True)
         + u * cw_ref[n_hist:CONV_WIDTH, :] + cb_ref[...])
    c = _ln_rows(c, lng_ref[...], lnb_ref[...], LN_EPS)
    o_ref[...] = c * jax.nn.sigmoid(c) * beta_ref[...]
    so_ref[:, 0:n_hist - 1, :] = st[:, 1:, :]
    so_ref[:, n_hist - 1:n_hist, :] = u


def _conv_step(pc, state, cw, cb, ln_g, ln_b, beta, sb):
    n_s, _, two_d = pc.shape
    d_conv = two_d // 2
    consts = (cw, cb, ln_g, ln_b, beta)
    sspec = pl.BlockSpec((sb, CONV_WIDTH - 1, d_conv), lambda i: (i, 0, 0))
    return pl.pallas_call(
        _conv_step_kernel,
        out_shape=[jax.ShapeDtypeStruct((n_s, 1, d_conv), F32), jax.ShapeDtypeStruct(state.shape, F32)],
        grid=(n_s // sb,),
        in_specs=[pl.BlockSpec((sb, 1, two_d), lambda i: (i, 0, 0)), sspec] + [_const_spec(c) for c in consts],
        out_specs=[pl.BlockSpec((sb, 1, d_conv), lambda i: (i, 0, 0)), sspec],
        compiler_params=_params("parallel"),
        name="conv_step",
    )(pc, state, *consts)


def _attn_seq_kernel(q_ref, k_ref, v_ref, o_ref, *, scale):
    d = q_ref.shape[-1]
    dh = d // N_MEM_HEADS
    for h in range(N_MEM_HEADS):
        cols = slice(h * dh, (h + 1) * dh)
        k = k_ref[:, cols].astype(BF16)
        s = lax.dot_general(q_ref[:, cols], k, (((1,), (1,)), ((), ())), preferred_element_type=F32) * scale
        s = s - jnp.max(s, axis=-1, keepdims=True)
        e = jnp.exp(s)
        p = e / jnp.sum(e, axis=-1, keepdims=True)
        o_ref[:, cols] = _dot(p.astype(BF16), v_ref[:, cols].astype(BF16)).astype(o_ref.dtype)


def _attn_seq(q, mk, mv, batch, seq, tq):
    m, d = q.shape
    n_mem = mk.shape[1]
    nt = seq // tq
    kern = functools.partial(_attn_seq_kernel, scale=1.0 / math.sqrt(d // N_MEM_HEADS))
    kv_spec = pl.BlockSpec((None, n_mem, d), lambda b, i: (b, 0, 0))
    return pl.pallas_call(
        kern,
        out_shape=jax.ShapeDtypeStruct((m, d), BF16),
        grid=(batch, nt),
        in_specs=[pl.BlockSpec((tq, d), lambda b, i: (b * nt + i, 0)), kv_spec, kv_spec],
        out_specs=pl.BlockSpec((tq, d), lambda b, i: (b * nt + i, 0)),
        compiler_params=_params("parallel", "arbitrary"),
        name="attn_seq",
    )(q, mk, mv)


def _attn_step_kernel(q_ref, k_ref, v_ref, o_ref, *, scale):
    s = jnp.sum(k_ref[...] * q_ref[...], axis=-1, keepdims=True) * scale
    s = s - jnp.max(s, axis=1, keepdims=True)
    e = jnp.exp(s)
    p = e / jnp.sum(e, axis=1, keepdims=True)
    o_ref[...] = jnp.sum(p * v_ref[...], axis=1, keepdims=True)


def _attn_step(q, mk, mv, sb):
    n_s, _, n_h, dh = q.shape
    n_mem = mk.shape[1]
    kern = functools.partial(_attn_step_kernel, scale=1.0 / math.sqrt(dh))
    qspec = pl.BlockSpec((sb, 1, n_h, dh), lambda i: (i, 0, 0, 0))
    kvspec = pl.BlockSpec((sb, n_mem, n_h, dh), lambda i: (i, 0, 0, 0))
    return pl.pallas_call(
        kern,
        out_shape=jax.ShapeDtypeStruct(q.shape, F32),
        grid=(n_s // sb,),
        in_specs=[qspec, kvspec, kvspec],
        out_specs=qspec,
        compiler_params=_params("parallel"),
        name="attn_step",
    )(q, mk, mv)


def _head_selectors(d_rwkv):
    n_heads = d_rwkv // HEAD_DIM
    ch = jnp.arange(d_rwkv)[:, None] // HEAD_DIM
    col = jnp.arange(V7X_LANES)[None, :]
    gsel = (ch == col).astype(BF16)
    assert n_heads <= V7X_LANES
    return gsel, gsel.T


def kernel(x_prompt, x_sample, mem_prompt, state_shift, state_conv, state_wkv, cache_mem_k, cache_mem_v,
           ffn1_w1, ffn1_w3, ffn1_w2, ln1_g, ln1_b, w_in, mu_shift, w0, w2_decay, a0, a2_iclr, g2_gate,
           k_k, k_a, r_k, gn_g, gn_b, conv_w, conv_b, conv_ln_g, conv_ln_b, beta_rwkv, beta_conv, w_out,
           ln2_g, ln2_b, w_mq, w_mk, w_mv, w_mo, ln3_g, ln3_b, ffn2_w1, ffn2_w3, ffn2_w2, ln4_g, ln4_b):
    depth = ffn1_w1.shape[0]
    assert depth == 1
    batch, seq, d = x_prompt.shape
    n_s = x_sample.shape[0]
    assert x_sample.shape[1] == 1
    n_mem = mem_prompt.shape[1]
    d_rwkv = w0.shape[-1]
    d_conv = conv_b.shape[-1]
    n_heads = d_rwkv // HEAD_DIM
    n_rkv = 3 * d_rwkv
    shift_cols = mu_shift.shape[-1]
    alpha = (2.0 * depth) ** 0.25
    l = 0

    w_rkv = w_in[l, :, :n_rkv]
    w_lo = w_in[l, :, n_rkv:shift_cols]
    w_cv = w_in[l, :, shift_cols:]
    mu_rkv = mu_shift[l:l + 1, :n_rkv]
    mu_lo = mu_shift[l:l + 1, n_rkv:]
    gsel, gselt = _head_selectors(d_rwkv)
    prep_consts = (mu_rkv, mu_lo, w0[l:l + 1], a0[l:l + 1], k_k[l:l + 1], k_a[l:l + 1],
                   r_k[l].reshape(1, d_rwkv), w2_decay[l], a2_iclr[l], g2_gate[l], gsel, gselt)
    cw = conv_w[l].reshape(CONV_WIDTH, d_conv)
    conv_consts = (cw, conv_b[l:l + 1], conv_ln_g[l:l + 1], conv_ln_b[l:l + 1], beta_conv[l:l + 1])
    post_consts = (gn_g[l:l + 1], gn_b[l:l + 1], beta_rwkv[l:l + 1], gsel, gselt)

    def trunk_front(x, tm):
        x1, x1b = _ffn_ln(x, None, ffn1_w1[l], ffn1_w3[l], ffn1_w2[l], ln1_g[l:l + 1], ln1_b[l:l + 1], alpha, tm, 256)
        p_rkv = _mm(x1b, w_rkv, tm, 512)
        p_lo = _mm(x1b, w_lo, tm, w_lo.shape[1])
        p_cv = _mm(x1b, w_cv, tm, 512)
        return x1, p_rkv, p_lo, p_cv

    def trunk_back(x1, o_a, o_b, attend, tm):
        x2, x2b = _mm_res_ln([o_a, o_b], w_out[l], x1, ln2_g[l:l + 1], ln2_b[l:l + 1], alpha, tm, 512)
        q = _mm(x2b, w_mq[l], tm, 512, BF16)
        att = attend(q)
        x3, x3b = _mm_res_ln([att], w_mo[l], x2, ln3_g[l:l + 1], ln3_b[l:l + 1], alpha, tm, 512)
        return _ffn_ln(x3, x3b, ffn2_w1[l], ffn2_w3[l], ffn2_w2[l], ln4_g[l:l + 1], ln4_b[l:l + 1], alpha, tm, 256)[0]

    m_p = batch * seq
    tm_p = 1024
    x1, p_rkv, p_lo, p_cv = trunk_front(x_prompt.reshape(m_p, d), tm_p)
    zeros_rkv = jnp.zeros((batch, 1, n_rkv), F32)
    zeros_lo = jnp.zeros((batch, 1, shift_cols - n_rkv), F32)
    r, w, k, v, kk, bb, g, bonus = _rwkv_prep_seq(p_rkv, p_lo, zeros_rkv, zeros_lo, prep_consts, batch, seq, 256)

    n_k2 = HEAD_DIM // 2

    def to_lanes_k(t):
        t = t.reshape(batch, seq, n_heads, 2, n_k2)
        return t.transpose(1, 4, 3, 0, 2).reshape(seq, n_k2, 2 * batch * n_heads)

    def to_lanes_v(t):
        t = t.reshape(batch, seq, n_heads, HEAD_DIM).transpose(1, 3, 0, 2).reshape(seq, HEAD_DIM, batch * n_heads)
        return jnp.concatenate([t, t], axis=-1)

    assert 2 * batch * n_heads == V7X_LANES
    y_t, s_t = _rwkv_seq(to_lanes_k(r), to_lanes_k(w), to_lanes_k(k), to_lanes_k(kk), to_lanes_k(bb),
                         to_lanes_v(v), 64)
    y = y_t[:, :, :batch * n_heads].reshape(seq, HEAD_DIM, batch, n_heads).transpose(2, 0, 3, 1).reshape(m_p, d_rwkv)
    wkv_p = s_t.reshape(n_k2, HEAD_DIM, 2, batch, n_heads).transpose(3, 4, 1, 2, 0).reshape(
        batch, n_heads, HEAD_DIM, HEAD_DIM)
    o_a = _rwkv_post(y, g, bonus, *post_consts, 256)
    o_b, conv_p = _conv_seq(p_cv, *conv_consts, batch, seq, 256)

    mem2 = mem_prompt.reshape(batch * n_mem, d)
    mk_p = _mm(mem2, w_mk[l], batch * n_mem, 512)
    mv_p = _mm(mem2, w_mv[l], batch * n_mem, 512)
    attend_p = lambda q: _attn_seq(q, mk_p.reshape(batch, n_mem, d), mv_p.reshape(batch, n_mem, d), batch, seq, 512)
    y_prompt = trunk_back(x1, o_a, o_b, attend_p, tm_p).reshape(batch, seq, d)
    shift_p = jnp.concatenate([p_rkv.reshape(batch, seq, n_rkv)[:, -1], p_lo.reshape(batch, seq, -1)[:, -1]], axis=-1)
    dh = d // N_MEM_HEADS

    xs1, ps_rkv, ps_lo, ps_cv = trunk_front(x_sample.reshape(n_s, d), n_s)
    prev = state_shift[l]
    rs, ws, ks, vs, kks, bs, gs, bonus_s = _rwkv_prep_step(ps_rkv, prev[:, :n_rkv], ps_lo, prev[:, n_rkv:],
                                                          prep_consts, n_s)
    hd = lambda t: t.reshape(n_s, n_heads, 1, HEAD_DIM)
    ys, wkv_s = _rwkv_step(state_wkv[l], hd(rs), hd(ws), hd(ks), hd(kks), hd(bs), hd(vs), 8)
    o_as = _rwkv_post(ys.reshape(n_s, d_rwkv), gs, bonus_s, *post_consts, n_s)
    o_bs, conv_s = _conv_step(ps_cv.reshape(n_s, 1, 2 * d_conv), state_conv[l], *conv_consts, 8)
    o_bs = o_bs.reshape(n_s, d_conv).astype(BF16)
    attend_s = lambda q: _attn_step(q.astype(F32).reshape(n_s, 1, N_MEM_HEADS, dh), cache_mem_k[l], cache_mem_v[l],
                                    1).reshape(n_s, d).astype(BF16)
    y_sample = trunk_back(xs1, o_as, o_bs, attend_s, n_s).reshape(n_s, 1, d)
    shift_s = jnp.concatenate([ps_rkv, ps_lo], axis=-1)

    return (y_prompt, y_sample,
            shift_p[None], conv_p[None], wkv_p[None],
            mk_p.reshape(1, batch, n_mem, N_MEM_HEADS, dh), mv_p.reshape(1, batch, n_mem, N_MEM_HEADS, dh),
            shift_s[None], conv_s[None], wkv_s[None])
```

```python
import functools
import math

import jax
import jax.numpy as jnp
from jax import lax
from jax.experimental import pallas as pl
from jax.experimental.pallas import tpu as pltpu

F32 = jnp.float32
BF16 = jnp.bfloat16

V7X_LANES = 128
V7X_SUBLANES = 8
V7X_VMEM_LIMIT_BYTES = 56 * 1024 * 1024

HEAD_DIM = 64
CONV_WIDTH = 31
DECAY_LORA = 64
ICLR_LORA = 64
GATE_LORA = 160
N_MEM_HEADS = 4
LN_EPS = 1e-5
GN_EPS = 64e-5


def _params(*sem):
    return pltpu.CompilerParams(dimension_semantics=sem, vmem_limit_bytes=V7X_VMEM_LIMIT_BYTES)


def _resident(block_shape, index_map):
    return pl.BlockSpec(block_shape, index_map, pipeline_mode=pl.Buffered(1))


def _ln_rows(y, g, b, eps):
    mu = jnp.mean(y, axis=-1, keepdims=True)
    yc = y - mu
    var = jnp.mean(yc * yc, axis=-1, keepdims=True)
    return yc * lax.rsqrt(var + eps) * g + b


def _dot(a, b):
    return jnp.dot(a, b, preferred_element_type=F32)


def _split3_dot(x, g):
    hi = x.astype(BF16)
    r1 = x - hi.astype(F32)
    mid = r1.astype(BF16)
    lo = (r1 - mid.astype(F32)).astype(BF16)
    return _dot(hi, g) + _dot(mid, g) + _dot(lo, g)


def _head_sum(x, gsel, gselt):
    return _split3_dot(_split3_dot(x, gsel), gselt)


def _ffn_ln_kernel(x_ref, *refs, alpha, n_chunk, row_chunk, cast_x):
    if cast_x:
        w1_ref, w3_ref, w2_ref, g_ref, b_ref, o_ref, ob_ref, xb_ref = refs
    else:
        xb_ref, w1_ref, w3_ref, w2_ref, g_ref, b_ref, o_ref, ob_ref = refs
    j = pl.program_id(1)
    tm, d = o_ref.shape

    @pl.when(j == 0)
    def _():
        o_ref[...] = jnp.zeros_like(o_ref)
        if cast_x:
            xb_ref[...] = x_ref[...].astype(BF16)

    xb = xb_ref[...]
    h1 = _dot(xb, w1_ref[...].astype(BF16))
    h3 = _dot(xb, w3_ref[...].astype(BF16))
    hb = (h1 * jax.nn.sigmoid(h1) * h3).astype(BF16)
    w2b = w2_ref[...].astype(BF16)
    cw = d // n_chunk
    for c in range(n_chunk):
        o_ref[:, c * cw:(c + 1) * cw] += _dot(hb, w2b[:, c * cw:(c + 1) * cw])

    @pl.when(j == pl.num_programs(1) - 1)
    def _():
        def body(r, carry):
            rows = pl.ds(pl.multiple_of(r * row_chunk, row_chunk), row_chunk)
            y = alpha * x_ref[rows, :] + 0.5 * o_ref[rows, :]
            y = _ln_rows(y, g_ref[...], b_ref[...], LN_EPS)
            o_ref[rows, :] = y
            ob_ref[rows, :] = y.astype(BF16)
            return carry

        lax.fori_loop(0, tm // row_chunk, body, 0)


def _ffn_ln(x, xb, w1, w3, w2, g, b, alpha, tm, tf):
    m, d = x.shape
    f = w1.shape[1]
    cast_x = xb is None
    kern = functools.partial(_ffn_ln_kernel, alpha=alpha, n_chunk=4, row_chunk=min(tm, 128), cast_x=cast_x)
    row_tile = lambda i, j: (i, 0)
    return pl.pallas_call(
        kern,
        out_shape=[jax.ShapeDtypeStruct((m, d), F32), jax.ShapeDtypeStruct((m, d), BF16)],
        grid=(m // tm, f // tf),
        in_specs=[_resident((tm, d), row_tile)] * (1 if cast_x else 2) + [
            pl.BlockSpec((d, tf), lambda i, j: (0, j)),
            pl.BlockSpec((d, tf), lambda i, j: (0, j)),
            pl.BlockSpec((tf, d), lambda i, j: (j, 0)),
            pl.BlockSpec((1, d), lambda i, j: (0, 0)),
            pl.BlockSpec((1, d), lambda i, j: (0, 0)),
        ],
        out_specs=[_resident((tm, d), row_tile), _resident((tm, d), row_tile)],
        scratch_shapes=[pltpu.VMEM((tm, d), BF16)] if cast_x else [],
        compiler_params=_params("parallel", "arbitrary"),
        name="ffn_ln",
    )(*((x,) if cast_x else (x, xb)), w1, w3, w2, g, b)


def _mm_kernel(x_ref, w_ref, o_ref):
    o_ref[...] = _dot(x_ref[...].astype(BF16), w_ref[...].astype(BF16)).astype(o_ref.dtype)


def _mm(x, w, tm, tn, out_dtype=F32):
    m, k = x.shape
    n = w.shape[1]
    return pl.pallas_call(
        _mm_kernel,
        out_shape=jax.ShapeDtypeStruct((m, n), out_dtype),
        grid=(m // tm, n // tn),
        in_specs=[
            pl.BlockSpec((tm, k), lambda i, j: (i, 0)),
            pl.BlockSpec((k, tn), lambda i, j: (0, j)),
        ],
        out_specs=pl.BlockSpec((tm, tn), lambda i, j: (i, j)),
        compiler_params=_params("parallel", "arbitrary"),
        name="mm",
    )(x, w)


def _cast_kernel(x_ref, o_ref):
    o_ref[...] = x_ref[...].astype(o_ref.dtype)


def _cast_bf16(w, tr):
    rows, cols = w.shape
    return pl.pallas_call(
        _cast_kernel,
        out_shape=jax.ShapeDtypeStruct(w.shape, BF16),
        grid=(rows // tr,),
        in_specs=[pl.BlockSpec((tr, cols), lambda i: (i, 0))],
        out_specs=pl.BlockSpec((tr, cols), lambda i: (i, 0)),
        compiler_params=_params("parallel"),
        name="cast_bf16",
    )(w)


def _mm_res_ln_kernel(*refs, n_a, alpha, row_chunk):
    a_refs = refs[:n_a]
    w_refs = refs[n_a:2 * n_a]
    res_ref, g_ref, b_ref, o_ref, ob_ref, acc_ref = refs[2 * n_a:]
    tm, d = acc_ref.shape

    acc = _dot(a_refs[0][...], w_refs[0][...])
    for a_ref, w_ref in zip(a_refs[1:], w_refs[1:]):
        acc += _dot(a_ref[...], w_ref[...])
    acc_ref[...] = acc

    def body(r, carry):
        rows = pl.ds(pl.multiple_of(r * row_chunk, row_chunk), row_chunk)
        y = _ln_rows(alpha * res_ref[rows, :] + acc_ref[rows, :], g_ref[...], b_ref[...], LN_EPS)
        o_ref[rows, :] = y
        ob_ref[rows, :] = y.astype(BF16)
        return carry

    lax.fori_loop(0, tm // row_chunk, body, 0)


def _mm_res_ln(a_list, w_list, res, g, b, alpha, tm):
    m, d = res.shape
    n_a = len(a_list)
    kern = functools.partial(_mm_res_ln_kernel, n_a=n_a, alpha=alpha, row_chunk=min(tm, 64))
    row_tile = lambda i: (i, 0)
    in_specs = [pl.BlockSpec((tm, a.shape[1]), row_tile) for a in a_list]
    in_specs += [_resident((a.shape[1], d), functools.partial(lambda i, r: (r, 0), r=r))
                 for a, (_, r) in zip(a_list, w_list)]
    in_specs += [
        pl.BlockSpec((tm, d), row_tile),
        pl.BlockSpec((1, d), lambda i: (0, 0)),
        pl.BlockSpec((1, d), lambda i: (0, 0)),
    ]
    return pl.pallas_call(
        kern,
        out_shape=[jax.ShapeDtypeStruct((m, d), F32), jax.ShapeDtypeStruct((m, d), BF16)],
        grid=(m // tm,),
        in_specs=in_specs,
        out_specs=[pl.BlockSpec((tm, d), row_tile), pl.BlockSpec((tm, d), row_tile)],
        scratch_shapes=[pltpu.VMEM((tm, d), F32)],
        compiler_params=_params("parallel"),
        name="mm_res_ln",
    )(*a_list, *[w for w, _ in w_list], res, g, b)


def _shift_rows(cur, first_row):
    rolled = pltpu.roll(cur, 1, 0)
    row_id = lax.broadcasted_iota(jnp.int32, cur.shape, 0)
    return jnp.where(row_id == 0, first_row, rolled)


def _rwkv_prep_math(cur_rkv, prev_rkv, cur_lo, prev_lo, mu_rkv, mu_lo, w0, a0, k_k, k_a, r_k,
                    w2d, a2, g2, gsel, gselt, d_rwkv):
    pm = cur_rkv + mu_rkv * (prev_rkv - cur_rkv)
    pl_ = cur_lo + mu_lo * (prev_lo - cur_lo)
    r = pm[:, :d_rwkv]
    k = pm[:, d_rwkv:2 * d_rwkv]
    v = pm[:, 2 * d_rwkv:]
    wd = pl_[:, :DECAY_LORA]
    ad = pl_[:, DECAY_LORA:DECAY_LORA + ICLR_LORA]
    gd = pl_[:, DECAY_LORA + ICLR_LORA:]
    z = w0 + _dot(jnp.tanh(wd).astype(BF16), w2d.astype(BF16))
    w_log = -jax.nn.softplus(-z) - 0.5
    decay = jnp.exp(-jnp.exp(w_log))
    a = jax.nn.sigmoid(a0 + _dot(ad.astype(BF16), a2.astype(BF16)))
    g = _dot(jax.nn.sigmoid(gd).astype(BF16), g2.astype(BF16))
    kk = k * k_k
    norm = jnp.sqrt(_head_sum(kk * kk, gsel, gselt))
    kk = kk / jnp.maximum(norm, 1e-12)
    kmod = k * (1.0 + (a - 1.0) * k_a)
    bonus = _head_sum(r * kmod * r_k, gsel, gselt) * v
    return r, decay, kmod, v, kk, kk * a, g, bonus


def _rwkv_prep_seq_kernel(cur_rkv_ref, prev8_rkv_ref, cur_lo_ref, prev8_lo_ref, sp_rkv_ref, sp_lo_ref,
                          mu_rkv_ref, mu_lo_ref, w0_ref, a0_ref, kk_ref, ka_ref, rk_ref,
                          w2d_ref, a2_ref, g2_ref, gsel_ref, gselt_ref,
                          r_o, w_o, k_o, v_o, kk_o, b_o, g_o, bonus_o):
    first = pl.program_id(1) == 0
    cur_rkv = cur_rkv_ref[...]
    cur_lo = cur_lo_ref[...]
    last = V7X_SUBLANES - 1
    first_rkv = jnp.where(first, sp_rkv_ref[...], prev8_rkv_ref[last:last + 1, :])
    first_lo = jnp.where(first, sp_lo_ref[...], prev8_lo_ref[last:last + 1, :])
    outs = _rwkv_prep_math(
        cur_rkv, _shift_rows(cur_rkv, first_rkv), cur_lo, _shift_rows(cur_lo, first_lo),
        mu_rkv_ref[...], mu_lo_ref[...], w0_ref[...], a0_ref[...], kk_ref[...], ka_ref[...], rk_ref[...],
        w2d_ref[...], a2_ref[...], g2_ref[...], gsel_ref[...], gselt_ref[...], r_o.shape[-1])
    for o_ref, val in zip((r_o, w_o, k_o, v_o, kk_o, b_o, g_o, bonus_o), outs):
        o_ref[...] = val


def _rwkv_prep_step_kernel(cur_rkv_ref, prev_rkv_ref, cur_lo_ref, prev_lo_ref,
                           mu_rkv_ref, mu_lo_ref, w0_ref, a0_ref, kk_ref, ka_ref, rk_ref,
                           w2d_ref, a2_ref, g2_ref, gsel_ref, gselt_ref,
                           r_o, w_o, k_o, v_o, kk_o, b_o, g_o, bonus_o):
    outs = _rwkv_prep_math(
        cur_rkv_ref[...], prev_rkv_ref[...], cur_lo_ref[...], prev_lo_ref[...],
        mu_rkv_ref[...], mu_lo_ref[...], w0_ref[...], a0_ref[...], kk_ref[...], ka_ref[...], rk_ref[...],
        w2d_ref[...], a2_ref[...], g2_ref[...], gsel_ref[...], gselt_ref[...], r_o.shape[-1])
    for o_ref, val in zip((r_o, w_o, k_o, v_o, kk_o, b_o, g_o, bonus_o), outs):
        o_ref[...] = val


def _const_spec(arr):
    nd = arr.ndim
    return pl.BlockSpec(arr.shape, lambda *_: (0,) * nd)


def _rwkv_prep_seq(p_rkv, p_lo, sp_rkv, sp_lo, consts, batch, seq, tt):
    m, n_rkv = p_rkv.shape
    n_lo = p_lo.shape[1]
    d_rwkv = n_rkv // 3
    nt = seq // tt
    sub = V7X_SUBLANES

    def cur_map(b, i):
        return (b * nt + i, 0)

    def prev_map(b, i):
        return (jnp.maximum((b * seq + i * tt) // sub - 1, 0), 0)

    out_sds = jax.ShapeDtypeStruct((m, d_rwkv), F32)
    out_spec = pl.BlockSpec((tt, d_rwkv), cur_map)
    return pl.pallas_call(
        _rwkv_prep_seq_kernel,
        out_shape=[out_sds] * 8,
        grid=(batch, nt),
        in_specs=[
            pl.BlockSpec((tt, n_rkv), cur_map),
            pl.BlockSpec((sub, n_rkv), prev_map),
            pl.BlockSpec((tt, n_lo), cur_map),
            pl.BlockSpec((sub, n_lo), prev_map),
            pl.BlockSpec((None, 1, n_rkv), lambda b, i: (b, 0, 0)),
            pl.BlockSpec((None, 1, n_lo), lambda b, i: (b, 0, 0)),
        ] + [_const_spec(c) for c in consts],
        out_specs=[out_spec] * 8,
        compiler_params=_params("parallel", "arbitrary"),
        name="rwkv_prep_seq",
    )(p_rkv, p_rkv, p_lo, p_lo, sp_rkv, sp_lo, *consts)


def _rwkv_prep_step(p_rkv, prev_rkv, p_lo, prev_lo, consts, tb):
    m, n_rkv = p_rkv.shape
    n_lo = p_lo.shape[1]
    d_rwkv = n_rkv // 3
    out_sds = jax.ShapeDtypeStruct((m, d_rwkv), F32)
    row = lambda i: (i, 0)
    return pl.pallas_call(
        _rwkv_prep_step_kernel,
        out_shape=[out_sds] * 8,
        grid=(m // tb,),
        in_specs=[
            pl.BlockSpec((tb, n_rkv), row), pl.BlockSpec((tb, n_rkv), row),
            pl.BlockSpec((tb, n_lo), row), pl.BlockSpec((tb, n_lo), row),
        ] + [_const_spec(c) for c in consts],
        out_specs=[pl.BlockSpec((tb, d_rwkv), row)] * 8,
        compiler_params=_params("parallel"),
        name="rwkv_prep_step",
    )(p_rkv, prev_rkv, p_lo, prev_lo, *consts)


def _rwkv_seq_kernel(r_ref, w_ref, k_ref, kk_ref, b_ref, v_ref, y_ref, s_ref):
    tb, n_k2, lanes = r_ref.shape
    half = lanes // 2

    @pl.when(pl.program_id(0) == 0)
    def _():
        s_ref[...] = jnp.zeros_like(s_ref)

    def step(t, carry):
        v_t = v_ref[t]
        sa = s_ref[0] * kk_ref[t, pl.ds(0, 1), :]
        for k2 in range(1, n_k2):
            sa += s_ref[k2] * kk_ref[t, pl.ds(k2, 1), :]
        sa = sa + pltpu.roll(sa, half, 1)
        y = None
        for k2 in range(n_k2):
            row = pl.ds(k2, 1)
            s_new = s_ref[k2] * w_ref[t, row, :] - b_ref[t, row, :] * sa + k_ref[t, row, :] * v_t
            s_ref[k2] = s_new
            term = s_new * r_ref[t, row, :]
            y = term if y is None else y + term
        y_ref[t] = y + pltpu.roll(y, half, 1)
        return carry

    lax.fori_loop(0, tb, step, 0)


def _rwkv_seq(r_t, w_t, k_t, kk_t, b_t, v_t, tb):
    seq, n_k2, lanes = r_t.shape
    n_v = v_t.shape[1]
    kspec = pl.BlockSpec((tb, n_k2, lanes), lambda i: (i, 0, 0))
    vspec = pl.BlockSpec((tb, n_v, lanes), lambda i: (i, 0, 0))
    return pl.pallas_call(
        _rwkv_seq_kernel,
        out_shape=[jax.ShapeDtypeStruct((seq, n_v, lanes), F32),
                   jax.ShapeDtypeStruct((n_k2, n_v, lanes), F32)],
        grid=(seq // tb,),
        in_specs=[kspec, kspec, kspec, kspec, kspec, vspec],
        out_specs=[vspec, pl.BlockSpec((n_k2, n_v, lanes), lambda i: (0, 0, 0))],
        compiler_params=_params("arbitrary"),
        name="rwkv_seq",
    )(r_t, w_t, k_t, kk_t, b_t, v_t)


def _rwkv_step_kernel(s_ref, r_ref, w_ref, k_ref, kk_ref, b_ref, v_ref, y_ref, so_ref):
    n = s_ref.shape[-1]
    eye = (lax.broadcasted_iota(jnp.int32, (n, n), 0) == lax.broadcasted_iota(jnp.int32, (n, n), 1)).astype(F32)
    s = s_ref[...]
    v_col = jnp.sum(eye * v_ref[...], axis=-1, keepdims=True)
    sa = jnp.sum(s * kk_ref[...], axis=-1, keepdims=True)
    s_new = s * w_ref[...] - sa * b_ref[...] + v_col * k_ref[...]
    so_ref[...] = s_new
    y_col = jnp.sum(s_new * r_ref[...], axis=-1, keepdims=True)
    y_ref[...] = jnp.sum(eye * y_col, axis=-2, keepdims=True)


def _rwkv_step(state, r, w, k, kk, b, v, sb):
    n_s, n_h, n, _ = state.shape
    sspec = pl.BlockSpec((sb, n_h, n, n), lambda i: (i, 0, 0, 0))
    vspec = pl.BlockSpec((sb, n_h, 1, n), lambda i: (i, 0, 0, 0))
    return pl.pallas_call(
        _rwkv_step_kernel,
        out_shape=[jax.ShapeDtypeStruct((n_s, n_h, 1, n), F32), jax.ShapeDtypeStruct(state.shape, F32)],
        grid=(n_s // sb,),
        in_specs=[sspec] + [vspec] * 6,
        out_specs=[vspec, sspec],
        compiler_params=_params("parallel"),
        name="rwkv_step",
    )(state, r, w, k, kk, b, v)


def _rwkv_post_kernel(y_ref, g_ref, bonus_ref, gng_ref, gnb_ref, beta_ref, gsel_ref, gselt_ref, o_ref):
    y = y_ref[...]
    gsel = gsel_ref[...]
    gselt = gselt_ref[...]
    mu = _head_sum(y, gsel, gselt) * (1.0 / HEAD_DIM)
    yc = y - mu
    var = _head_sum(yc * yc, gsel, gselt) * (1.0 / HEAD_DIM)
    yn = yc * lax.rsqrt(var + GN_EPS) * gng_ref[...] + gnb_ref[...]
    o_ref[...] = ((yn + bonus_ref[...]) * g_ref[...] * beta_ref[...]).astype(o_ref.dtype)


def _rwkv_post(y, g, bonus, gn_g, gn_b, beta, gsel, gselt, tt):
    m, d = y.shape
    row = pl.BlockSpec((tt, d), lambda i: (i, 0))
    consts = (gn_g, gn_b, beta, gsel, gselt)
    return pl.pallas_call(
        _rwkv_post_kernel,
        out_shape=jax.ShapeDtypeStruct((m, d), BF16),
        grid=(m // tt,),
        in_specs=[row, row, row] + [_const_spec(c) for c in consts],
        out_specs=row,
        compiler_params=_params("parallel"),
        name="rwkv_post",
    )(y, g, bonus, *consts)


CONV_HALO = 32


def _glu(pc, d_conv):
    return pc[:, :d_conv] * jax.nn.sigmoid(pc[:, d_conv:])


def _conv_seq_kernel(pc_ref, halo_ref, cw_ref, cb_ref, lng_ref, lnb_ref, beta_ref, o_ref, st_ref, ubuf_ref,
                     *, row_chunk):
    tt, d_conv = o_ref.shape
    i = pl.program_id(1)
    n_hist = CONV_WIDTH - 1
    pad = CONV_HALO - n_hist
    u_halo = _glu(halo_ref[...], d_conv)
    ubuf_ref[0:CONV_HALO, :] = jnp.where(i == 0, jnp.zeros_like(u_halo), u_halo)
    ubuf_ref[CONV_HALO:, :] = _glu(pc_ref[...], d_conv)

    for base in range(0, tt, row_chunk):
        acc = ubuf_ref[base + pad:base + pad + row_chunk, :] * cw_ref[0:1, :]
        for w in range(1, CONV_WIDTH):
            acc += ubuf_ref[base + pad + w:base + pad + w + row_chunk, :] * cw_ref[w:w + 1, :]
        c = _ln_rows(acc + cb_ref[...], lng_ref[...], lnb_ref[...], LN_EPS)
        o_ref[base:base + row_chunk, :] = (c * jax.nn.sigmoid(c) * beta_ref[...]).astype(o_ref.dtype)

    @pl.when(i == pl.num_programs(1) - 1)
    def _():
        st_ref[...] = ubuf_ref[CONV_HALO + tt - n_hist:, :]


def _conv_seq(pc, cw, cb, ln_g, ln_b, beta, batch, seq, tt):
    m, two_d = pc.shape
    d_conv = two_d // 2
    nt = seq // tt
    hb = tt // CONV_HALO
    consts = (cw, cb, ln_g, ln_b, beta)
    kern = functools.partial(_conv_seq_kernel, row_chunk=16)
    return pl.pallas_call(
        kern,
        out_shape=[jax.ShapeDtypeStruct((m, d_conv), BF16),
                   jax.ShapeDtypeStruct((batch, CONV_WIDTH - 1, d_conv), F32)],
        grid=(batch, nt),
        in_specs=[
            pl.BlockSpec((tt, two_d), lambda b, i: (b * nt + i, 0)),
            pl.BlockSpec((CONV_HALO, two_d), lambda b, i: (jnp.maximum((b * nt + i) * hb - 1, 0), 0)),
        ] + [_const_spec(c) for c in consts],
        out_specs=[pl.BlockSpec((tt, d_conv), lambda b, i: (b * nt + i, 0)),
                   pl.BlockSpec((None, CONV_WIDTH - 1, d_conv), lambda b, i: (b, 0, 0))],
        scratch_shapes=[pltpu.VMEM((CONV_HALO + tt, d_conv), F32)],
        compiler_params=_params("parallel", "arbitrary"),
        name="conv_seq",
    )(pc, pc, *consts)


def _conv_step_kernel(pc_ref, st_ref, cw_ref, cb_ref, lng_ref, lnb_ref, beta_ref, o_ref, so_ref):
    d_conv = o_ref.shape[-1]
    n_hist = CONV_WIDTH - 1
    pc = pc_ref[...]
    u = pc[:, :, :d_conv] * jax.nn.sigmoid(pc[:, :, d_conv:])
    st = st_ref[...]
    c = (jnp.sum(st * cw_ref[0:n_hist, :], axis=1, keepdims=True)
         + u * cw_ref[n_hist:CONV_WIDTH, :] + cb_ref[...])
    c = _ln_rows(c, lng_ref[...], lnb_ref[...], LN_EPS)
    o_ref[...] = c * jax.nn.sigmoid(c) * beta_ref[...]
    so_ref[:, 0:n_hist - 1, :] = st[:, 1:, :]
    so_ref[:, n_hist - 1:n_hist, :] = u


def _conv_step(pc, state, cw, cb, ln_g, ln_b, beta, sb):
    n_s, _, two_d = pc.shape
    d_conv = two_d // 2
    consts = (cw, cb, ln_g, ln_b, beta)
    sspec = pl.BlockSpec((sb, CONV_WIDTH - 1, d_conv), lambda i: (i, 0, 0))
    return pl.pallas_call(
        _conv_step_kernel,
        out_shape=[jax.ShapeDtypeStruct((n_s, 1, d_conv), F32), jax.ShapeDtypeStruct(state.shape, F32)],
        grid=(n_s // sb,),
        in_specs=[pl.BlockSpec((sb, 1, two_d), lambda i: (i, 0, 0)), sspec] + [_const_spec(c) for c in consts],
        out_specs=[pl.BlockSpec((sb, 1, d_conv), lambda i: (i, 0, 0)), sspec],
        compiler_params=_params("parallel"),
        name="conv_step",
    )(pc, state, *consts)


def _attn_seq_kernel(q_ref, k_ref, v_ref, o_ref, *, scale):
    d = q_ref.shape[-1]
    dh = d // N_MEM_HEADS
    for h in range(N_MEM_HEADS):
        cols = slice(h * dh, (h + 1) * dh)
        k = k_ref[:, cols].astype(BF16)
        s = lax.dot_general(q_ref[:, cols], k, (((1,), (1,)), ((), ())), preferred_element_type=F32) * scale
        s = s - jnp.max(s, axis=-1, keepdims=True)
        e = jnp.exp(s)
        p = e / jnp.sum(e, axis=-1, keepdims=True)
        o_ref[:, cols] = _dot(p.astype(BF16), v_ref[:, cols].astype(BF16)).astype(o_ref.dtype)


def _attn_seq(q, mk, mv, batch, seq, tq):
    m, d = q.shape
    n_mem = mk.shape[1]
    nt = seq // tq
    kern = functools.partial(_attn_seq_kernel, scale=1.0 / math.sqrt(d // N_MEM_HEADS))
    kv_spec = pl.BlockSpec((None, n_mem, d), lambda b, i: (b, 0, 0))
    return pl.pallas_call(
        kern,
        out_shape=jax.ShapeDtypeStruct((m, d), BF16),
        grid=(batch, nt),
        in_specs=[pl.BlockSpec((tq, d), lambda b, i: (b * nt + i, 0)), kv_spec, kv_spec],
        out_specs=pl.BlockSpec((tq, d), lambda b, i: (b * nt + i, 0)),
        compiler_params=_params("parallel", "arbitrary"),
        name="attn_seq",
    )(q, mk, mv)


def _attn_step_kernel(q_ref, k_ref, v_ref, o_ref, *, scale):
    for i in range(q_ref.shape[0]):
        s = jnp.sum(k_ref[i] * q_ref[i], axis=-1, keepdims=True) * scale
        s = s - jnp.max(s, axis=0, keepdims=True)
        e = jnp.exp(s)
        p = e / jnp.sum(e, axis=0, keepdims=True)
        o_ref[i] = jnp.sum(p * v_ref[i], axis=0, keepdims=True)


def _attn_step(q, mk, mv, sb):
    n_s, _, n_h, dh = q.shape
    n_mem = mk.shape[1]
    kern = functools.partial(_attn_step_kernel, scale=1.0 / math.sqrt(dh))
    qspec = pl.BlockSpec((sb, 1, n_h, dh), lambda i: (i, 0, 0, 0))
    kvspec = pl.BlockSpec((sb, n_mem, n_h, dh), lambda i: (i, 0, 0, 0))
    return pl.pallas_call(
        kern,
        out_shape=jax.ShapeDtypeStruct(q.shape, F32),
        grid=(n_s // sb,),
        in_specs=[qspec, kvspec, kvspec],
        out_specs=qspec,
        compiler_params=_params("parallel"),
        name="attn_step",
    )(q, mk, mv)


def _head_selectors(d_rwkv):
    n_heads = d_rwkv // HEAD_DIM
    ch = jnp.arange(d_rwkv)[:, None] // HEAD_DIM
    col = jnp.arange(V7X_LANES)[None, :]
    gsel = (ch == col).astype(BF16)
    assert n_heads <= V7X_LANES
    return gsel, gsel.T


def kernel(x_prompt, x_sample, mem_prompt, state_shift, state_conv, state_wkv, cache_mem_k, cache_mem_v,
           ffn1_w1, ffn1_w3, ffn1_w2, ln1_g, ln1_b, w_in, mu_shift, w0, w2_decay, a0, a2_iclr, g2_gate,
           k_k, k_a, r_k, gn_g, gn_b, conv_w, conv_b, conv_ln_g, conv_ln_b, beta_rwkv, beta_conv, w_out,
           ln2_g, ln2_b, w_mq, w_mk, w_mv, w_mo, ln3_g, ln3_b, ffn2_w1, ffn2_w3, ffn2_w2, ln4_g, ln4_b):
    depth = ffn1_w1.shape[0]
    assert depth == 1
    batch, seq, d = x_prompt.shape
    n_s = x_sample.shape[0]
    assert x_sample.shape[1] == 1
    n_mem = mem_prompt.shape[1]
    d_rwkv = w0.shape[-1]
    d_conv = conv_b.shape[-1]
    n_heads = d_rwkv // HEAD_DIM
    n_rkv = 3 * d_rwkv
    shift_cols = mu_shift.shape[-1]
    alpha = (2.0 * depth) ** 0.25
    l = 0

    w_rkv = w_in[l, :, :n_rkv]
    w_lo = w_in[l, :, n_rkv:shift_cols]
    w_cv = w_in[l, :, shift_cols:]
    mu_rkv = mu_shift[l:l + 1, :n_rkv]
    mu_lo = mu_shift[l:l + 1, n_rkv:]
    gsel, gselt = _head_selectors(d_rwkv)
    prep_consts = (mu_rkv, mu_lo, w0[l:l + 1], a0[l:l + 1], k_k[l:l + 1], k_a[l:l + 1],
                   r_k[l].reshape(1, d_rwkv), w2_decay[l], a2_iclr[l], g2_gate[l], gsel, gselt)
    cw = conv_w[l].reshape(CONV_WIDTH, d_conv)
    conv_consts = (cw, conv_b[l:l + 1], conv_ln_g[l:l + 1], conv_ln_b[l:l + 1], beta_conv[l:l + 1])
    post_consts = (gn_g[l:l + 1], gn_b[l:l + 1], beta_rwkv[l:l + 1], gsel, gselt)
    w_out_b = _cast_bf16(w_out[l], 512)
    w_mo_b = _cast_bf16(w_mo[l], 512)

    def trunk_front(x, tm):
        x1, x1b = _ffn_ln(x, None, ffn1_w1[l], ffn1_w3[l], ffn1_w2[l], ln1_g[l:l + 1], ln1_b[l:l + 1], alpha, tm, 256)
        p_rkv = _mm(x1b, w_rkv, tm, 512)
        p_lo = _mm(x1b, w_lo, tm, w_lo.shape[1])
        p_cv = _mm(x1b, w_cv, tm, 512)
        return x1, p_rkv, p_lo, p_cv

    def trunk_back(x1, o_a, o_b, attend, tm):
        tm_ln = min(tm, 512)
        x2, x2b = _mm_res_ln([o_a, o_b], [(w_out_b, 0), (w_out_b, 1)], x1, ln2_g[l:l + 1], ln2_b[l:l + 1], alpha, tm_ln)
        q = _mm(x2b, w_mq[l], tm, 512, BF16)
        att = attend(q)
        x3, x3b = _mm_res_ln([att], [(w_mo_b, 0)], x2, ln3_g[l:l + 1], ln3_b[l:l + 1], alpha, tm_ln)
        return _ffn_ln(x3, x3b, ffn2_w1[l], ffn2_w3[l], ffn2_w2[l], ln4_g[l:l + 1], ln4_b[l:l + 1], alpha, tm, 256)[0]

    m_p = batch * seq
    tm_p = 1024
    x1, p_rkv, p_lo, p_cv = trunk_front(x_prompt.reshape(m_p, d), tm_p)
    zeros_rkv = jnp.zeros((batch, 1, n_rkv), F32)
    zeros_lo = jnp.zeros((batch, 1, shift_cols - n_rkv), F32)
    r, w, k, v, kk, bb, g, bonus = _rwkv_prep_seq(p_rkv, p_lo, zeros_rkv, zeros_lo, prep_consts, batch, seq, 256)

    n_k2 = HEAD_DIM // 2

    def to_lanes_k(t):
        t = t.reshape(batch, seq, n_heads, 2, n_k2)
        return t.transpose(1, 4, 3, 0, 2).reshape(seq, n_k2, 2 * batch * n_heads)

    def to_lanes_v(t):
        t = t.reshape(batch, seq, n_heads, HEAD_DIM).transpose(1, 3, 0, 2).reshape(seq, HEAD_DIM, batch * n_heads)
        return jnp.concatenate([t, t], axis=-1)

    assert 2 * batch * n_heads == V7X_LANES
    y_t, s_t = _rwkv_seq(to_lanes_k(r), to_lanes_k(w), to_lanes_k(k), to_lanes_k(kk), to_lanes_k(bb),
                         to_lanes_v(v), 64)
    y = y_t[:, :, :batch * n_heads].reshape(seq, HEAD_DIM, batch, n_heads).transpose(2, 0, 3, 1).reshape(m_p, d_rwkv)
    wkv_p = s_t.reshape(n_k2, HEAD_DIM, 2, batch, n_heads).transpose(3, 4, 1, 2, 0).reshape(
        batch, n_heads, HEAD_DIM, HEAD_DIM)
    o_a = _rwkv_post(y, g, bonus, *post_consts, 256)
    o_b, conv_p = _conv_seq(p_cv, *conv_consts, batch, seq, 256)

    mem2 = mem_prompt.reshape(batch * n_mem, d)
    mk_p = _mm(mem2, w_mk[l], batch * n_mem, 512)
    mv_p = _mm(mem2, w_mv[l], batch * n_mem, 512)
    attend_p = lambda q: _attn_seq(q, mk_p.reshape(batch, n_mem, d), mv_p.reshape(batch, n_mem, d), batch, seq, 512)
    y_prompt = trunk_back(x1, o_a, o_b, attend_p, tm_p).reshape(batch, seq, d)
    shift_p = jnp.concatenate([p_rkv.reshape(batch, seq, n_rkv)[:, -1], p_lo.reshape(batch, seq, -1)[:, -1]], axis=-1)
    dh = d // N_MEM_HEADS

    xs1, ps_rkv, ps_lo, ps_cv = trunk_front(x_sample.reshape(n_s, d), n_s)
    prev = state_shift[l]
    rs, ws, ks, vs, kks, bs, gs, bonus_s = _rwkv_prep_step(ps_rkv, prev[:, :n_rkv], ps_lo, prev[:, n_rkv:],
                                                          prep_consts, n_s)
    hd = lambda t: t.reshape(n_s, n_heads, 1, HEAD_DIM)
    ys, wkv_s = _rwkv_step(state_wkv[l], hd(rs), hd(ws), hd(ks), hd(kks), hd(bs), hd(vs), 8)
    o_as = _rwkv_post(ys.reshape(n_s, d_rwkv), gs, bonus_s, *post_consts, n_s)
    o_bs, conv_s = _conv_step(ps_cv.reshape(n_s, 1, 2 * d_conv), state_conv[l], *conv_consts, 8)
    o_bs = o_bs.reshape(n_s, d_conv).astype(BF16)
    attend_s = lambda q: _attn_step(q.astype(F32).reshape(n_s, 1, N_MEM_HEADS, dh), cache_mem_k[l], cache_mem_v[l],
                                    2).reshape(n_s, d).astype(BF16)
    y_sample = trunk_back(xs1, o_as, o_bs, attend_s, n_s).reshape(n_s, 1, d)
    shift_s = jnp.concatenate([ps_rkv, ps_lo], axis=-1)

    return (y_prompt, y_sample,
            shift_p[None], conv_p[None], wkv_p[None],
            mk_p.reshape(1, batch, n_mem, N_MEM_HEADS, dh), mv_p.reshape(1, batch, n_mem, N_MEM_HEADS, dh),
            shift_s[None], conv_s[None], wkv_s[None])
```

```python
import functools
import math

import jax
import jax.numpy as jnp
from jax import lax
from jax.experimental import pallas as pl
from jax.experimental.pallas import tpu as pltpu

F32 = jnp.float32
BF16 = jnp.bfloat16

V7X_LANES = 128
V7X_SUBLANES = 8
V7X_VMEM_LIMIT_BYTES = 56 * 1024 * 1024

HEAD_DIM = 64
CONV_WIDTH = 31
DECAY_LORA = 64
ICLR_LORA = 64
GATE_LORA = 160
N_MEM_HEADS = 4
LN_EPS = 1e-5
GN_EPS = 64e-5


def _params(*sem):
    return pltpu.CompilerParams(dimension_semantics=sem, vmem_limit_bytes=V7X_VMEM_LIMIT_BYTES)


def _resident(block_shape, index_map):
    return pl.BlockSpec(block_shape, index_map, pipeline_mode=pl.Buffered(1))


def _ln_rows(y, g, b, eps):
    mu = jnp.mean(y, axis=-1, keepdims=True)
    yc = y - mu
    var = jnp.mean(yc * yc, axis=-1, keepdims=True)
    return yc * lax.rsqrt(var + eps) * g + b


def _dot(a, b):
    return jnp.dot(a, b, preferred_element_type=F32)


def _split_dot(x, g):
    hi = x.astype(BF16)
    mid = (x - hi.astype(F32)).astype(BF16)
    return _dot(hi, g) + _dot(mid, g)


def _head_sum(x, gsel, gselt):
    return _split_dot(_split_dot(x, gsel), gselt)


def _ffn_ln_kernel(x_ref, *refs, alpha, n_chunk, row_chunk, cast_x):
    if cast_x:
        w1_ref, w3_ref, w2_ref, g_ref, b_ref, o_ref, ob_ref, xb_ref = refs
    else:
        xb_ref, w1_ref, w3_ref, w2_ref, g_ref, b_ref, o_ref, ob_ref = refs
    j = pl.program_id(1)
    tm, d = o_ref.shape

    @pl.when(j == 0)
    def _():
        o_ref[...] = jnp.zeros_like(o_ref)
        if cast_x:
            xb_ref[...] = x_ref[...].astype(BF16)

    xb = xb_ref[...]
    h1 = _dot(xb, w1_ref[...].astype(BF16))
    h3 = _dot(xb, w3_ref[...].astype(BF16))
    hb = (h1 * jax.nn.sigmoid(h1) * h3).astype(BF16)
    w2b = w2_ref[...].astype(BF16)
    cw = d // n_chunk
    for c in range(n_chunk):
        o_ref[:, c * cw:(c + 1) * cw] += _dot(hb, w2b[:, c * cw:(c + 1) * cw])

    @pl.when(j == pl.num_programs(1) - 1)
    def _():
        def body(r, carry):
            rows = pl.ds(pl.multiple_of(r * row_chunk, row_chunk), row_chunk)
            y = alpha * x_ref[rows, :] + 0.5 * o_ref[rows, :]
            y = _ln_rows(y, g_ref[...], b_ref[...], LN_EPS)
            o_ref[rows, :] = y
            ob_ref[rows, :] = y.astype(BF16)
            return carry

        lax.fori_loop(0, tm // row_chunk, body, 0)


def _ffn_ln(x, xb, w1, w3, w2, g, b, alpha, tm, tf):
    m, d = x.shape
    f = w1.shape[1]
    cast_x = xb is None
    kern = functools.partial(_ffn_ln_kernel, alpha=alpha, n_chunk=4, row_chunk=min(tm, 128), cast_x=cast_x)
    row_tile = lambda i, j: (i, 0)
    return pl.pallas_call(
        kern,
        out_shape=[jax.ShapeDtypeStruct((m, d), F32), jax.ShapeDtypeStruct((m, d), BF16)],
        grid=(m // tm, f // tf),
        in_specs=[_resident((tm, d), row_tile)] * (1 if cast_x else 2) + [
            pl.BlockSpec((d, tf), lambda i, j: (0, j)),
            pl.BlockSpec((d, tf), lambda i, j: (0, j)),
            pl.BlockSpec((tf, d), lambda i, j: (j, 0)),
            pl.BlockSpec((1, d), lambda i, j: (0, 0)),
            pl.BlockSpec((1, d), lambda i, j: (0, 0)),
        ],
        out_specs=[_resident((tm, d), row_tile), _resident((tm, d), row_tile)],
        scratch_shapes=[pltpu.VMEM((tm, d), BF16)] if cast_x else [],
        compiler_params=_params("parallel", "arbitrary"),
        name="ffn_ln",
    )(*((x,) if cast_x else (x, xb)), w1, w3, w2, g, b)


def _mm_kernel(x_ref, w_ref, o_ref):
    o_ref[...] = _dot(x_ref[...].astype(BF16), w_ref[...].astype(BF16)).astype(o_ref.dtype)


def _mm(x, w, tm, tn, out_dtype=F32):
    m, k = x.shape
    n = w.shape[1]
    return pl.pallas_call(
        _mm_kernel,
        out_shape=jax.ShapeDtypeStruct((m, n), out_dtype),
        grid=(m // tm, n // tn),
        in_specs=[
            pl.BlockSpec((tm, k), lambda i, j: (i, 0)),
            pl.BlockSpec((k, tn), lambda i, j: (0, j)),
        ],
        out_specs=pl.BlockSpec((tm, tn), lambda i, j: (i, j)),
        compiler_params=_params("parallel", "arbitrary"),
        name="mm",
    )(x, w)


def _cast_kernel(x_ref, o_ref):
    o_ref[...] = x_ref[...].astype(o_ref.dtype)


def _cast_bf16(w, tr):
    rows, cols = w.shape
    return pl.pallas_call(
        _cast_kernel,
        out_shape=jax.ShapeDtypeStruct(w.shape, BF16),
        grid=(rows // tr,),
        in_specs=[pl.BlockSpec((tr, cols), lambda i: (i, 0))],
        out_specs=pl.BlockSpec((tr, cols), lambda i: (i, 0)),
        compiler_params=_params("parallel"),
        name="cast_bf16",
    )(w)


def _mm_res_ln_kernel(*refs, n_a, alpha, row_chunk):
    a_refs = refs[:n_a]
    w_refs = refs[n_a:2 * n_a]
    res_ref, g_ref, b_ref, o_ref, ob_ref, acc_ref = refs[2 * n_a:]
    tm, d = acc_ref.shape

    acc = _dot(a_refs[0][...], w_refs[0][...])
    for a_ref, w_ref in zip(a_refs[1:], w_refs[1:]):
        acc += _dot(a_ref[...], w_ref[...])
    acc_ref[...] = acc

    def body(r, carry):
        rows = pl.ds(pl.multiple_of(r * row_chunk, row_chunk), row_chunk)
        y = _ln_rows(alpha * res_ref[rows, :] + acc_ref[rows, :], g_ref[...], b_ref[...], LN_EPS)
        o_ref[rows, :] = y
        ob_ref[rows, :] = y.astype(BF16)
        return carry

    lax.fori_loop(0, tm // row_chunk, body, 0)


def _mm_res_ln(a_list, w_list, res, g, b, alpha, tm):
    m, d = res.shape
    n_a = len(a_list)
    kern = functools.partial(_mm_res_ln_kernel, n_a=n_a, alpha=alpha, row_chunk=min(tm, 64))
    row_tile = lambda i: (i, 0)
    in_specs = [pl.BlockSpec((tm, a.shape[1]), row_tile) for a in a_list]
    in_specs += [_resident((a.shape[1], d), functools.partial(lambda i, r: (r, 0), r=r))
                 for a, (_, r) in zip(a_list, w_list)]
    in_specs += [
        pl.BlockSpec((tm, d), row_tile),
        pl.BlockSpec((1, d), lambda i: (0, 0)),
        pl.BlockSpec((1, d), lambda i: (0, 0)),
    ]
    return pl.pallas_call(
        kern,
        out_shape=[jax.ShapeDtypeStruct((m, d), F32), jax.ShapeDtypeStruct((m, d), BF16)],
        grid=(m // tm,),
        in_specs=in_specs,
        out_specs=[pl.BlockSpec((tm, d), row_tile), pl.BlockSpec((tm, d), row_tile)],
        scratch_shapes=[pltpu.VMEM((tm, d), F32)],
        compiler_params=_params("parallel"),
        name="mm_res_ln",
    )(*a_list, *[w for w, _ in w_list], res, g, b)


def _shift_rows(cur, first_row):
    rolled = pltpu.roll(cur, 1, 0)
    row_id = lax.broadcasted_iota(jnp.int32, cur.shape, 0)
    return jnp.where(row_id == 0, first_row, rolled)


def _rwkv_prep_math(cur_rkv, prev_rkv, cur_lo, prev_lo, mu_rkv, mu_lo, w0, a0, k_k, k_a, r_k,
                    w2d, a2, g2, gsel, gselt, d_rwkv):
    pm = cur_rkv + mu_rkv * (prev_rkv - cur_rkv)
    pl_ = cur_lo + mu_lo * (prev_lo - cur_lo)
    r = pm[:, :d_rwkv]
    k = pm[:, d_rwkv:2 * d_rwkv]
    v = pm[:, 2 * d_rwkv:]
    wd = pl_[:, :DECAY_LORA]
    ad = pl_[:, DECAY_LORA:DECAY_LORA + ICLR_LORA]
    gd = pl_[:, DECAY_LORA + ICLR_LORA:]
    z = w0 + _dot(jnp.tanh(wd).astype(BF16), w2d.astype(BF16))
    w_log = -jax.nn.softplus(-z) - 0.5
    decay = jnp.exp(-jnp.exp(w_log))
    a = jax.nn.sigmoid(a0 + _dot(ad.astype(BF16), a2.astype(BF16)))
    g = _dot(jax.nn.sigmoid(gd).astype(BF16), g2.astype(BF16))
    kk = k * k_k
    norm = jnp.sqrt(_head_sum(kk * kk, gsel, gselt))
    kk = kk / jnp.maximum(norm, 1e-12)
    kmod = k * (1.0 + (a - 1.0) * k_a)
    bonus = _head_sum(r * kmod * r_k, gsel, gselt) * v
    return r, decay, kmod, v, kk, kk * a, g, bonus


def _rwkv_prep_seq_kernel(cur_rkv_ref, prev8_rkv_ref, cur_lo_ref, prev8_lo_ref, sp_rkv_ref, sp_lo_ref,
                          mu_rkv_ref, mu_lo_ref, w0_ref, a0_ref, kk_ref, ka_ref, rk_ref,
                          w2d_ref, a2_ref, g2_ref, gsel_ref, gselt_ref, vrep_ref,
                          r_o, w_o, k_o, kk_o, b_o, vx_o, g_o, bonus_o):
    first = pl.program_id(1) == 0
    cur_rkv = cur_rkv_ref[...]
    cur_lo = cur_lo_ref[...]
    last = V7X_SUBLANES - 1
    lanes = V7X_LANES
    first_rkv = jnp.where(first, sp_rkv_ref[...], prev8_rkv_ref[last:last + 1, :])
    first_lo = jnp.where(first, sp_lo_ref[...], prev8_lo_ref[last:last + 1, :])
    r, w, k, v, kk, b, g, bonus = _rwkv_prep_math(
        cur_rkv, _shift_rows(cur_rkv, first_rkv), cur_lo, _shift_rows(cur_lo, first_lo),
        mu_rkv_ref[...], mu_lo_ref[...], w0_ref[...], a0_ref[...], kk_ref[...], ka_ref[...], rk_ref[...],
        w2d_ref[...], a2_ref[...], g2_ref[...], gsel_ref[...], gselt_ref[...], g_o.shape[-1])
    g_o[...] = g
    bonus_o[...] = bonus
    vrep = vrep_ref[...]
    for c in range(r_o.shape[1]):
        cols = slice(c * lanes, (c + 1) * lanes)
        for o_ref, val in zip((r_o, w_o, k_o, kk_o, b_o), (r, w, k, kk, b)):
            o_ref[:, c, :] = val[:, cols]
        vx_o[:, c, :] = _split_dot(v[:, cols], vrep)


def _rwkv_prep_step_kernel(cur_rkv_ref, prev_rkv_ref, cur_lo_ref, prev_lo_ref,
                           mu_rkv_ref, mu_lo_ref, w0_ref, a0_ref, kk_ref, ka_ref, rk_ref,
                           w2d_ref, a2_ref, g2_ref, gsel_ref, gselt_ref,
                           r_o, w_o, k_o, v_o, kk_o, b_o, g_o, bonus_o):
    outs = _rwkv_prep_math(
        cur_rkv_ref[...], prev_rkv_ref[...], cur_lo_ref[...], prev_lo_ref[...],
        mu_rkv_ref[...], mu_lo_ref[...], w0_ref[...], a0_ref[...], kk_ref[...], ka_ref[...], rk_ref[...],
        w2d_ref[...], a2_ref[...], g2_ref[...], gsel_ref[...], gselt_ref[...], r_o.shape[-1])
    for o_ref, val in zip((r_o, w_o, k_o, v_o, kk_o, b_o, g_o, bonus_o), outs):
        o_ref[...] = val


def _const_spec(arr):
    nd = arr.ndim
    return pl.BlockSpec(arr.shape, lambda *_: (0,) * nd)


def _rwkv_prep_seq(p_rkv, p_lo, sp_rkv, sp_lo, consts, vrep, batch, seq, tt):
    m, n_rkv = p_rkv.shape
    n_lo = p_lo.shape[1]
    d_rwkv = n_rkv // 3
    nt = seq // tt
    sub = V7X_SUBLANES

    def cur_map(b, i):
        return (b * nt + i, 0)

    def prev_map(b, i):
        return (jnp.maximum((b * seq + i * tt) // sub - 1, 0), 0)

    lanes = V7X_LANES
    n_rows = d_rwkv // lanes
    cur_map3 = lambda b, i: (b * nt + i, 0, 0)
    out_shape = ([jax.ShapeDtypeStruct((m, n_rows, lanes), F32)] * 5
                 + [jax.ShapeDtypeStruct((m, n_rows, d_rwkv), F32)] + [jax.ShapeDtypeStruct((m, d_rwkv), F32)] * 2)
    out_specs = ([pl.BlockSpec((tt, n_rows, lanes), cur_map3)] * 5
                 + [pl.BlockSpec((tt, n_rows, d_rwkv), cur_map3)] + [pl.BlockSpec((tt, d_rwkv), cur_map)] * 2)
    return pl.pallas_call(
        _rwkv_prep_seq_kernel,
        out_shape=out_shape,
        grid=(batch, nt),
        in_specs=[
            pl.BlockSpec((tt, n_rkv), cur_map),
            pl.BlockSpec((sub, n_rkv), prev_map),
            pl.BlockSpec((tt, n_lo), cur_map),
            pl.BlockSpec((sub, n_lo), prev_map),
            pl.BlockSpec((None, 1, n_rkv), lambda b, i: (b, 0, 0)),
            pl.BlockSpec((None, 1, n_lo), lambda b, i: (b, 0, 0)),
        ] + [_const_spec(c) for c in consts] + [_const_spec(vrep)],
        out_specs=out_specs,
        compiler_params=_params("parallel", "arbitrary"),
        name="rwkv_prep_seq",
    )(p_rkv, p_rkv, p_lo, p_lo, sp_rkv, sp_lo, *consts, vrep)


def _rwkv_prep_step(p_rkv, prev_rkv, p_lo, prev_lo, consts, tb):
    m, n_rkv = p_rkv.shape
    n_lo = p_lo.shape[1]
    d_rwkv = n_rkv // 3
    out_sds = jax.ShapeDtypeStruct((m, d_rwkv), F32)
    row = lambda i: (i, 0)
    return pl.pallas_call(
        _rwkv_prep_step_kernel,
        out_shape=[out_sds] * 8,
        grid=(m // tb,),
        in_specs=[
            pl.BlockSpec((tb, n_rkv), row), pl.BlockSpec((tb, n_rkv), row),
            pl.BlockSpec((tb, n_lo), row), pl.BlockSpec((tb, n_lo), row),
        ] + [_const_spec(c) for c in consts],
        out_specs=[pl.BlockSpec((tb, d_rwkv), row)] * 8,
        compiler_params=_params("parallel"),
        name="rwkv_prep_step",
    )(p_rkv, prev_rkv, p_lo, prev_lo, *consts)


def _group_allsum(x, gsum):
    hi = x.astype(BF16)
    mid = (x - hi.astype(F32)).astype(BF16)
    both = _dot(jnp.concatenate([hi, mid], axis=0), gsum)
    return both[:x.shape[0]] + both[x.shape[0]:]


def _rwkv_seq_kernel(r_ref, w_ref, k_ref, kk_ref, b_ref, vx_ref, gsum_ref, y_ref, s_ref, *, group):
    nb, tb, n_kq, lanes = r_ref.shape
    sub = V7X_SUBLANES
    n_v = s_ref.shape[1] // nb
    n_vq = n_v // sub

    @pl.when(pl.program_id(0) == 0)
    def _():
        s_ref[...] = jnp.zeros_like(s_ref)

    lane_grp = lax.broadcasted_iota(jnp.int32, (sub, lanes), 1) // group
    gsum = gsum_ref[...]

    def update_state(t):
        def rows(ref, kq):
            return jnp.concatenate(
                [jnp.broadcast_to(ref[b, t, pl.ds(kq, 1), :], (n_v, lanes)) for b in range(nb)], axis=0)

        v_all = jnp.concatenate(
            [vx_ref[b, t, :, vq * lanes:(vq + 1) * lanes] for b in range(nb) for vq in range(n_vq)], axis=0)
        sa = s_ref[0] * rows(kk_ref, 0)
        for kq in range(1, n_kq):
            sa += s_ref[kq] * rows(kk_ref, kq)
        sa = _group_allsum(sa, gsum)
        y = None
        for kq in range(n_kq):
            s_new = s_ref[kq] * rows(w_ref, kq) - rows(b_ref, kq) * sa + rows(k_ref, kq) * v_all
            s_ref[kq] = s_new
            term = s_new * rows(r_ref, kq)
            y = term if y is None else y + term
        return y

    def emit_y(t, y):
        y = _group_allsum(y, gsum)
        for b in range(nb):
            yc = jnp.zeros((sub, lanes), F32)
            for vq in range(n_vq):
                lo = (b * n_vq + vq) * sub
                yc = jnp.where(lane_grp == vq, y[lo:lo + sub, :], yc)
            y_ref[b, t] = yc

    def step(t, y_prev):
        y_cur = update_state(t)
        emit_y(t - 1, y_prev)
        return y_cur

    emit_y(tb - 1, lax.fori_loop(1, tb, step, update_state(0)))


def _rwkv_seq(r, w, k, kk, b, vx, gsum, n_heads, tb):
    batch, seq, n_kq, lanes = r.shape
    sub = V7X_SUBLANES
    d_rwkv = n_kq * lanes
    assert lanes == V7X_LANES == sub * n_heads and d_rwkv == HEAD_DIM * n_heads and HEAD_DIM == sub * sub
    spec = pl.BlockSpec((batch, tb, n_kq, lanes), lambda i: (0, i, 0, 0))
    state_shape = (n_kq, batch * HEAD_DIM, lanes)
    return pl.pallas_call(
        functools.partial(_rwkv_seq_kernel, group=n_heads),
        out_shape=[jax.ShapeDtypeStruct((batch, seq, sub, lanes), F32),
                   jax.ShapeDtypeStruct(state_shape, F32)],
        grid=(seq // tb,),
        in_specs=[spec] * 5 + [pl.BlockSpec((batch, tb, sub, d_rwkv), lambda i: (0, i, 0, 0)), _const_spec(gsum)],
        out_specs=[pl.BlockSpec((batch, tb, sub, lanes), lambda i: (0, i, 0, 0)),
                   pl.BlockSpec(state_shape, lambda i: (0, 0, 0))],
        compiler_params=_params("arbitrary"),
        name="rwkv_seq",
    )(r, w, k, kk, b, vx, gsum)


def _rwkv_step_kernel(s_ref, r_ref, w_ref, k_ref, kk_ref, b_ref, v_ref, y_ref, so_ref):
    n = s_ref.shape[-1]
    eye = (lax.broadcasted_iota(jnp.int32, (n, n), 0) == lax.broadcasted_iota(jnp.int32, (n, n), 1)).astype(F32)
    s = s_ref[...]
    v_col = jnp.sum(eye * v_ref[...], axis=-1, keepdims=True)
    sa = jnp.sum(s * kk_ref[...], axis=-1, keepdims=True)
    s_new = s * w_ref[...] - sa * b_ref[...] + v_col * k_ref[...]
    so_ref[...] = s_new
    y_col = jnp.sum(s_new * r_ref[...], axis=-1, keepdims=True)
    y_ref[...] = jnp.sum(eye * y_col, axis=-2, keepdims=True)


def _rwkv_step(state, r, w, k, kk, b, v, sb):
    n_s, n_h, n, _ = state.shape
    sspec = pl.BlockSpec((sb, n_h, n, n), lambda i: (i, 0, 0, 0))
    vspec = pl.BlockSpec((sb, n_h, 1, n), lambda i: (i, 0, 0, 0))
    return pl.pallas_call(
        _rwkv_step_kernel,
        out_shape=[jax.ShapeDtypeStruct((n_s, n_h, 1, n), F32), jax.ShapeDtypeStruct(state.shape, F32)],
        grid=(n_s // sb,),
        in_specs=[sspec] + [vspec] * 6,
        out_specs=[vspec, sspec],
        compiler_params=_params("parallel"),
        name="rwkv_step",
    )(state, r, w, k, kk, b, v)


def _rwkv_post_kernel(y_ref, g_ref, bonus_ref, gng_ref, gnb_ref, beta_ref, gsel_ref, gselt_ref, o_ref):
    y = jnp.concatenate([y_ref[:, c, :] for c in range(y_ref.shape[1])], axis=-1)
    gsel = gsel_ref[...]
    gselt = gselt_ref[...]
    mu = _head_sum(y, gsel, gselt) * (1.0 / HEAD_DIM)
    yc = y - mu
    var = _head_sum(yc * yc, gsel, gselt) * (1.0 / HEAD_DIM)
    yn = yc * lax.rsqrt(var + GN_EPS) * gng_ref[...] + gnb_ref[...]
    o_ref[...] = ((yn + bonus_ref[...]) * g_ref[...] * beta_ref[...]).astype(o_ref.dtype)


def _rwkv_post(y, g, bonus, gn_g, gn_b, beta, gsel, gselt, tt):
    m, n_rows, lanes = y.shape
    d = n_rows * lanes
    row = pl.BlockSpec((tt, d), lambda i: (i, 0))
    consts = (gn_g, gn_b, beta, gsel, gselt)
    return pl.pallas_call(
        _rwkv_post_kernel,
        out_shape=jax.ShapeDtypeStruct((m, d), BF16),
        grid=(m // tt,),
        in_specs=[pl.BlockSpec((tt, n_rows, lanes), lambda i: (i, 0, 0)), row, row] + [_const_spec(c) for c in consts],
        out_specs=row,
        compiler_params=_params("parallel"),
        name="rwkv_post",
    )(y, g, bonus, *consts)


CONV_HALO = 32


def _glu(pc, d_conv):
    return pc[:, :d_conv] * jax.nn.sigmoid(pc[:, d_conv:])


def _conv_seq_kernel(pc_ref, halo_ref, cw_ref, cb_ref, lng_ref, lnb_ref, beta_ref, o_ref, st_ref, ubuf_ref,
                     *, row_chunk):
    tt, d_conv = o_ref.shape
    i = pl.program_id(1)
    n_hist = CONV_WIDTH - 1
    pad = CONV_HALO - n_hist
    u_halo = _glu(halo_ref[...], d_conv)
    ubuf_ref[0:CONV_HALO, :] = jnp.where(i == 0, jnp.zeros_like(u_halo), u_halo)
    ubuf_ref[CONV_HALO:, :] = _glu(pc_ref[...], d_conv)

    for base in range(0, tt, row_chunk):
        acc = ubuf_ref[base + pad:base + pad + row_chunk, :] * cw_ref[0:1, :]
        for w in range(1, CONV_WIDTH):
            acc += ubuf_ref[base + pad + w:base + pad + w + row_chunk, :] * cw_ref[w:w + 1, :]
        c = _ln_rows(acc + cb_ref[...], lng_ref[...], lnb_ref[...], LN_EPS)
        o_ref[base:base + row_chunk, :] = (c * jax.nn.sigmoid(c) * beta_ref[...]).astype(o_ref.dtype)

    @pl.when(i == pl.num_programs(1) - 1)
    def _():
        st_ref[...] = ubuf_ref[CONV_HALO + tt - n_hist:, :]


def _conv_seq(pc, cw, cb, ln_g, ln_b, beta, batch, seq, tt):
    m, two_d = pc.shape
    d_conv = two_d // 2
    nt = seq // tt
    hb = tt // CONV_HALO
    consts = (cw, cb, ln_g, ln_b, beta)
    kern = functools.partial(_conv_seq_kernel, row_chunk=16)
    return pl.pallas_call(
        kern,
        out_shape=[jax.ShapeDtypeStruct((m, d_conv), BF16),
                   jax.ShapeDtypeStruct((batch, CONV_WIDTH - 1, d_conv), F32)],
        grid=(batch, nt),
        in_specs=[
            pl.BlockSpec((tt, two_d), lambda b, i: (b * nt + i, 0)),
            pl.BlockSpec((CONV_HALO, two_d), lambda b, i: (jnp.maximum((b * nt + i) * hb - 1, 0), 0)),
        ] + [_const_spec(c) for c in consts],
        out_specs=[pl.BlockSpec((tt, d_conv), lambda b, i: (b * nt + i, 0)),
                   pl.BlockSpec((None, CONV_WIDTH - 1, d_conv), lambda b, i: (b, 0, 0))],
        scratch_shapes=[pltpu.VMEM((CONV_HALO + tt, d_conv), F32)],
        compiler_params=_params("parallel", "arbitrary"),
        name="conv_seq",
    )(pc, pc, *consts)


def _conv_step_kernel(pc_ref, st_ref, cw_ref, cb_ref, lng_ref, lnb_ref, beta_ref, o_ref, so_ref):
    d_conv = o_ref.shape[-1]
    n_hist = CONV_WIDTH - 1
    pc = pc_ref[...]
    u = pc[:, :, :d_conv] * jax.nn.sigmoid(pc[:, :, d_conv:])
    st = st_ref[...]
    c = (jnp.sum(st * cw_ref[0:n_hist, :], axis=1, keepdims=True)
         + u * cw_ref[n_hist:CONV_WIDTH, :] + cb_ref[...])
    c = _ln_rows(c, lng_ref[...], lnb_ref[...], LN_EPS)
    o_ref[...] = c * jax.nn.sigmoid(c) * beta_ref[...]
    so_ref[:, 0:n_hist - 1, :] = st[:, 1:, :]
    so_ref[:, n_hist - 1:n_hist, :] = u


def _conv_step(pc, state, cw, cb, ln_g, ln_b, beta, sb):
    n_s, _, two_d = pc.shape
    d_conv = two_d // 2
    consts = (cw, cb, ln_g, ln_b, beta)
    sspec = pl.BlockSpec((sb, CONV_WIDTH - 1, d_conv), lambda i: (i, 0, 0))
    return pl.pallas_call(
        _conv_step_kernel,
        out_shape=[jax.ShapeDtypeStruct((n_s, 1, d_conv), F32), jax.ShapeDtypeStruct(state.shape, F32)],
        grid=(n_s // sb,),
        in_specs=[pl.BlockSpec((sb, 1, two_d), lambda i: (i, 0, 0)), sspec] + [_const_spec(c) for c in consts],
        out_specs=[pl.BlockSpec((sb, 1, d_conv), lambda i: (i, 0, 0)), sspec],
        compiler_params=_params("parallel"),
        name="conv_step",
    )(pc, state, *consts)


def _attn_seq_kernel(q_ref, k_ref, v_ref, o_ref, *, scale):
    d = q_ref.shape[-1]
    dh = d // N_MEM_HEADS
    for h in range(N_MEM_HEADS):
        cols = slice(h * dh, (h + 1) * dh)
        k = k_ref[:, cols].astype(BF16)
        s = lax.dot_general(q_ref[:, cols], k, (((1,), (1,)), ((), ())), preferred_element_type=F32) * scale
        s = s - jnp.max(s, axis=-1, keepdims=True)
        e = jnp.exp(s)
        p = e / jnp.sum(e, axis=-1, keepdims=True)
        o_ref[:, cols] = _dot(p.astype(BF16), v_ref[:, cols].astype(BF16)).astype(o_ref.dtype)


def _attn_seq(q, mk, mv, batch, seq, tq):
    m, d = q.shape
    n_mem = mk.shape[1]
    nt = seq // tq
    kern = functools.partial(_attn_seq_kernel, scale=1.0 / math.sqrt(d // N_MEM_HEADS))
    kv_spec = pl.BlockSpec((None, n_mem, d), lambda b, i: (b, 0, 0))
    return pl.pallas_call(
        kern,
        out_shape=jax.ShapeDtypeStruct((m, d), BF16),
        grid=(batch, nt),
        in_specs=[pl.BlockSpec((tq, d), lambda b, i: (b * nt + i, 0)), kv_spec, kv_spec],
        out_specs=pl.BlockSpec((tq, d), lambda b, i: (b * nt + i, 0)),
        compiler_params=_params("parallel", "arbitrary"),
        name="attn_seq",
    )(q, mk, mv)


def _attn_step_kernel(q_ref, k_ref, v_ref, o_ref, *, scale):
    for i in range(q_ref.shape[0]):
        s = jnp.sum(k_ref[i] * q_ref[i], axis=-1, keepdims=True) * scale
        s = s - jnp.max(s, axis=0, keepdims=True)
        e = jnp.exp(s)
        p = e / jnp.sum(e, axis=0, keepdims=True)
        o_ref[i] = jnp.sum(p * v_ref[i], axis=0, keepdims=True)


def _attn_step(q, mk, mv, sb):
    n_s, _, n_h, dh = q.shape
    n_mem = mk.shape[1]
    kern = functools.partial(_attn_step_kernel, scale=1.0 / math.sqrt(dh))
    qspec = pl.BlockSpec((sb, 1, n_h, dh), lambda i: (i, 0, 0, 0))
    kvspec = pl.BlockSpec((sb, n_mem, n_h, dh), lambda i: (i, 0, 0, 0))
    return pl.pallas_call(
        kern,
        out_shape=jax.ShapeDtypeStruct(q.shape, F32),
        grid=(n_s // sb,),
        in_specs=[qspec, kvspec, kvspec],
        out_specs=qspec,
        compiler_params=_params("parallel"),
        name="attn_step",
    )(q, mk, mv)


def _head_selectors(d_rwkv, n_heads):
    head =jnp.arange(d_rwkv)[:, None] % n_heads
    col = jnp.arange(V7X_LANES)[None, :]
    gsel = (head == col).astype(BF16)
    assert n_heads <= V7X_LANES
    return gsel, gsel.T


def _to_key_order(x, n_heads):
    lead = x.shape[:-1]
    return jnp.swapaxes(x.reshape(*lead, n_heads, HEAD_DIM), -1, -2).reshape(*lead, n_heads * HEAD_DIM)


def _from_key_order(x, n_heads):
    lead = x.shape[:-1]
    return jnp.swapaxes(x.reshape(*lead, HEAD_DIM, n_heads), -1, -2).reshape(*lead, n_heads * HEAD_DIM)


def _to_value_order(x, n_heads):
    lead = x.shape[:-1]
    nd = len(lead)
    sub = V7X_SUBLANES
    x = x.reshape(*lead, n_heads, HEAD_DIM // sub, sub)
    return x.transpose(*range(nd), nd + 2, nd + 1, nd).reshape(*lead, n_heads * HEAD_DIM)


def _from_value_order(x, n_heads):
    lead = x.shape[:-1]
    nd = len(lead)
    sub = V7X_SUBLANES
    x = x.reshape(*lead, sub, HEAD_DIM // sub, n_heads)
    return x.transpose(*range(nd), nd + 2, nd + 1, nd).reshape(*lead, n_heads * HEAD_DIM)


def kernel(x_prompt, x_sample, mem_prompt, state_shift, state_conv, state_wkv, cache_mem_k, cache_mem_v,
           ffn1_w1, ffn1_w3, ffn1_w2, ln1_g, ln1_b, w_in, mu_shift, w0, w2_decay, a0, a2_iclr, g2_gate,
           k_k, k_a, r_k, gn_g, gn_b, conv_w, conv_b, conv_ln_g, conv_ln_b, beta_rwkv, beta_conv, w_out,
           ln2_g, ln2_b, w_mq, w_mk, w_mv, w_mo, ln3_g, ln3_b, ffn2_w1, ffn2_w3, ffn2_w2, ln4_g, ln4_b):
    depth = ffn1_w1.shape[0]
    assert depth == 1
    batch, seq, d = x_prompt.shape
    n_s = x_sample.shape[0]
    assert x_sample.shape[1] == 1
    n_mem = mem_prompt.shape[1]
    d_rwkv = w0.shape[-1]
    d_conv = conv_b.shape[-1]
    n_heads = d_rwkv // HEAD_DIM
    n_rkv = 3 * d_rwkv
    shift_cols = mu_shift.shape[-1]
    alpha = (2.0 * depth) ** 0.25
    l = 0

    tok = functools.partial(_to_key_order, n_heads=n_heads)
    tov = functools.partial(_to_value_order, n_heads=n_heads)
    fromk = functools.partial(_from_key_order, n_heads=n_heads)
    fromv = functools.partial(_from_value_order, n_heads=n_heads)

    def rkv_to_kernel_order(x):
        return jnp.concatenate([tok(x[..., :d_rwkv]), tok(x[..., d_rwkv:2 * d_rwkv]), tov(x[..., 2 * d_rwkv:n_rkv])],
                               axis=-1)

    def rkv_from_kernel_order(x):
        return jnp.concatenate([fromk(x[..., :d_rwkv]), fromk(x[..., d_rwkv:2 * d_rwkv]),
                                fromv(x[..., 2 * d_rwkv:n_rkv])], axis=-1)

    w_rkv = rkv_to_kernel_order(w_in[l, :, :n_rkv])
    w_lo = w_in[l, :, n_rkv:shift_cols]
    w_cv = w_in[l, :, shift_cols:]
    mu_rkv = rkv_to_kernel_order(mu_shift[l:l + 1, :n_rkv])
    mu_lo = mu_shift[l:l + 1, n_rkv:]
    gsel, gselt = _head_selectors(d_rwkv, n_heads)
    prep_consts = (mu_rkv, mu_lo, tok(w0[l:l + 1]), tok(a0[l:l + 1]), tok(k_k[l:l + 1]), tok(k_a[l:l + 1]),
                   tok(r_k[l].reshape(1, d_rwkv)), tok(w2_decay[l]), tok(a2_iclr[l]), tov(g2_gate[l]), gsel, gselt)
    cw = conv_w[l].reshape(CONV_WIDTH, d_conv)
    conv_consts = (cw, conv_b[l:l + 1], conv_ln_g[l:l + 1], conv_ln_b[l:l + 1], beta_conv[l:l + 1])
    post_consts = (tov(gn_g[l:l + 1]), tov(gn_b[l:l + 1]), tov(beta_rwkv[l:l + 1]), gsel, gselt)
    w_out_rows = jnp.concatenate([tov(w_out[l, :d_rwkv].T).T, w_out[l, d_rwkv:]], axis=0)
    w_out_b = _cast_bf16(w_out_rows, 512)
    w_mo_b = _cast_bf16(w_mo[l], 512)

    def trunk_front(x, tm):
        x1, x1b = _ffn_ln(x, None, ffn1_w1[l], ffn1_w3[l], ffn1_w2[l], ln1_g[l:l + 1], ln1_b[l:l + 1], alpha, tm, 256)
        p_rkv = _mm(x1b, w_rkv, tm, 512)
        p_lo = _mm(x1b, w_lo, tm, w_lo.shape[1])
        p_cv = _mm(x1b, w_cv, tm, 512)
        return x1, p_rkv, p_lo, p_cv

    def trunk_back(x1, o_a, o_b, attend, tm):
        tm_ln = min(tm, 512)
        x2, x2b = _mm_res_ln([o_a, o_b], [(w_out_b, 0), (w_out_b, 1)], x1, ln2_g[l:l + 1], ln2_b[l:l + 1], alpha, tm_ln)
        q = _mm(x2b, w_mq[l], tm, 512, BF16)
        att = attend(q)
        x3, x3b = _mm_res_ln([att], [(w_mo_b, 0)], x2, ln3_g[l:l + 1], ln3_b[l:l + 1], alpha, tm_ln)
        return _ffn_ln(x3, x3b, ffn2_w1[l], ffn2_w3[l], ffn2_w2[l], ln4_g[l:l + 1], ln4_b[l:l + 1], alpha, tm, 256)[0]

    m_p = batch * seq
    tm_p = 1024
    x1, p_rkv, p_lo, p_cv = trunk_front(x_prompt.reshape(m_p, d), tm_p)
    zeros_rkv = jnp.zeros((batch, 1, n_rkv), F32)
    zeros_lo = jnp.zeros((batch, 1, shift_cols - n_rkv), F32)
    sub = V7X_SUBLANES
    lane = jnp.arange(V7X_LANES)
    chan = jnp.arange(d_rwkv)
    gsum = (lane[:, None] % n_heads == lane[None, :] % n_heads).astype(BF16)
    vrep = ((lane[:, None] // n_heads == chan[None, :] // V7X_LANES)
            & (lane[:, None] % n_heads == chan[None, :] % n_heads)).astype(BF16)
    r, w, k, kk, bb, vx, g, bonus = _rwkv_prep_seq(p_rkv, p_lo, zeros_rkv, zeros_lo, prep_consts, vrep,
                                                   batch, seq, 256)
    bt = lambda t: t.reshape(batch, seq, *t.shape[1:])
    y4, s_t = _rwkv_seq(bt(r), bt(w), bt(k), bt(kk), bt(bb), bt(vx), gsum, n_heads, 32)
    wkv_p = s_t.reshape(HEAD_DIM // sub, batch, HEAD_DIM, sub, n_heads).transpose(1, 4, 2, 0, 3).reshape(
        batch, n_heads, HEAD_DIM, HEAD_DIM)
    o_a = _rwkv_post(y4.reshape(m_p, sub, V7X_LANES), g, bonus, *post_consts, 256)
    o_b, conv_p = _conv_seq(p_cv, *conv_consts, batch, seq, 256)

    mem2 = mem_prompt.reshape(batch * n_mem, d)
    mk_p = _mm(mem2, w_mk[l], batch * n_mem, 512)
    mv_p = _mm(mem2, w_mv[l], batch * n_mem, 512)
    attend_p = lambda q: _attn_seq(q, mk_p.reshape(batch, n_mem, d), mv_p.reshape(batch, n_mem, d), batch, seq, 512)
    y_prompt = trunk_back(x1, o_a, o_b, attend_p, tm_p).reshape(batch, seq, d)
    shift_p = jnp.concatenate([rkv_from_kernel_order(p_rkv.reshape(batch, seq, n_rkv)[:, -1]),
                               p_lo.reshape(batch, seq, -1)[:, -1]], axis=-1)
    dh = d // N_MEM_HEADS

    xs1, ps_rkv, ps_lo, ps_cv = trunk_front(x_sample.reshape(n_s, d), n_s)
    prev = state_shift[l]
    rs, ws, ks, vs, kks, bs, gs, bonus_s = _rwkv_prep_step(ps_rkv, rkv_to_kernel_order(prev[:, :n_rkv]), ps_lo,
                                                          prev[:, n_rkv:], prep_consts, n_s)
    hdk = lambda t: fromk(t).reshape(n_s, n_heads, 1, HEAD_DIM)
    ys, wkv_s = _rwkv_step(state_wkv[l], hdk(rs), hdk(ws), hdk(ks), hdk(kks), hdk(bs),
                           fromv(vs).reshape(n_s, n_heads, 1, HEAD_DIM), 8)
    ys = tov(ys.reshape(n_s, d_rwkv)).reshape(n_s, sub, V7X_LANES)
    o_as = _rwkv_post(ys, gs, bonus_s, *post_consts, n_s)
    o_bs, conv_s = _conv_step(ps_cv.reshape(n_s, 1, 2 * d_conv), state_conv[l], *conv_consts, 8)
    o_bs = o_bs.reshape(n_s, d_conv).astype(BF16)
    attend_s = lambda q: _attn_step(q.astype(F32).reshape(n_s, 1, N_MEM_HEADS, dh), cache_mem_k[l], cache_mem_v[l],
                                    2).reshape(n_s, d).astype(BF16)
    y_sample = trunk_back(xs1, o_as, o_bs, attend_s, n_s).reshape(n_s, 1, d)
    shift_s = jnp.concatenate([rkv_from_kernel_order(ps_rkv), ps_lo], axis=-1)

    return (y_prompt, y_sample,
            shift_p[None], conv_p[None], wkv_p[None],
            mk_p.reshape(1, batch, n_mem, N_MEM_HEADS, dh), mv_p.reshape(1, batch, n_mem, N_MEM_HEADS, dh),
            shift_s[None], conv_s[None], wkv_s[None])
```

```python
import functools
import math

import jax
import jax.numpy as jnp
import numpy as np
from jax import lax
from jax.experimental import pallas as pl
from jax.experimental.pallas import tpu as pltpu

F32 = jnp.float32
BF16 = jnp.bfloat16

V7X_LANES = 128
V7X_SUBLANES = 8
V7X_VMEM_LIMIT_BYTES = 56 * 1024 * 1024

HEAD_DIM = 64
CONV_WIDTH = 31
DECAY_LORA = 64
ICLR_LORA = 64
GATE_LORA = 160
N_MEM_HEADS = 4
LN_EPS = 1e-5
GN_EPS = 64e-5


def _params(*sem):
    return pltpu.CompilerParams(dimension_semantics=sem, vmem_limit_bytes=V7X_VMEM_LIMIT_BYTES)


def _resident(block_shape, index_map):
    return pl.BlockSpec(block_shape, index_map, pipeline_mode=pl.Buffered(1))


def _ln_rows(y, g, b, eps):
    mu = jnp.mean(y, axis=-1, keepdims=True)
    yc = y - mu
    var = jnp.mean(yc * yc, axis=-1, keepdims=True)
    return yc * lax.rsqrt(var + eps) * g + b


def _dot(a, b):
    return jnp.dot(a, b, preferred_element_type=F32)


def _split_dot(x, g):
    hi = x.astype(BF16)
    mid = (x - hi.astype(F32)).astype(BF16)
    return _dot(hi, g) + _dot(mid, g)


def _head_sum(x, gsel, gselt):
    return _split_dot(_split_dot(x, gsel), gselt)


def _ffn_ln_kernel(x_ref, *refs, alpha, n_chunk, row_chunk, cast_x):
    if cast_x:
        w1_ref, w3_ref, w2_ref, g_ref, b_ref, o_ref, ob_ref, xb_ref = refs
    else:
        xb_ref, w1_ref, w3_ref, w2_ref, g_ref, b_ref, o_ref, ob_ref = refs
    j = pl.program_id(1)
    tm, d = o_ref.shape

    @pl.when(j == 0)
    def _():
        o_ref[...] = jnp.zeros_like(o_ref)
        if cast_x:
            xb_ref[...] = x_ref[...].astype(BF16)

    xb = xb_ref[...]
    h1 = _dot(xb, w1_ref[...].astype(BF16))
    h3 = _dot(xb, w3_ref[...].astype(BF16))
    hb = (h1 * jax.nn.sigmoid(h1) * h3).astype(BF16)
    w2b = w2_ref[...].astype(BF16)
    cw = d // n_chunk
    for c in range(n_chunk):
        o_ref[:, c * cw:(c + 1) * cw] += _dot(hb, w2b[:, c * cw:(c + 1) * cw])

    @pl.when(j == pl.num_programs(1) - 1)
    def _():
        def body(r, carry):
            rows = pl.ds(pl.multiple_of(r * row_chunk, row_chunk), row_chunk)
            y = alpha * x_ref[rows, :] + 0.5 * o_ref[rows, :]
            y = _ln_rows(y, g_ref[...], b_ref[...], LN_EPS)
            o_ref[rows, :] = y
            ob_ref[rows, :] = y.astype(BF16)
            return carry

        lax.fori_loop(0, tm // row_chunk, body, 0)


def _ffn_ln(x, xb, w1, w3, w2, g, b, alpha, tm, tf):
    m, d = x.shape
    f = w1.shape[1]
    cast_x = xb is None
    kern = functools.partial(_ffn_ln_kernel, alpha=alpha, n_chunk=4, row_chunk=min(tm, 128), cast_x=cast_x)
    row_tile = lambda i, j: (i, 0)
    return pl.pallas_call(
        kern,
        out_shape=[jax.ShapeDtypeStruct((m, d), F32), jax.ShapeDtypeStruct((m, d), BF16)],
        grid=(m // tm, f // tf),
        in_specs=[_resident((tm, d), row_tile)] * (1 if cast_x else 2) + [
            pl.BlockSpec((d, tf), lambda i, j: (0, j)),
            pl.BlockSpec((d, tf), lambda i, j: (0, j)),
            pl.BlockSpec((tf, d), lambda i, j: (j, 0)),
            pl.BlockSpec((1, d), lambda i, j: (0, 0)),
            pl.BlockSpec((1, d), lambda i, j: (0, 0)),
        ],
        out_specs=[_resident((tm, d), row_tile), _resident((tm, d), row_tile)],
        scratch_shapes=[pltpu.VMEM((tm, d), BF16)] if cast_x else [],
        compiler_params=_params("parallel", "arbitrary"),
        name="ffn_ln",
    )(*((x,) if cast_x else (x, xb)), w1, w3, w2, g, b)


def _mm_kernel(x_ref, w_ref, o_ref):
    o_ref[...] = _dot(x_ref[...].astype(BF16), w_ref[...].astype(BF16)).astype(o_ref.dtype)


def _mm(x, w, tm, tn, out_dtype=F32):
    m, k = x.shape
    n = w.shape[1]
    return pl.pallas_call(
        _mm_kernel,
        out_shape=jax.ShapeDtypeStruct((m, n), out_dtype),
        grid=(m // tm, n // tn),
        in_specs=[
            pl.BlockSpec((tm, k), lambda i, j: (i, 0)),
            pl.BlockSpec((k, tn), lambda i, j: (0, j)),
        ],
        out_specs=pl.BlockSpec((tm, tn), lambda i, j: (i, j)),
        compiler_params=_params("parallel", "arbitrary"),
        name="mm",
    )(x, w)


def _cast_kernel(x_ref, o_ref):
    o_ref[...] = x_ref[...].astype(o_ref.dtype)


def _cast_bf16(w, tr):
    rows, cols = w.shape
    return pl.pallas_call(
        _cast_kernel,
        out_shape=jax.ShapeDtypeStruct(w.shape, BF16),
        grid=(rows // tr,),
        in_specs=[pl.BlockSpec((tr, cols), lambda i: (i, 0))],
        out_specs=pl.BlockSpec((tr, cols), lambda i: (i, 0)),
        compiler_params=_params("parallel"),
        name="cast_bf16",
    )(w)


def _mm_res_ln_kernel(*refs, n_a, alpha, row_chunk):
    a_refs = refs[:n_a]
    w_refs = refs[n_a:2 * n_a]
    res_ref, g_ref, b_ref, o_ref, ob_ref, acc_ref = refs[2 * n_a:]
    tm, d = acc_ref.shape

    acc = _dot(a_refs[0][...], w_refs[0][...])
    for a_ref, w_ref in zip(a_refs[1:], w_refs[1:]):
        acc += _dot(a_ref[...], w_ref[...])
    acc_ref[...] = acc

    def body(r, carry):
        rows = pl.ds(pl.multiple_of(r * row_chunk, row_chunk), row_chunk)
        y = _ln_rows(alpha * res_ref[rows, :] + acc_ref[rows, :], g_ref[...], b_ref[...], LN_EPS)
        o_ref[rows, :] = y
        ob_ref[rows, :] = y.astype(BF16)
        return carry

    lax.fori_loop(0, tm // row_chunk, body, 0)


def _mm_res_ln(a_list, w_list, res, g, b, alpha, tm):
    m, d = res.shape
    n_a = len(a_list)
    kern = functools.partial(_mm_res_ln_kernel, n_a=n_a, alpha=alpha, row_chunk=min(tm, 64))
    row_tile = lambda i: (i, 0)
    in_specs = [pl.BlockSpec((tm, a.shape[1]), row_tile) for a in a_list]
    in_specs += [_resident((a.shape[1], d), functools.partial(lambda i, r: (r, 0), r=r))
                 for a, (_, r) in zip(a_list, w_list)]
    in_specs += [
        pl.BlockSpec((tm, d), row_tile),
        pl.BlockSpec((1, d), lambda i: (0, 0)),
        pl.BlockSpec((1, d), lambda i: (0, 0)),
    ]
    return pl.pallas_call(
        kern,
        out_shape=[jax.ShapeDtypeStruct((m, d), F32), jax.ShapeDtypeStruct((m, d), BF16)],
        grid=(m // tm,),
        in_specs=in_specs,
        out_specs=[pl.BlockSpec((tm, d), row_tile), pl.BlockSpec((tm, d), row_tile)],
        scratch_shapes=[pltpu.VMEM((tm, d), F32)],
        compiler_params=_params("parallel"),
        name="mm_res_ln",
    )(*a_list, *[w for w, _ in w_list], res, g, b)


def _shift_rows(cur, first_row):
    rolled = pltpu.roll(cur, 1, 0)
    row_id = lax.broadcasted_iota(jnp.int32, cur.shape, 0)
    return jnp.where(row_id == 0, first_row, rolled)


def _rwkv_prep_math(cur_rkv, prev_rkv, cur_lo, prev_lo, mu_rkv, mu_lo, w0, a0, k_k, k_a, r_k,
                    w2d, a2, g2, gsel, gselt, d_rwkv):
    pm = cur_rkv + mu_rkv * (prev_rkv - cur_rkv)
    pl_ = cur_lo + mu_lo * (prev_lo - cur_lo)
    r = pm[:, :d_rwkv]
    k = pm[:, d_rwkv:2 * d_rwkv]
    v = pm[:, 2 * d_rwkv:]
    wd = pl_[:, :DECAY_LORA]
    ad = pl_[:, DECAY_LORA:DECAY_LORA + ICLR_LORA]
    gd = pl_[:, DECAY_LORA + ICLR_LORA:]
    z = w0 + _dot(jnp.tanh(wd).astype(BF16), w2d.astype(BF16))
    w_log = -jax.nn.softplus(-z) - 0.5
    decay = jnp.exp(-jnp.exp(w_log))
    a = jax.nn.sigmoid(a0 + _dot(ad.astype(BF16), a2.astype(BF16)))
    g = _dot(jax.nn.sigmoid(gd).astype(BF16), g2.astype(BF16))
    kk = k * k_k
    norm = jnp.sqrt(_head_sum(kk * kk, gsel, gselt))
    kk = kk / jnp.maximum(norm, 1e-12)
    kmod = k * (1.0 + (a - 1.0) * k_a)
    bonus = _head_sum(r * kmod * r_k, gsel, gselt) * v
    return r, decay, kmod, v, kk, kk * a, g, bonus


def _rwkv_prep_seq_kernel(cur_rkv_ref, prev8_rkv_ref, cur_lo_ref, prev8_lo_ref, sp_rkv_ref, sp_lo_ref,
                          mu_rkv_ref, mu_lo_ref, w0_ref, a0_ref, kk_ref, ka_ref, rk_ref,
                          w2d_ref, a2_ref, g2_ref, gsel_ref, gselt_ref, vrep_ref,
                          r_o, w_o, k_o, kk_o, b_o, vx_o, g_o, bonus_o):
    first = pl.program_id(1) == 0
    cur_rkv = cur_rkv_ref[...]
    cur_lo = cur_lo_ref[...]
    last = V7X_SUBLANES - 1
    lanes = V7X_LANES
    first_rkv = jnp.where(first, sp_rkv_ref[...], prev8_rkv_ref[last:last + 1, :])
    first_lo = jnp.where(first, sp_lo_ref[...], prev8_lo_ref[last:last + 1, :])
    r, w, k, v, kk, b, g, bonus = _rwkv_prep_math(
        cur_rkv, _shift_rows(cur_rkv, first_rkv), cur_lo, _shift_rows(cur_lo, first_lo),
        mu_rkv_ref[...], mu_lo_ref[...], w0_ref[...], a0_ref[...], kk_ref[...], ka_ref[...], rk_ref[...],
        w2d_ref[...], a2_ref[...], g2_ref[...], gsel_ref[...], gselt_ref[...], g_o.shape[-1])
    g_o[...] = g
    bonus_o[...] = bonus
    vrep = vrep_ref[...]
    for c in range(r_o.shape[1]):
        cols = slice(c * lanes, (c + 1) * lanes)
        for o_ref, val in zip((r_o, w_o, k_o, kk_o, b_o), (r, w, k, kk, b)):
            o_ref[:, c, :] = val[:, cols]
        vx_o[:, c, :] = _split_dot(v[:, cols], vrep)


def _rwkv_prep_step_kernel(cur_rkv_ref, prev_rkv_ref, cur_lo_ref, prev_lo_ref,
                           mu_rkv_ref, mu_lo_ref, w0_ref, a0_ref, kk_ref, ka_ref, rk_ref,
                           w2d_ref, a2_ref, g2_ref, gsel_ref, gselt_ref,
                           r_o, w_o, k_o, v_o, kk_o, b_o, g_o, bonus_o):
    outs = _rwkv_prep_math(
        cur_rkv_ref[...], prev_rkv_ref[...], cur_lo_ref[...], prev_lo_ref[...],
        mu_rkv_ref[...], mu_lo_ref[...], w0_ref[...], a0_ref[...], kk_ref[...], ka_ref[...], rk_ref[...],
        w2d_ref[...], a2_ref[...], g2_ref[...], gsel_ref[...], gselt_ref[...], r_o.shape[-1])
    for o_ref, val in zip((r_o, w_o, k_o, v_o, kk_o, b_o, g_o, bonus_o), outs):
        o_ref[...] = val


def _const_spec(arr):
    nd = arr.ndim
    return pl.BlockSpec(arr.shape, lambda *_: (0,) * nd)


def _rwkv_prep_seq(p_rkv, p_lo, sp_rkv, sp_lo, consts, vrep, batch, seq, tt):
    m, n_rkv = p_rkv.shape
    n_lo = p_lo.shape[1]
    d_rwkv = n_rkv // 3
    nt = seq // tt
    sub = V7X_SUBLANES

    def cur_map(b, i):
        return (b * nt + i, 0)

    def prev_map(b, i):
        return (jnp.maximum((b * seq + i * tt) // sub - 1, 0), 0)

    lanes = V7X_LANES
    n_rows = d_rwkv // lanes
    cur_map3 = lambda b, i: (b * nt + i, 0, 0)
    out_shape = ([jax.ShapeDtypeStruct((m, n_rows, lanes), F32)] * 5
                 + [jax.ShapeDtypeStruct((m, n_rows, d_rwkv), F32)] + [jax.ShapeDtypeStruct((m, d_rwkv), F32)] * 2)
    out_specs = ([pl.BlockSpec((tt, n_rows, lanes), cur_map3)] * 5
                 + [pl.BlockSpec((tt, n_rows, d_rwkv), cur_map3)] + [pl.BlockSpec((tt, d_rwkv), cur_map)] * 2)
    return pl.pallas_call(
        _rwkv_prep_seq_kernel,
        out_shape=out_shape,
        grid=(batch, nt),
        in_specs=[
            pl.BlockSpec((tt, n_rkv), cur_map),
            pl.BlockSpec((sub, n_rkv), prev_map),
            pl.BlockSpec((tt, n_lo), cur_map),
            pl.BlockSpec((sub, n_lo), prev_map),
            pl.BlockSpec((None, 1, n_rkv), lambda b, i: (b, 0, 0)),
            pl.BlockSpec((None, 1, n_lo), lambda b, i: (b, 0, 0)),
        ] + [_const_spec(c) for c in consts] + [_const_spec(vrep)],
        out_specs=out_specs,
        compiler_params=_params("parallel", "arbitrary"),
        name="rwkv_prep_seq",
    )(p_rkv, p_rkv, p_lo, p_lo, sp_rkv, sp_lo, *consts, vrep)


def _rwkv_prep_step(p_rkv, prev_rkv, p_lo, prev_lo, consts, tb):
    m, n_rkv = p_rkv.shape
    n_lo = p_lo.shape[1]
    d_rwkv = n_rkv // 3
    out_sds = jax.ShapeDtypeStruct((m, d_rwkv), F32)
    row = lambda i: (i, 0)
    return pl.pallas_call(
        _rwkv_prep_step_kernel,
        out_shape=[out_sds] * 8,
        grid=(m // tb,),
        in_specs=[
            pl.BlockSpec((tb, n_rkv), row), pl.BlockSpec((tb, n_rkv), row),
            pl.BlockSpec((tb, n_lo), row), pl.BlockSpec((tb, n_lo), row),
        ] + [_const_spec(c) for c in consts],
        out_specs=[pl.BlockSpec((tb, d_rwkv), row)] * 8,
        compiler_params=_params("parallel"),
        name="rwkv_prep_step",
    )(p_rkv, prev_rkv, p_lo, prev_lo, *consts)


def _group_allsum(x, gsum):
    hi = x.astype(BF16)
    mid = (x - hi.astype(F32)).astype(BF16)
    both = _dot(jnp.concatenate([hi, mid], axis=0), gsum)
    return both[:x.shape[0]] + both[x.shape[0]:]


def _rwkv_seq_kernel(r_ref, w_ref, k_ref, kk_ref, b_ref, vx_ref, gsum_ref, y_ref, s_ref, *, group):
    nb, tb, n_kq, lanes = r_ref.shape
    sub = V7X_SUBLANES
    n_v = s_ref.shape[1] // nb
    n_vq = n_v // sub

    @pl.when(pl.program_id(0) == 0)
    def _():
        s_ref[...] = jnp.zeros_like(s_ref)

    lane_grp = lax.broadcasted_iota(jnp.int32, (sub, lanes), 1) // group
    gsum = gsum_ref[...]

    def update_state(t):
        def rows(ref, kq):
            return jnp.concatenate(
                [jnp.broadcast_to(ref[b, t, pl.ds(kq, 1), :], (n_v, lanes)) for b in range(nb)], axis=0)

        v_all = jnp.concatenate(
            [vx_ref[b, t, :, vq * lanes:(vq + 1) * lanes] for b in range(nb) for vq in range(n_vq)], axis=0)
        sa = s_ref[0] * rows(kk_ref, 0)
        for kq in range(1, n_kq):
            sa += s_ref[kq] * rows(kk_ref, kq)
        sa = _group_allsum(sa, gsum)
        y = None
        for kq in range(n_kq):
            s_new = s_ref[kq] * rows(w_ref, kq) - rows(b_ref, kq) * sa + rows(k_ref, kq) * v_all
            s_ref[kq] = s_new
            term = s_new * rows(r_ref, kq)
            y = term if y is None else y + term
        return y

    def emit_y(t, y):
        y = _group_allsum(y, gsum)
        for b in range(nb):
            yc = jnp.zeros((sub, lanes), F32)
            for vq in range(n_vq):
                lo = (b * n_vq + vq) * sub
                yc = jnp.where(lane_grp == vq, y[lo:lo + sub, :], yc)
            y_ref[b, t] = yc

    def step(t, y_prev):
        y_cur = update_state(t)
        emit_y(t - 1, y_prev)
        return y_cur

    emit_y(tb - 1, lax.fori_loop(1, tb, step, update_state(0)))


def _rwkv_seq(r, w, k, kk, b, vx, gsum, n_heads, tb):
    batch, seq, n_kq, lanes = r.shape
    sub = V7X_SUBLANES
    d_rwkv = n_kq * lanes
    assert lanes == V7X_LANES == sub * n_heads and d_rwkv == HEAD_DIM * n_heads and HEAD_DIM == sub * sub
    spec = pl.BlockSpec((batch, tb, n_kq, lanes), lambda i: (0, i, 0, 0))
    state_shape = (n_kq, batch * HEAD_DIM, lanes)
    return pl.pallas_call(
        functools.partial(_rwkv_seq_kernel, group=n_heads),
        out_shape=[jax.ShapeDtypeStruct((batch, seq, sub, lanes), F32),
                   jax.ShapeDtypeStruct(state_shape, F32)],
        grid=(seq // tb,),
        in_specs=[spec] * 5 + [pl.BlockSpec((batch, tb, sub, d_rwkv), lambda i: (0, i, 0, 0)), _const_spec(gsum)],
        out_specs=[pl.BlockSpec((batch, tb, sub, lanes), lambda i: (0, i, 0, 0)),
                   pl.BlockSpec(state_shape, lambda i: (0, 0, 0))],
        compiler_params=_params("arbitrary"),
        name="rwkv_seq",
    )(r, w, k, kk, b, vx, gsum)


def _rwkv_step_kernel(s_ref, r_ref, w_ref, k_ref, kk_ref, b_ref, v_ref, y_ref, so_ref):
    n = s_ref.shape[-1]
    eye = (lax.broadcasted_iota(jnp.int32, (n, n), 0) == lax.broadcasted_iota(jnp.int32, (n, n), 1)).astype(F32)
    s = s_ref[...]
    v_col = jnp.sum(eye * v_ref[...], axis=-1, keepdims=True)
    sa = jnp.sum(s * kk_ref[...], axis=-1, keepdims=True)
    s_new = s * w_ref[...] - sa * b_ref[...] + v_col * k_ref[...]
    so_ref[...] = s_new
    y_col = jnp.sum(s_new * r_ref[...], axis=-1, keepdims=True)
    y_ref[...] = jnp.sum(eye * y_col, axis=-2, keepdims=True)


def _rwkv_step(state, r, w, k, kk, b, v, sb):
    n_s, n_h, n, _ = state.shape
    sspec = pl.BlockSpec((sb, n_h, n, n), lambda i: (i, 0, 0, 0))
    vspec = pl.BlockSpec((sb, n_h, 1, n), lambda i: (i, 0, 0, 0))
    return pl.pallas_call(
        _rwkv_step_kernel,
        out_shape=[jax.ShapeDtypeStruct((n_s, n_h, 1, n), F32), jax.ShapeDtypeStruct(state.shape, F32)],
        grid=(n_s // sb,),
        in_specs=[sspec] + [vspec] * 6,
        out_specs=[vspec, sspec],
        compiler_params=_params("parallel"),
        name="rwkv_step",
    )(state, r, w, k, kk, b, v)


def _rwkv_post_kernel(y_ref, g_ref, bonus_ref, gng_ref, gnb_ref, beta_ref, gsel_ref, gselt_ref, o_ref):
    y = jnp.concatenate([y_ref[:, c, :] for c in range(y_ref.shape[1])], axis=-1)
    gsel = gsel_ref[...]
    gselt = gselt_ref[...]
    mu = _head_sum(y, gsel, gselt) * (1.0 / HEAD_DIM)
    yc = y - mu
    var = _head_sum(yc * yc, gsel, gselt) * (1.0 / HEAD_DIM)
    yn = yc * lax.rsqrt(var + GN_EPS) * gng_ref[...] + gnb_ref[...]
    o_ref[...] = ((yn + bonus_ref[...]) * g_ref[...] * beta_ref[...]).astype(o_ref.dtype)


def _rwkv_post(y, g, bonus, gn_g, gn_b, beta, gsel, gselt, tt):
    m, n_rows, lanes = y.shape
    d = n_rows * lanes
    row = pl.BlockSpec((tt, d), lambda i: (i, 0))
    consts = (gn_g, gn_b, beta, gsel, gselt)
    return pl.pallas_call(
        _rwkv_post_kernel,
        out_shape=jax.ShapeDtypeStruct((m, d), BF16),
        grid=(m // tt,),
        in_specs=[pl.BlockSpec((tt, n_rows, lanes), lambda i: (i, 0, 0)), row, row] + [_const_spec(c) for c in consts],
        out_specs=row,
        compiler_params=_params("parallel"),
        name="rwkv_post",
    )(y, g, bonus, *consts)


CONV_HALO = 32


def _glu(pc, d_conv):
    return pc[:, :d_conv] * jax.nn.sigmoid(pc[:, d_conv:])


def _conv_seq_kernel(pc_ref, halo_ref, cw_ref, cb_ref, lng_ref, lnb_ref, beta_ref, o_ref, st_ref, ubuf_ref,
                     ushift_ref, *, row_chunk):
    tt, d_conv = o_ref.shape
    i = pl.program_id(1)
    sub = V7X_SUBLANES
    n_hist = CONV_WIDTH - 1
    pad = CONV_HALO - n_hist
    u_halo = _glu(halo_ref[...], d_conv)
    ubuf_ref[0:CONV_HALO, :] = jnp.where(i == 0, jnp.zeros_like(u_halo), u_halo)
    ubuf_ref[CONV_HALO:, :] = _glu(pc_ref[...], d_conv)
    n_shift_rows = ushift_ref.shape[1]
    for s in range(1, sub):
        ushift_ref[s - 1] = ubuf_ref[s:s + n_shift_rows, :]

    def tap(base, w):
        off = pad + w
        q, s = off // sub, off % sub
        rows = slice(base + q * sub, base + q * sub + row_chunk)
        return (ubuf_ref[rows, :] if s == 0 else ushift_ref[s - 1, rows, :]) * cw_ref[w:w + 1, :]

    for base in range(0, tt, row_chunk):
        acc = tap(base, 0)
        for w in range(1, CONV_WIDTH):
            acc += tap(base, w)
        c = _ln_rows(acc + cb_ref[...], lng_ref[...], lnb_ref[...], LN_EPS)
        o_ref[base:base + row_chunk, :] = (c * jax.nn.sigmoid(c) * beta_ref[...]).astype(o_ref.dtype)

    @pl.when(i == pl.num_programs(1) - 1)
    def _():
        st_ref[...] = ubuf_ref[CONV_HALO + tt - n_hist:, :]


def _conv_seq(pc, cw, cb, ln_g, ln_b, beta, batch, seq, tt):
    m, two_d = pc.shape
    d_conv = two_d // 2
    nt = seq // tt
    hb = tt // CONV_HALO
    consts = (cw, cb, ln_g, ln_b, beta)
    kern = functools.partial(_conv_seq_kernel, row_chunk=16)
    return pl.pallas_call(
        kern,
        out_shape=[jax.ShapeDtypeStruct((m, d_conv), BF16),
                   jax.ShapeDtypeStruct((batch, CONV_WIDTH - 1, d_conv), F32)],
        grid=(batch, nt),
        in_specs=[
            pl.BlockSpec((tt, two_d), lambda b, i: (b * nt + i, 0)),
            pl.BlockSpec((CONV_HALO, two_d), lambda b, i: (jnp.maximum((b * nt + i) * hb - 1, 0), 0)),
        ] + [_const_spec(c) for c in consts],
        out_specs=[pl.BlockSpec((tt, d_conv), lambda b, i: (b * nt + i, 0)),
                   pl.BlockSpec((None, CONV_WIDTH - 1, d_conv), lambda b, i: (b, 0, 0))],
        scratch_shapes=[pltpu.VMEM((CONV_HALO + tt, d_conv), F32),
                        pltpu.VMEM((V7X_SUBLANES - 1, CONV_HALO + tt - V7X_SUBLANES, d_conv), F32)],
        compiler_params=_params("parallel", "arbitrary"),
        name="conv_seq",
    )(pc, pc, *consts)


def _conv_step_kernel(pc_ref, st_ref, cw_ref, cb_ref, lng_ref, lnb_ref, beta_ref, o_ref, so_ref):
    d_conv = o_ref.shape[-1]
    n_hist = CONV_WIDTH - 1
    pc = pc_ref[...]
    u = pc[:, :, :d_conv] * jax.nn.sigmoid(pc[:, :, d_conv:])
    st = st_ref[...]
    c = (jnp.sum(st * cw_ref[0:n_hist, :], axis=1, keepdims=True)
         + u * cw_ref[n_hist:CONV_WIDTH, :] + cb_ref[...])
    c = _ln_rows(c, lng_ref[...], lnb_ref[...], LN_EPS)
    o_ref[...] = c * jax.nn.sigmoid(c) * beta_ref[...]
    so_ref[:, 0:n_hist - 1, :] = st[:, 1:, :]
    so_ref[:, n_hist - 1:n_hist, :] = u


def _conv_step(pc, state, cw, cb, ln_g, ln_b, beta, sb):
    n_s, _, two_d = pc.shape
    d_conv = two_d // 2
    consts = (cw, cb, ln_g, ln_b, beta)
    sspec = pl.BlockSpec((sb, CONV_WIDTH - 1, d_conv), lambda i: (i, 0, 0))
    return pl.pallas_call(
        _conv_step_kernel,
        out_shape=[jax.ShapeDtypeStruct((n_s, 1, d_conv), F32), jax.ShapeDtypeStruct(state.shape, F32)],
        grid=(n_s // sb,),
        in_specs=[pl.BlockSpec((sb, 1, two_d), lambda i: (i, 0, 0)), sspec] + [_const_spec(c) for c in consts],
        out_specs=[pl.BlockSpec((sb, 1, d_conv), lambda i: (i, 0, 0)), sspec],
        compiler_params=_params("parallel"),
        name="conv_step",
    )(pc, state, *consts)


def _attn_seq_kernel(q_ref, k_ref, v_ref, o_ref, *, scale):
    d = q_ref.shape[-1]
    dh = d // N_MEM_HEADS
    for h in range(N_MEM_HEADS):
        cols = slice(h * dh, (h + 1) * dh)
        k = k_ref[:, cols].astype(BF16)
        s = lax.dot_general(q_ref[:, cols], k, (((1,), (1,)), ((), ())), preferred_element_type=F32) * scale
        s = s - jnp.max(s, axis=-1, keepdims=True)
        e = jnp.exp(s)
        p = e / jnp.sum(e, axis=-1, keepdims=True)
        o_ref[:, cols] = _dot(p.astype(BF16), v_ref[:, cols].astype(BF16)).astype(o_ref.dtype)


def _attn_seq(q, mk, mv, batch, seq, tq):
    m, d = q.shape
    n_mem = mk.shape[1]
    nt = seq // tq
    kern = functools.partial(_attn_seq_kernel, scale=1.0 / math.sqrt(d // N_MEM_HEADS))
    kv_spec = pl.BlockSpec((None, n_mem, d), lambda b, i: (b, 0, 0))
    return pl.pallas_call(
        kern,
        out_shape=jax.ShapeDtypeStruct((m, d), BF16),
        grid=(batch, nt),
        in_specs=[pl.BlockSpec((tq, d), lambda b, i: (b * nt + i, 0)), kv_spec, kv_spec],
        out_specs=pl.BlockSpec((tq, d), lambda b, i: (b * nt + i, 0)),
        compiler_params=_params("parallel", "arbitrary"),
        name="attn_seq",
    )(q, mk, mv)


def _attn_step_kernel(q_ref, k_ref, v_ref, o_ref, *, scale):
    for i in range(q_ref.shape[0]):
        s = jnp.sum(k_ref[i] * q_ref[i], axis=-1, keepdims=True) * scale
        s = s - jnp.max(s, axis=0, keepdims=True)
        e = jnp.exp(s)
        p = e / jnp.sum(e, axis=0, keepdims=True)
        o_ref[i] = jnp.sum(p * v_ref[i], axis=0, keepdims=True)


def _attn_step(q, mk, mv, sb):
    n_s, _, n_h, dh = q.shape
    n_mem = mk.shape[1]
    kern = functools.partial(_attn_step_kernel, scale=1.0 / math.sqrt(dh))
    qspec = pl.BlockSpec((sb, 1, n_h, dh), lambda i: (i, 0, 0, 0))
    kvspec = pl.BlockSpec((sb, n_mem, n_h, dh), lambda i: (i, 0, 0, 0))
    return pl.pallas_call(
        kern,
        out_shape=jax.ShapeDtypeStruct(q.shape, F32),
        grid=(n_s // sb,),
        in_specs=[qspec, kvspec, kvspec],
        out_specs=qspec,
        compiler_params=_params("parallel"),
        name="attn_step",
    )(q, mk, mv)


def _head_selectors(d_rwkv, n_heads):
    assert n_heads <= V7X_LANES
    lane = np.arange(V7X_LANES)
    chan = np.arange(d_rwkv)
    gsel = chan[:, None] % n_heads == lane[None, :]
    gsum = lane[:, None] % n_heads == lane[None, :] % n_heads
    vrep = ((lane[:, None] // n_heads == chan[None, :] // V7X_LANES)
            & (lane[:, None] % n_heads == chan[None, :] % n_heads))
    as_bf16 = lambda a: jnp.asarray(a.astype(np.float32), BF16)
    return as_bf16(gsel), as_bf16(gsel.T), as_bf16(gsum), as_bf16(vrep)


def _to_key_order(x, n_heads):
    lead = x.shape[:-1]
    return jnp.swapaxes(x.reshape(*lead, n_heads, HEAD_DIM), -1, -2).reshape(*lead, n_heads * HEAD_DIM)


def _from_key_order(x, n_heads):
    lead = x.shape[:-1]
    return jnp.swapaxes(x.reshape(*lead, HEAD_DIM, n_heads), -1, -2).reshape(*lead, n_heads * HEAD_DIM)


def _to_value_order(x, n_heads):
    lead = x.shape[:-1]
    nd = len(lead)
    sub = V7X_SUBLANES
    x = x.reshape(*lead, n_heads, HEAD_DIM // sub, sub)
    return x.transpose(*range(nd), nd + 2, nd + 1, nd).reshape(*lead, n_heads * HEAD_DIM)


def _from_value_order(x, n_heads):
    lead = x.shape[:-1]
    nd = len(lead)
    sub = V7X_SUBLANES
    x = x.reshape(*lead, sub, HEAD_DIM // sub, n_heads)
    return x.transpose(*range(nd), nd + 2, nd + 1, nd).reshape(*lead, n_heads * HEAD_DIM)


def kernel(x_prompt, x_sample, mem_prompt, state_shift, state_conv, state_wkv, cache_mem_k, cache_mem_v,
           ffn1_w1, ffn1_w3, ffn1_w2, ln1_g, ln1_b, w_in, mu_shift, w0, w2_decay, a0, a2_iclr, g2_gate,
           k_k, k_a, r_k, gn_g, gn_b, conv_w, conv_b, conv_ln_g, conv_ln_b, beta_rwkv, beta_conv, w_out,
           ln2_g, ln2_b, w_mq, w_mk, w_mv, w_mo, ln3_g, ln3_b, ffn2_w1, ffn2_w3, ffn2_w2, ln4_g, ln4_b):
    depth = ffn1_w1.shape[0]
    assert depth == 1
    batch, seq, d = x_prompt.shape
    n_s = x_sample.shape[0]
    assert x_sample.shape[1] == 1
    n_mem = mem_prompt.shape[1]
    d_rwkv = w0.shape[-1]
    d_conv = conv_b.shape[-1]
    n_heads = d_rwkv // HEAD_DIM
    n_rkv = 3 * d_rwkv
    shift_cols = mu_shift.shape[-1]
    alpha = (2.0 * depth) ** 0.25
    l = 0

    tok = functools.partial(_to_key_order, n_heads=n_heads)
    tov = functools.partial(_to_value_order, n_heads=n_heads)
    fromk = functools.partial(_from_key_order, n_heads=n_heads)
    fromv = functools.partial(_from_value_order, n_heads=n_heads)

    def rkv_to_kernel_order(x):
        return jnp.concatenate([tok(x[..., :d_rwkv]), tok(x[..., d_rwkv:2 * d_rwkv]), tov(x[..., 2 * d_rwkv:n_rkv])],
                               axis=-1)

    def rkv_from_kernel_order(x):
        return jnp.concatenate([fromk(x[..., :d_rwkv]), fromk(x[..., d_rwkv:2 * d_rwkv]),
                                fromv(x[..., 2 * d_rwkv:n_rkv])], axis=-1)

    w_rkv = rkv_to_kernel_order(w_in[l, :, :n_rkv])
    w_lo = w_in[l, :, n_rkv:shift_cols]
    w_cv = w_in[l, :, shift_cols:]
    mu_rkv = rkv_to_kernel_order(mu_shift[l:l + 1, :n_rkv])
    mu_lo = mu_shift[l:l + 1, n_rkv:]
    gsel, gselt, gsum, vrep = _head_selectors(d_rwkv, n_heads)
    prep_consts = (mu_rkv, mu_lo, tok(w0[l:l + 1]), tok(a0[l:l + 1]), tok(k_k[l:l + 1]), tok(k_a[l:l + 1]),
                   tok(r_k[l].reshape(1, d_rwkv)), tok(w2_decay[l]), tok(a2_iclr[l]), tov(g2_gate[l]), gsel, gselt)
    cw = conv_w[l].reshape(CONV_WIDTH, d_conv)
    conv_consts = (cw, conv_b[l:l + 1], conv_ln_g[l:l + 1], conv_ln_b[l:l + 1], beta_conv[l:l + 1])
    post_consts = (tov(gn_g[l:l + 1]), tov(gn_b[l:l + 1]), tov(beta_rwkv[l:l + 1]), gsel, gselt)
    w_out_rows = jnp.concatenate([tov(w_out[l, :d_rwkv].T).T, w_out[l, d_rwkv:]], axis=0)
    w_out_b = _cast_bf16(w_out_rows, 512)
    w_mo_b = _cast_bf16(w_mo[l], 512)

    def trunk_front(x, tm):
        x1, x1b = _ffn_ln(x, None, ffn1_w1[l], ffn1_w3[l], ffn1_w2[l], ln1_g[l:l + 1], ln1_b[l:l + 1], alpha, tm, 256)
        p_rkv = _mm(x1b, w_rkv, tm, 512)
        p_lo = _mm(x1b, w_lo, tm, w_lo.shape[1])
        p_cv = _mm(x1b, w_cv, tm, 512)
        return x1, p_rkv, p_lo, p_cv

    def trunk_back(x1, o_a, o_b, attend, tm):
        tm_ln = min(tm, 512)
        x2, x2b = _mm_res_ln([o_a, o_b], [(w_out_b, 0), (w_out_b, 1)], x1, ln2_g[l:l + 1], ln2_b[l:l + 1], alpha, tm_ln)
        q = _mm(x2b, w_mq[l], tm, 512, BF16)
        att = attend(q)
        x3, x3b = _mm_res_ln([att], [(w_mo_b, 0)], x2, ln3_g[l:l + 1], ln3_b[l:l + 1], alpha, tm_ln)
        return _ffn_ln(x3, x3b, ffn2_w1[l], ffn2_w3[l], ffn2_w2[l], ln4_g[l:l + 1], ln4_b[l:l + 1], alpha, tm, 256)[0]

    m_p = batch * seq
    tm_p = 1024
    x1, p_rkv, p_lo, p_cv = trunk_front(x_prompt.reshape(m_p, d), tm_p)
    zeros_rkv = jnp.zeros((batch, 1, n_rkv), F32)
    zeros_lo = jnp.zeros((batch, 1, shift_cols - n_rkv), F32)
    sub = V7X_SUBLANES
    r, w, k, kk, bb, vx, g, bonus = _rwkv_prep_seq(p_rkv, p_lo, zeros_rkv, zeros_lo, prep_consts, vrep,
                                                   batch, seq, 256)
    bt = lambda t: t.reshape(batch, seq, *t.shape[1:])
    y4, s_t = _rwkv_seq(bt(r), bt(w), bt(k), bt(kk), bt(bb), bt(vx), gsum, n_heads, 64)
    wkv_p = s_t.reshape(HEAD_DIM // sub, batch, HEAD_DIM, sub, n_heads).transpose(1, 4, 2, 0, 3).reshape(
        batch, n_heads, HEAD_DIM, HEAD_DIM)
    o_a = _rwkv_post(y4.reshape(m_p, sub, V7X_LANES), g, bonus, *post_consts, 256)
    o_b, conv_p = _conv_seq(p_cv, *conv_consts, batch, seq, 256)

    mem2 = mem_prompt.reshape(batch * n_mem, d)
    mk_p = _mm(mem2, w_mk[l], batch * n_mem, 512)
    mv_p = _mm(mem2, w_mv[l], batch * n_mem, 512)
    attend_p = lambda q: _attn_seq(q, mk_p.reshape(batch, n_mem, d), mv_p.reshape(batch, n_mem, d), batch, seq, 512)
    y_prompt = trunk_back(x1, o_a, o_b, attend_p, tm_p).reshape(batch, seq, d)
    shift_p = jnp.concatenate([rkv_from_kernel_order(p_rkv.reshape(batch, seq, n_rkv)[:, -1]),
                               p_lo.reshape(batch, seq, -1)[:, -1]], axis=-1)
    dh = d // N_MEM_HEADS

    xs1, ps_rkv, ps_lo, ps_cv = trunk_front(x_sample.reshape(n_s, d), n_s)
    prev = state_shift[l]
    rs, ws, ks, vs, kks, bs, gs, bonus_s = _rwkv_prep_step(ps_rkv, rkv_to_kernel_order(prev[:, :n_rkv]), ps_lo,
                                                          prev[:, n_rkv:], prep_consts, n_s)
    hdk = lambda t: fromk(t).reshape(n_s, n_heads, 1, HEAD_DIM)
    ys, wkv_s = _rwkv_step(state_wkv[l], hdk(rs), hdk(ws), hdk(ks), hdk(kks), hdk(bs),
                           fromv(vs).reshape(n_s, n_heads, 1, HEAD_DIM), 8)
    ys = tov(ys.reshape(n_s, d_rwkv)).reshape(n_s, sub, V7X_LANES)
    o_as = _rwkv_post(ys, gs, bonus_s, *post_consts, n_s)
    o_bs, conv_s = _conv_step(ps_cv.reshape(n_s, 1, 2 * d_conv), state_conv[l], *conv_consts, 8)
    o_bs = o_bs.reshape(n_s, d_conv).astype(BF16)
    attend_s = lambda q: _attn_step(q.astype(F32).reshape(n_s, 1, N_MEM_HEADS, dh), cache_mem_k[l], cache_mem_v[l],
                                    2).reshape(n_s, d).astype(BF16)
    y_sample = trunk_back(xs1, o_as, o_bs, attend_s, n_s).reshape(n_s, 1, d)
    shift_s = jnp.concatenate([rkv_from_kernel_order(ps_rkv), ps_lo], axis=-1)

    return (y_prompt, y_sample,
            shift_p[None], conv_p[None], wkv_p[None],
            mk_p.reshape(1, batch, n_mem, N_MEM_HEADS, dh), mv_p.reshape(1, batch, n_mem, N_MEM_HEADS, dh),
            shift_s[None], conv_s[None], wkv_s[None])
```

```python
import functools
import math

import jax
import jax.numpy as jnp
import numpy as np
from jax import lax
from jax.experimental import pallas as pl
from jax.experimental.pallas import tpu as pltpu

F32 = jnp.float32
BF16 = jnp.bfloat16

V7X_LANES = 128
V7X_SUBLANES = 8
V7X_VMEM_LIMIT_BYTES = 56 * 1024 * 1024

HEAD_DIM = 64
CONV_WIDTH = 31
DECAY_LORA = 64
ICLR_LORA = 64
GATE_LORA = 160
N_MEM_HEADS = 4
LN_EPS = 1e-5
GN_EPS = 64e-5


def _params(*sem):
    return pltpu.CompilerParams(dimension_semantics=sem, vmem_limit_bytes=V7X_VMEM_LIMIT_BYTES)


def _resident(block_shape, index_map):
    return pl.BlockSpec(block_shape, index_map, pipeline_mode=pl.Buffered(1))


def _ln_rows(y, g, b, eps):
    mu = jnp.mean(y, axis=-1, keepdims=True)
    yc = y - mu
    var = jnp.mean(yc * yc, axis=-1, keepdims=True)
    return yc * lax.rsqrt(var + eps) * g + b


def _dot(a, b):
    return jnp.dot(a, b, preferred_element_type=F32)


def _split_dot(x, g):
    hi = x.astype(BF16)
    mid = (x - hi.astype(F32)).astype(BF16)
    return _dot(hi, g) + _dot(mid, g)


def _head_sum(x, gsel, gselt):
    return _split_dot(_split_dot(x, gsel), gselt)


def _ffn_ln_kernel(x_ref, *refs, alpha, n_chunk, row_chunk, cast_x):
    if cast_x:
        w1_ref, w3_ref, w2_ref, g_ref, b_ref, o_ref, ob_ref, xb_ref = refs
    else:
        xb_ref, w1_ref, w3_ref, w2_ref, g_ref, b_ref, o_ref, ob_ref = refs
    j = pl.program_id(1)
    tm, d = o_ref.shape

    @pl.when(j == 0)
    def _():
        o_ref[...] = jnp.zeros_like(o_ref)
        if cast_x:
            xb_ref[...] = x_ref[...].astype(BF16)

    xb = xb_ref[...]
    h1 = _dot(xb, w1_ref[...].astype(BF16))
    h3 = _dot(xb, w3_ref[...].astype(BF16))
    hb = (h1 * jax.nn.sigmoid(h1) * h3).astype(BF16)
    w2b = w2_ref[...].astype(BF16)
    cw = d // n_chunk
    for c in range(n_chunk):
        o_ref[:, c * cw:(c + 1) * cw] += _dot(hb, w2b[:, c * cw:(c + 1) * cw])

    @pl.when(j == pl.num_programs(1) - 1)
    def _():
        def body(r, carry):
            rows = pl.ds(pl.multiple_of(r * row_chunk, row_chunk), row_chunk)
            y = alpha * x_ref[rows, :] + 0.5 * o_ref[rows, :]
            y = _ln_rows(y, g_ref[...], b_ref[...], LN_EPS)
            o_ref[rows, :] = y
            ob_ref[rows, :] = y.astype(BF16)
            return carry

        lax.fori_loop(0, tm // row_chunk, body, 0)


def _ffn_ln(x, xb, w1, w3, w2, g, b, alpha, tm, tf):
    m, d = x.shape
    f = w1.shape[1]
    cast_x = xb is None
    kern = functools.partial(_ffn_ln_kernel, alpha=alpha, n_chunk=4, row_chunk=min(tm, 128), cast_x=cast_x)
    row_tile = lambda i, j: (i, 0)
    return pl.pallas_call(
        kern,
        out_shape=[jax.ShapeDtypeStruct((m, d), F32), jax.ShapeDtypeStruct((m, d), BF16)],
        grid=(m // tm, f // tf),
        in_specs=[_resident((tm, d), row_tile)] * (1 if cast_x else 2) + [
            pl.BlockSpec((d, tf), lambda i, j: (0, j)),
            pl.BlockSpec((d, tf), lambda i, j: (0, j)),
            pl.BlockSpec((tf, d), lambda i, j: (j, 0)),
            pl.BlockSpec((1, d), lambda i, j: (0, 0)),
            pl.BlockSpec((1, d), lambda i, j: (0, 0)),
        ],
        out_specs=[_resident((tm, d), row_tile), _resident((tm, d), row_tile)],
        scratch_shapes=[pltpu.VMEM((tm, d), BF16)] if cast_x else [],
        compiler_params=_params("parallel", "arbitrary"),
        name="ffn_ln",
    )(*((x,) if cast_x else (x, xb)), w1, w3, w2, g, b)


def _mm_kernel(x_ref, w_ref, o_ref):
    o_ref[...] = _dot(x_ref[...].astype(BF16), w_ref[...].astype(BF16)).astype(o_ref.dtype)


def _mm(x, w, tm, tn, out_dtype=F32):
    m, k = x.shape
    n = w.shape[1]
    return pl.pallas_call(
        _mm_kernel,
        out_shape=jax.ShapeDtypeStruct((m, n), out_dtype),
        grid=(m // tm, n // tn),
        in_specs=[
            pl.BlockSpec((tm, k), lambda i, j: (i, 0)),
            pl.BlockSpec((k, tn), lambda i, j: (0, j)),
        ],
        out_specs=pl.BlockSpec((tm, tn), lambda i, j: (i, j)),
        compiler_params=_params("parallel", "arbitrary"),
        name="mm",
    )(x, w)


def _cast_kernel(x_ref, o_ref):
    o_ref[...] = x_ref[...].astype(o_ref.dtype)


def _cast_bf16(w, tr):
    rows, cols = w.shape
    return pl.pallas_call(
        _cast_kernel,
        out_shape=jax.ShapeDtypeStruct(w.shape, BF16),
        grid=(rows // tr,),
        in_specs=[pl.BlockSpec((tr, cols), lambda i: (i, 0))],
        out_specs=pl.BlockSpec((tr, cols), lambda i: (i, 0)),
        compiler_params=_params("parallel"),
        name="cast_bf16",
    )(w)


def _mm_res_ln_kernel(*refs, n_a, alpha, row_chunk):
    a_refs = refs[:n_a]
    w_refs = refs[n_a:2 * n_a]
    res_ref, g_ref, b_ref, o_ref, ob_ref, acc_ref = refs[2 * n_a:]
    tm, d = acc_ref.shape

    acc = _dot(a_refs[0][...], w_refs[0][...])
    for a_ref, w_ref in zip(a_refs[1:], w_refs[1:]):
        acc += _dot(a_ref[...], w_ref[...])
    acc_ref[...] = acc

    def body(r, carry):
        rows = pl.ds(pl.multiple_of(r * row_chunk, row_chunk), row_chunk)
        y = _ln_rows(alpha * res_ref[rows, :] + acc_ref[rows, :], g_ref[...], b_ref[...], LN_EPS)
        o_ref[rows, :] = y
        ob_ref[rows, :] = y.astype(BF16)
        return carry

    lax.fori_loop(0, tm // row_chunk, body, 0)


def _mm_res_ln(a_list, w_list, res, g, b, alpha, tm):
    m, d = res.shape
    n_a = len(a_list)
    kern = functools.partial(_mm_res_ln_kernel, n_a=n_a, alpha=alpha, row_chunk=min(tm, 64))
    row_tile = lambda i: (i, 0)
    in_specs = [pl.BlockSpec((tm, a.shape[1]), row_tile) for a in a_list]
    in_specs += [_resident((a.shape[1], d), functools.partial(lambda i, r: (r, 0), r=r))
                 for a, (_, r) in zip(a_list, w_list)]
    in_specs += [
        pl.BlockSpec((tm, d), row_tile),
        pl.BlockSpec((1, d), lambda i: (0, 0)),
        pl.BlockSpec((1, d), lambda i: (0, 0)),
    ]
    return pl.pallas_call(
        kern,
        out_shape=[jax.ShapeDtypeStruct((m, d), F32), jax.ShapeDtypeStruct((m, d), BF16)],
        grid=(m // tm,),
        in_specs=in_specs,
        out_specs=[pl.BlockSpec((tm, d), row_tile), pl.BlockSpec((tm, d), row_tile)],
        scratch_shapes=[pltpu.VMEM((tm, d), F32)],
        compiler_params=_params("parallel"),
        name="mm_res_ln",
    )(*a_list, *[w for w, _ in w_list], res, g, b)


def _shift_rows(cur, first_row):
    rolled = pltpu.roll(cur, 1, 0)
    row_id = lax.broadcasted_iota(jnp.int32, cur.shape, 0)
    return jnp.where(row_id == 0, first_row, rolled)


def _rwkv_prep_math(cur_rkv, prev_rkv, cur_lo, prev_lo, mu_rkv, mu_lo, w0, a0, k_k, k_a, r_k,
                    w2d, a2, g2, gsel, gselt, d_rwkv):
    pm = cur_rkv + mu_rkv * (prev_rkv - cur_rkv)
    pl_ = cur_lo + mu_lo * (prev_lo - cur_lo)
    r = pm[:, :d_rwkv]
    k = pm[:, d_rwkv:2 * d_rwkv]
    v = pm[:, 2 * d_rwkv:]
    wd = pl_[:, :DECAY_LORA]
    ad = pl_[:, DECAY_LORA:DECAY_LORA + ICLR_LORA]
    gd = pl_[:, DECAY_LORA + ICLR_LORA:]
    z = w0 + _dot(jnp.tanh(wd).astype(BF16), w2d.astype(BF16))
    w_log = -jax.nn.softplus(-z) - 0.5
    decay = jnp.exp(-jnp.exp(w_log))
    a = jax.nn.sigmoid(a0 + _dot(ad.astype(BF16), a2.astype(BF16)))
    g = _dot(jax.nn.sigmoid(gd).astype(BF16), g2.astype(BF16))
    kk = k * k_k
    norm = jnp.sqrt(_head_sum(kk * kk, gsel, gselt))
    kk = kk / jnp.maximum(norm, 1e-12)
    kmod = k * (1.0 + (a - 1.0) * k_a)
    bonus = _head_sum(r * kmod * r_k, gsel, gselt) * v
    return r, decay, kmod, v, kk, kk * a, g, bonus


def _rwkv_prep_seq_kernel(cur_rkv_ref, prev8_rkv_ref, cur_lo_ref, prev8_lo_ref, sp_rkv_ref, sp_lo_ref,
                          mu_rkv_ref, mu_lo_ref, w0_ref, a0_ref, kk_ref, ka_ref, rk_ref,
                          w2d_ref, a2_ref, g2_ref, gsel_ref, gselt_ref,
                          r_o, w_o, k_o, kk_o, b_o, v_o, g_o, bonus_o):
    first = pl.program_id(1) == 0
    cur_rkv = cur_rkv_ref[...]
    cur_lo = cur_lo_ref[...]
    last = V7X_SUBLANES - 1
    lanes = V7X_LANES
    first_rkv = jnp.where(first, sp_rkv_ref[...], prev8_rkv_ref[last:last + 1, :])
    first_lo = jnp.where(first, sp_lo_ref[...], prev8_lo_ref[last:last + 1, :])
    r, w, k, v, kk, b, g, bonus = _rwkv_prep_math(
        cur_rkv, _shift_rows(cur_rkv, first_rkv), cur_lo, _shift_rows(cur_lo, first_lo),
        mu_rkv_ref[...], mu_lo_ref[...], w0_ref[...], a0_ref[...], kk_ref[...], ka_ref[...], rk_ref[...],
        w2d_ref[...], a2_ref[...], g2_ref[...], gsel_ref[...], gselt_ref[...], g_o.shape[-1])
    g_o[...] = g
    bonus_o[...] = bonus
    for c in range(r_o.shape[1]):
        cols = slice(c * lanes, (c + 1) * lanes)
        for o_ref, val in zip((r_o, w_o, k_o, kk_o, b_o, v_o), (r, w, k, kk, b, v)):
            o_ref[:, c, :] = val[:, cols]


def _rwkv_prep_step_kernel(cur_rkv_ref, prev_rkv_ref, cur_lo_ref, prev_lo_ref,
                           mu_rkv_ref, mu_lo_ref, w0_ref, a0_ref, kk_ref, ka_ref, rk_ref,
                           w2d_ref, a2_ref, g2_ref, gsel_ref, gselt_ref,
                           r_o, w_o, k_o, v_o, kk_o, b_o, g_o, bonus_o):
    outs = _rwkv_prep_math(
        cur_rkv_ref[...], prev_rkv_ref[...], cur_lo_ref[...], prev_lo_ref[...],
        mu_rkv_ref[...], mu_lo_ref[...], w0_ref[...], a0_ref[...], kk_ref[...], ka_ref[...], rk_ref[...],
        w2d_ref[...], a2_ref[...], g2_ref[...], gsel_ref[...], gselt_ref[...], r_o.shape[-1])
    for o_ref, val in zip((r_o, w_o, k_o, v_o, kk_o, b_o, g_o, bonus_o), outs):
        o_ref[...] = val


def _const_spec(arr):
    nd = arr.ndim
    return pl.BlockSpec(arr.shape, lambda *_: (0,) * nd)


def _rwkv_prep_seq(p_rkv, p_lo, sp_rkv, sp_lo, consts, batch, seq, tt):
    m, n_rkv = p_rkv.shape
    n_lo = p_lo.shape[1]
    d_rwkv = n_rkv // 3
    nt = seq // tt
    sub = V7X_SUBLANES

    def cur_map(b, i):
        return (b * nt + i, 0)

    def prev_map(b, i):
        return (jnp.maximum((b * seq + i * tt) // sub - 1, 0), 0)

    lanes = V7X_LANES
    n_rows = d_rwkv // lanes
    cur_map3 = lambda b, i: (b * nt + i, 0, 0)
    out_shape = [jax.ShapeDtypeStruct((m, n_rows, lanes), F32)] * 6 + [jax.ShapeDtypeStruct((m, d_rwkv), F32)] * 2
    out_specs = [pl.BlockSpec((tt, n_rows, lanes), cur_map3)] * 6 + [pl.BlockSpec((tt, d_rwkv), cur_map)] * 2
    return pl.pallas_call(
        _rwkv_prep_seq_kernel,
        out_shape=out_shape,
        grid=(batch, nt),
        in_specs=[
            pl.BlockSpec((tt, n_rkv), cur_map),
            pl.BlockSpec((sub, n_rkv), prev_map),
            pl.BlockSpec((tt, n_lo), cur_map),
            pl.BlockSpec((sub, n_lo), prev_map),
            pl.BlockSpec((None, 1, n_rkv), lambda b, i: (b, 0, 0)),
            pl.BlockSpec((None, 1, n_lo), lambda b, i: (b, 0, 0)),
        ] + [_const_spec(c) for c in consts],
        out_specs=out_specs,
        compiler_params=_params("parallel", "arbitrary"),
        name="rwkv_prep_seq",
    )(p_rkv, p_rkv, p_lo, p_lo, sp_rkv, sp_lo, *consts)


def _rwkv_prep_step(p_rkv, prev_rkv, p_lo, prev_lo, consts, tb):
    m, n_rkv = p_rkv.shape
    n_lo = p_lo.shape[1]
    d_rwkv = n_rkv // 3
    out_sds = jax.ShapeDtypeStruct((m, d_rwkv), F32)
    row = lambda i: (i, 0)
    return pl.pallas_call(
        _rwkv_prep_step_kernel,
        out_shape=[out_sds] * 8,
        grid=(m // tb,),
        in_specs=[
            pl.BlockSpec((tb, n_rkv), row), pl.BlockSpec((tb, n_rkv), row),
            pl.BlockSpec((tb, n_lo), row), pl.BlockSpec((tb, n_lo), row),
        ] + [_const_spec(c) for c in consts],
        out_specs=[pl.BlockSpec((tb, d_rwkv), row)] * 8,
        compiler_params=_params("parallel"),
        name="rwkv_prep_step",
    )(p_rkv, prev_rkv, p_lo, prev_lo, *consts)


def _split_dot_stacked(x, g):
    hi = x.astype(BF16)
    mid = (x - hi.astype(F32)).astype(BF16)
    both = _dot(jnp.concatenate([hi, mid], axis=0), g)
    return both[:x.shape[0]] + both[x.shape[0]:]


def _rwkv_seq_kernel(r_ref, w_ref, k_ref, kk_ref, b_ref, v_ref, gsum_ref, vrep_ref, y_ref, s_ref,
                     *, group, n_chain):
    nb, tb, n_kq, lanes = r_ref.shape
    sub = V7X_SUBLANES
    n_v = s_ref.shape[1] // nb
    n_vq = n_v // sub
    nbc = nb // n_chain

    @pl.when(pl.program_id(0) == 0)
    def _():
        s_ref[...] = jnp.zeros_like(s_ref)

    lane_grp = lax.broadcasted_iota(jnp.int32, (sub, lanes), 1) // group
    gsum = gsum_ref[...]
    vrep = vrep_ref[...]

    def update_state(t, c):
        seqs = range(c * nbc, (c + 1) * nbc)
        srows = slice(c * nbc * n_v, (c + 1) * nbc * n_v)

        def rows(ref, kq):
            return jnp.concatenate(
                [jnp.broadcast_to(ref[b, t, pl.ds(kq, 1), :], (n_v, lanes)) for b in seqs], axis=0)

        vx = _split_dot_stacked(jnp.concatenate([v_ref[b, t] for b in seqs], axis=0), vrep)
        v_all = jnp.concatenate(
            [vx[i * sub:(i + 1) * sub, vq * lanes:(vq + 1) * lanes] for i in range(nbc) for vq in range(n_vq)],
            axis=0)
        sa = s_ref[0, srows, :] * rows(kk_ref, 0)
        for kq in range(1, n_kq):
            sa += s_ref[kq, srows, :] * rows(kk_ref, kq)
        sa = _split_dot_stacked(sa, gsum)
        y = None
        for kq in range(n_kq):
            s_new = s_ref[kq, srows, :] * rows(w_ref, kq) - rows(b_ref, kq) * sa + rows(k_ref, kq) * v_all
            s_ref[kq, srows, :] = s_new
            term = s_new * rows(r_ref, kq)
            y = term if y is None else y + term
        return y

    def emit_y(t, c, y):
        y = _split_dot_stacked(y, gsum)
        for i in range(nbc):
            yc = jnp.zeros((sub, lanes), F32)
            for vq in range(n_vq):
                lo = (i * n_vq + vq) * sub
                yc = jnp.where(lane_grp == vq, y[lo:lo + sub, :], yc)
            y_ref[c * nbc + i, t] = yc

    def step(t, y_prev):
        y_cur = tuple(update_state(t, c) for c in range(n_chain))
        for c in range(n_chain):
            emit_y(t - 1, c, y_prev[c])
        return y_cur

    y_last = lax.fori_loop(1, tb, step, tuple(update_state(0, c) for c in range(n_chain)))
    for c in range(n_chain):
        emit_y(tb - 1, c, y_last[c])


def _rwkv_seq(r, w, k, kk, b, v, gsum, vrep, n_heads, tb):
    batch, seq, n_kq, lanes = r.shape
    sub = V7X_SUBLANES
    d_rwkv = n_kq * lanes
    assert lanes == V7X_LANES == sub * n_heads and d_rwkv == HEAD_DIM * n_heads and HEAD_DIM == sub * sub
    n_chain = 2 if batch % 2 == 0 else 1
    spec = pl.BlockSpec((batch, tb, n_kq, lanes), lambda i: (0, i, 0, 0))
    state_shape = (n_kq, batch * HEAD_DIM, lanes)
    return pl.pallas_call(
        functools.partial(_rwkv_seq_kernel, group=n_heads, n_chain=n_chain),
        out_shape=[jax.ShapeDtypeStruct((batch, seq, sub, lanes), F32),
                   jax.ShapeDtypeStruct(state_shape, F32)],
        grid=(seq // tb,),
        in_specs=[spec] * 6 + [_const_spec(gsum), _const_spec(vrep)],
        out_specs=[spec, pl.BlockSpec(state_shape, lambda i: (0, 0, 0))],
        compiler_params=_params("arbitrary"),
        name="rwkv_seq",
    )(r, w, k, kk, b, v, gsum, vrep)


def _rwkv_step_kernel(s_ref, r_ref, w_ref, k_ref, kk_ref, b_ref, v_ref, y_ref, so_ref):
    n = s_ref.shape[-1]
    eye = (lax.broadcasted_iota(jnp.int32, (n, n), 0) == lax.broadcasted_iota(jnp.int32, (n, n), 1)).astype(F32)
    s = s_ref[...]
    v_col = jnp.sum(eye * v_ref[...], axis=-1, keepdims=True)
    sa = jnp.sum(s * kk_ref[...], axis=-1, keepdims=True)
    s_new = s * w_ref[...] - sa * b_ref[...] + v_col * k_ref[...]
    so_ref[...] = s_new
    y_col = jnp.sum(s_new * r_ref[...], axis=-1, keepdims=True)
    y_ref[...] = jnp.sum(eye * y_col, axis=-2, keepdims=True)


def _rwkv_step(state, r, w, k, kk, b, v, sb):
    n_s, n_h, n, _ = state.shape
    sspec = pl.BlockSpec((sb, n_h, n, n), lambda i: (i, 0, 0, 0))
    vspec = pl.BlockSpec((sb, n_h, 1, n), lambda i: (i, 0, 0, 0))
    return pl.pallas_call(
        _rwkv_step_kernel,
        out_shape=[jax.ShapeDtypeStruct((n_s, n_h, 1, n), F32), jax.ShapeDtypeStruct(state.shape, F32)],
        grid=(n_s // sb,),
        in_specs=[sspec] + [vspec] * 6,
        out_specs=[vspec, sspec],
        compiler_params=_params("parallel"),
        name="rwkv_step",
    )(state, r, w, k, kk, b, v)


def _rwkv_post_kernel(y_ref, g_ref, bonus_ref, gng_ref, gnb_ref, beta_ref, gsel_ref, gselt_ref, o_ref):
    y = jnp.concatenate([y_ref[:, c, :] for c in range(y_ref.shape[1])], axis=-1)
    gsel = gsel_ref[...]
    gselt = gselt_ref[...]
    mu = _head_sum(y, gsel, gselt) * (1.0 / HEAD_DIM)
    yc = y - mu
    var = _head_sum(yc * yc, gsel, gselt) * (1.0 / HEAD_DIM)
    yn = yc * lax.rsqrt(var + GN_EPS) * gng_ref[...] + gnb_ref[...]
    o_ref[...] = ((yn + bonus_ref[...]) * g_ref[...] * beta_ref[...]).astype(o_ref.dtype)


def _rwkv_post(y, g, bonus, gn_g, gn_b, beta, gsel, gselt, tt):
    m, n_rows, lanes = y.shape
    d = n_rows * lanes
    row = pl.BlockSpec((tt, d), lambda i: (i, 0))
    consts = (gn_g, gn_b, beta, gsel, gselt)
    return pl.pallas_call(
        _rwkv_post_kernel,
        out_shape=jax.ShapeDtypeStruct((m, d), BF16),
        grid=(m // tt,),
        in_specs=[pl.BlockSpec((tt, n_rows, lanes), lambda i: (i, 0, 0)), row, row] + [_const_spec(c) for c in consts],
        out_specs=row,
        compiler_params=_params("parallel"),
        name="rwkv_post",
    )(y, g, bonus, *consts)


CONV_HALO = 32


def _glu(pc, d_conv):
    return pc[:, :d_conv] * jax.nn.sigmoid(pc[:, d_conv:])


def _conv_seq_kernel(pc_ref, halo_ref, cw_ref, cb_ref, lng_ref, lnb_ref, beta_ref, o_ref, st_ref, ubuf_ref,
                     ushift_ref, *, row_chunk):
    tt, d_conv = o_ref.shape
    i = pl.program_id(1)
    sub = V7X_SUBLANES
    n_hist = CONV_WIDTH - 1
    pad = CONV_HALO - n_hist
    u_halo = _glu(halo_ref[...], d_conv)
    ubuf_ref[0:CONV_HALO, :] = jnp.where(i == 0, jnp.zeros_like(u_halo), u_halo)
    ubuf_ref[CONV_HALO:, :] = _glu(pc_ref[...], d_conv)
    n_shift_rows = ushift_ref.shape[1]
    for s in range(1, sub):
        ushift_ref[s - 1] = ubuf_ref[s:s + n_shift_rows, :]

    def tap(base, w):
        off = pad + w
        q, s = off // sub, off % sub
        rows = slice(base + q * sub, base + q * sub + row_chunk)
        return (ubuf_ref[rows, :] if s == 0 else ushift_ref[s - 1, rows, :]) * cw_ref[w:w + 1, :]

    for base in range(0, tt, row_chunk):
        acc = tap(base, 0)
        for w in range(1, CONV_WIDTH):
            acc += tap(base, w)
        c = _ln_rows(acc + cb_ref[...], lng_ref[...], lnb_ref[...], LN_EPS)
        o_ref[base:base + row_chunk, :] = (c * jax.nn.sigmoid(c) * beta_ref[...]).astype(o_ref.dtype)

    @pl.when(i == pl.num_programs(1) - 1)
    def _():
        st_ref[...] = ubuf_ref[CONV_HALO + tt - n_hist:, :]


def _conv_seq(pc, cw, cb, ln_g, ln_b, beta, batch, seq, tt):
    m, two_d = pc.shape
    d_conv = two_d // 2
    nt = seq // tt
    hb = tt // CONV_HALO
    consts = (cw, cb, ln_g, ln_b, beta)
    kern = functools.partial(_conv_seq_kernel, row_chunk=16)
    return pl.pallas_call(
        kern,
        out_shape=[jax.ShapeDtypeStruct((m, d_conv), BF16),
                   jax.ShapeDtypeStruct((batch, CONV_WIDTH - 1, d_conv), F32)],
        grid=(batch, nt),
        in_specs=[
            pl.BlockSpec((tt, two_d), lambda b, i: (b * nt + i, 0)),
            pl.BlockSpec((CONV_HALO, two_d), lambda b, i: (jnp.maximum((b * nt + i) * hb - 1, 0), 0)),
        ] + [_const_spec(c) for c in consts],
        out_specs=[pl.BlockSpec((tt, d_conv), lambda b, i: (b * nt + i, 0)),
                   pl.BlockSpec((None, CONV_WIDTH - 1, d_conv), lambda b, i: (b, 0, 0))],
        scratch_shapes=[pltpu.VMEM((CONV_HALO + tt, d_conv), F32),
                        pltpu.VMEM((V7X_SUBLANES - 1, CONV_HALO + tt - V7X_SUBLANES, d_conv), F32)],
        compiler_params=_params("parallel", "arbitrary"),
        name="conv_seq",
    )(pc, pc, *consts)


def _conv_step_kernel(pc_ref, st_ref, cw_ref, cb_ref, lng_ref, lnb_ref, beta_ref, o_ref, so_ref):
    d_conv = o_ref.shape[-1]
    n_hist = CONV_WIDTH - 1
    pc = pc_ref[...]
    u = pc[:, :, :d_conv] * jax.nn.sigmoid(pc[:, :, d_conv:])
    st = st_ref[...]
    c = (jnp.sum(st * cw_ref[0:n_hist, :], axis=1, keepdims=True)
         + u * cw_ref[n_hist:CONV_WIDTH, :] + cb_ref[...])
    c = _ln_rows(c, lng_ref[...], lnb_ref[...], LN_EPS)
    o_ref[...] = c * jax.nn.sigmoid(c) * beta_ref[...]
    so_ref[:, 0:n_hist - 1, :] = st[:, 1:, :]
    so_ref[:, n_hist - 1:n_hist, :] = u


def _conv_step(pc, state, cw, cb, ln_g, ln_b, beta, sb):
    n_s, _, two_d = pc.shape
    d_conv = two_d // 2
    consts = (cw, cb, ln_g, ln_b, beta)
    sspec = pl.BlockSpec((sb, CONV_WIDTH - 1, d_conv), lambda i: (i, 0, 0))
    return pl.pallas_call(
        _conv_step_kernel,
        out_shape=[jax.ShapeDtypeStruct((n_s, 1, d_conv), F32), jax.ShapeDtypeStruct(state.shape, F32)],
        grid=(n_s // sb,),
        in_specs=[pl.BlockSpec((sb, 1, two_d), lambda i: (i, 0, 0)), sspec] + [_const_spec(c) for c in consts],
        out_specs=[pl.BlockSpec((sb, 1, d_conv), lambda i: (i, 0, 0)), sspec],
        compiler_params=_params("parallel"),
        name="conv_step",
    )(pc, state, *consts)


def _attn_seq_kernel(q_ref, k_ref, v_ref, o_ref, *, scale):
    d = q_ref.shape[-1]
    dh = d // N_MEM_HEADS
    for h in range(N_MEM_HEADS):
        cols = slice(h * dh, (h + 1) * dh)
        k = k_ref[:, cols].astype(BF16)
        s = lax.dot_general(q_ref[:, cols], k, (((1,), (1,)), ((), ())), preferred_element_type=F32) * scale
        s = s - jnp.max(s, axis=-1, keepdims=True)
        e = jnp.exp(s)
        p = e / jnp.sum(e, axis=-1, keepdims=True)
        o_ref[:, cols] = _dot(p.astype(BF16), v_ref[:, cols].astype(BF16)).astype(o_ref.dtype)


def _attn_seq(q, mk, mv, batch, seq, tq):
    m, d = q.shape
    n_mem = mk.shape[1]
    nt = seq // tq
    kern = functools.partial(_attn_seq_kernel, scale=1.0 / math.sqrt(d // N_MEM_HEADS))
    kv_spec = pl.BlockSpec((None, n_mem, d), lambda b, i: (b, 0, 0))
    return pl.pallas_call(
        kern,
        out_shape=jax.ShapeDtypeStruct((m, d), BF16),
        grid=(batch, nt),
        in_specs=[pl.BlockSpec((tq, d), lambda b, i: (b * nt + i, 0)), kv_spec, kv_spec],
        out_specs=pl.BlockSpec((tq, d), lambda b, i: (b * nt + i, 0)),
        compiler_params=_params("parallel", "arbitrary"),
        name="attn_seq",
    )(q, mk, mv)


def _attn_step_kernel(q_ref, k_ref, v_ref, o_ref, *, scale):
    for i in range(q_ref.shape[0]):
        s = jnp.sum(k_ref[i] * q_ref[i], axis=-1, keepdims=True) * scale
        s = s - jnp.max(s, axis=0, keepdims=True)
        e = jnp.exp(s)
        p = e / jnp.sum(e, axis=0, keepdims=True)
        o_ref[i] = jnp.sum(p * v_ref[i], axis=0, keepdims=True)


def _attn_step(q, mk, mv, sb):
    n_s, _, n_h, dh = q.shape
    n_mem = mk.shape[1]
    kern = functools.partial(_attn_step_kernel, scale=1.0 / math.sqrt(dh))
    qspec = pl.BlockSpec((sb, 1, n_h, dh), lambda i: (i, 0, 0, 0))
    kvspec = pl.BlockSpec((sb, n_mem, n_h, dh), lambda i: (i, 0, 0, 0))
    return pl.pallas_call(
        kern,
        out_shape=jax.ShapeDtypeStruct(q.shape, F32),
        grid=(n_s // sb,),
        in_specs=[qspec, kvspec, kvspec],
        out_specs=qspec,
        compiler_params=_params("parallel"),
        name="attn_step",
    )(q, mk, mv)


def _head_selectors(d_rwkv, n_heads):
    assert n_heads <= V7X_LANES
    lane = np.arange(V7X_LANES)
    chan = np.arange(d_rwkv)
    gsel = chan[:, None] % n_heads == lane[None, :]
    gsum = lane[:, None] % n_heads == lane[None, :] % n_heads
    vrep = ((lane[:, None] // n_heads == chan[None, :] // V7X_LANES)
            & (lane[:, None] % n_heads == chan[None, :] % n_heads))
    as_bf16 = lambda a: jnp.asarray(a.astype(np.float32), BF16)
    return as_bf16(gsel), as_bf16(gsel.T), as_bf16(gsum), as_bf16(vrep)


def _to_key_order(x, n_heads):
    lead = x.shape[:-1]
    return jnp.swapaxes(x.reshape(*lead, n_heads, HEAD_DIM), -1, -2).reshape(*lead, n_heads * HEAD_DIM)


def _from_key_order(x, n_heads):
    lead = x.shape[:-1]
    return jnp.swapaxes(x.reshape(*lead, HEAD_DIM, n_heads), -1, -2).reshape(*lead, n_heads * HEAD_DIM)


def _to_value_order(x, n_heads):
    lead = x.shape[:-1]
    nd = len(lead)
    sub = V7X_SUBLANES
    x = x.reshape(*lead, n_heads, HEAD_DIM // sub, sub)
    return x.transpose(*range(nd), nd + 2, nd + 1, nd).reshape(*lead, n_heads * HEAD_DIM)


def _from_value_order(x, n_heads):
    lead = x.shape[:-1]
    nd = len(lead)
    sub = V7X_SUBLANES
    x = x.reshape(*lead, sub, HEAD_DIM // sub, n_heads)
    return x.transpose(*range(nd), nd + 2, nd + 1, nd).reshape(*lead, n_heads * HEAD_DIM)


def kernel(x_prompt, x_sample, mem_prompt, state_shift, state_conv, state_wkv, cache_mem_k, cache_mem_v,
           ffn1_w1, ffn1_w3, ffn1_w2, ln1_g, ln1_b, w_in, mu_shift, w0, w2_decay, a0, a2_iclr, g2_gate,
           k_k, k_a, r_k, gn_g, gn_b, conv_w, conv_b, conv_ln_g, conv_ln_b, beta_rwkv, beta_conv, w_out,
           ln2_g, ln2_b, w_mq, w_mk, w_mv, w_mo, ln3_g, ln3_b, ffn2_w1, ffn2_w3, ffn2_w2, ln4_g, ln4_b):
    depth = ffn1_w1.shape[0]
    assert depth == 1
    batch, seq, d = x_prompt.shape
    n_s = x_sample.shape[0]
    assert x_sample.shape[1] == 1
    n_mem = mem_prompt.shape[1]
    d_rwkv = w0.shape[-1]
    d_conv = conv_b.shape[-1]
    n_heads = d_rwkv // HEAD_DIM
    n_rkv = 3 * d_rwkv
    shift_cols = mu_shift.shape[-1]
    alpha = (2.0 * depth) ** 0.25
    l = 0

    tok = functools.partial(_to_key_order, n_heads=n_heads)
    tov = functools.partial(_to_value_order, n_heads=n_heads)
    fromk = functools.partial(_from_key_order, n_heads=n_heads)
    fromv = functools.partial(_from_value_order, n_heads=n_heads)

    def rkv_to_kernel_order(x):
        return jnp.concatenate([tok(x[..., :d_rwkv]), tok(x[..., d_rwkv:2 * d_rwkv]), tov(x[..., 2 * d_rwkv:n_rkv])],
                               axis=-1)

    def rkv_from_kernel_order(x):
        return jnp.concatenate([fromk(x[..., :d_rwkv]), fromk(x[..., d_rwkv:2 * d_rwkv]),
                                fromv(x[..., 2 * d_rwkv:n_rkv])], axis=-1)

    w_rkv = rkv_to_kernel_order(w_in[l, :, :n_rkv])
    w_lo = w_in[l, :, n_rkv:shift_cols]
    w_cv = w_in[l, :, shift_cols:]
    mu_rkv = rkv_to_kernel_order(mu_shift[l:l + 1, :n_rkv])
    mu_lo = mu_shift[l:l + 1, n_rkv:]
    gsel, gselt, gsum, vrep = _head_selectors(d_rwkv, n_heads)
    prep_consts = (mu_rkv, mu_lo, tok(w0[l:l + 1]), tok(a0[l:l + 1]), tok(k_k[l:l + 1]), tok(k_a[l:l + 1]),
                   tok(r_k[l].reshape(1, d_rwkv)), tok(w2_decay[l]), tok(a2_iclr[l]), tov(g2_gate[l]), gsel, gselt)
    cw = conv_w[l].reshape(CONV_WIDTH, d_conv)
    conv_consts = (cw, conv_b[l:l + 1], conv_ln_g[l:l + 1], conv_ln_b[l:l + 1], beta_conv[l:l + 1])
    post_consts = (tov(gn_g[l:l + 1]), tov(gn_b[l:l + 1]), tov(beta_rwkv[l:l + 1]), gsel, gselt)
    w_out_rows = jnp.concatenate([tov(w_out[l, :d_rwkv].T).T, w_out[l, d_rwkv:]], axis=0)
    w_out_b = _cast_bf16(w_out_rows, 512)
    w_mo_b = _cast_bf16(w_mo[l], 512)

    def trunk_front(x, tm):
        x1, x1b = _ffn_ln(x, None, ffn1_w1[l], ffn1_w3[l], ffn1_w2[l], ln1_g[l:l + 1], ln1_b[l:l + 1], alpha, tm, 512)
        p_rkv = _mm(x1b, w_rkv, tm, 512)
        p_lo = _mm(x1b, w_lo, tm, w_lo.shape[1])
        p_cv = _mm(x1b, w_cv, tm, 512)
        return x1, p_rkv, p_lo, p_cv

    def trunk_back(x1, o_a, o_b, attend, tm):
        tm_ln = min(tm, 512)
        x2, x2b = _mm_res_ln([o_a, o_b], [(w_out_b, 0), (w_out_b, 1)], x1, ln2_g[l:l + 1], ln2_b[l:l + 1], alpha, tm_ln)
        q = _mm(x2b, w_mq[l], tm, 512, BF16)
        att = attend(q)
        x3, x3b = _mm_res_ln([att], [(w_mo_b, 0)], x2, ln3_g[l:l + 1], ln3_b[l:l + 1], alpha, tm_ln)
        return _ffn_ln(x3, x3b, ffn2_w1[l], ffn2_w3[l], ffn2_w2[l], ln4_g[l:l + 1], ln4_b[l:l + 1], alpha, tm, 512)[0]

    m_p = batch * seq
    tm_p = 1024
    x1, p_rkv, p_lo, p_cv = trunk_front(x_prompt.reshape(m_p, d), tm_p)
    zeros_rkv = jnp.zeros((batch, 1, n_rkv), F32)
    zeros_lo = jnp.zeros((batch, 1, shift_cols - n_rkv), F32)
    sub = V7X_SUBLANES
    r, w, k, kk, bb, v, g, bonus = _rwkv_prep_seq(p_rkv, p_lo, zeros_rkv, zeros_lo, prep_consts, batch, seq, 256)
    bt = lambda t: t.reshape(batch, seq, *t.shape[1:])
    y4, s_t = _rwkv_seq(bt(r), bt(w), bt(k), bt(kk), bt(bb), bt(v), gsum, vrep, n_heads, 64)
    wkv_p = s_t.reshape(HEAD_DIM // sub, batch, HEAD_DIM, sub, n_heads).transpose(1, 4, 2, 0, 3).reshape(
        batch, n_heads, HEAD_DIM, HEAD_DIM)
    o_a = _rwkv_post(y4.reshape(m_p, sub, V7X_LANES), g, bonus, *post_consts, 256)
    o_b, conv_p = _conv_seq(p_cv, *conv_consts, batch, seq, 256)

    mem2 = mem_prompt.reshape(batch * n_mem, d)
    mk_p = _mm(mem2, w_mk[l], batch * n_mem, 512)
    mv_p = _mm(mem2, w_mv[l], batch * n_mem, 512)
    attend_p = lambda q: _attn_seq(q, mk_p.reshape(batch, n_mem, d), mv_p.reshape(batch, n_mem, d), batch, seq, 512)
    y_prompt = trunk_back(x1, o_a, o_b, attend_p, tm_p).reshape(batch, seq, d)
    shift_p = jnp.concatenate([rkv_from_kernel_order(p_rkv.reshape(batch, seq, n_rkv)[:, -1]),
                               p_lo.reshape(batch, seq, -1)[:, -1]], axis=-1)
    dh = d // N_MEM_HEADS

    xs1, ps_rkv, ps_lo, ps_cv = trunk_front(x_sample.reshape(n_s, d), n_s)
    prev = state_shift[l]
    rs, ws, ks, vs, kks, bs, gs, bonus_s = _rwkv_prep_step(ps_rkv, rkv_to_kernel_order(prev[:, :n_rkv]), ps_lo,
                                                          prev[:, n_rkv:], prep_consts, n_s)
    hdk = lambda t: fromk(t).reshape(n_s, n_heads, 1, HEAD_DIM)
    ys, wkv_s = _rwkv_step(state_wkv[l], hdk(rs), hdk(ws), hdk(ks), hdk(kks), hdk(bs),
                           fromv(vs).reshape(n_s, n_heads, 1, HEAD_DIM), 8)
    ys = tov(ys.reshape(n_s, d_rwkv)).reshape(n_s, sub, V7X_LANES)
    o_as = _rwkv_post(ys, gs, bonus_s, *post_consts, n_s)
    o_bs, conv_s = _conv_step(ps_cv.reshape(n_s, 1, 2 * d_conv), state_conv[l], *conv_consts, 8)
    o_bs = o_bs.reshape(n_s, d_conv).astype(BF16)
    attend_s = lambda q: _attn_step(q.astype(F32).reshape(n_s, 1, N_MEM_HEADS, dh), cache_mem_k[l], cache_mem_v[l],
                                    2).reshape(n_s, d).astype(BF16)
    y_sample = trunk_back(xs1, o_as, o_bs, attend_s, n_s).reshape(n_s, 1, d)
    shift_s = jnp.concatenate([rkv_from_kernel_order(ps_rkv), ps_lo], axis=-1)

    return (y_prompt, y_sample,
            shift_p[None], conv_p[None], wkv_p[None],
            mk_p.reshape(1, batch, n_mem, N_MEM_HEADS, dh), mv_p.reshape(1, batch, n_mem, N_MEM_HEADS, dh),
            shift_s[None], conv_s[None], wkv_s[None])
```

```python
import functools
import math

import jax
import jax.numpy as jnp
import numpy as np
from jax import lax
from jax.experimental import pallas as pl
from jax.experimental.pallas import tpu as pltpu

F32 = jnp.float32
BF16 = jnp.bfloat16

V7X_LANES = 128
V7X_SUBLANES = 8
V7X_VMEM_LIMIT_BYTES = 56 * 1024 * 1024

HEAD_DIM = 64
CONV_WIDTH = 31
DECAY_LORA = 64
ICLR_LORA = 64
GATE_LORA = 160
N_MEM_HEADS = 4
LN_EPS = 1e-5
GN_EPS = 64e-5


def _params(*sem):
    return pltpu.CompilerParams(dimension_semantics=sem, vmem_limit_bytes=V7X_VMEM_LIMIT_BYTES)


def _resident(block_shape, index_map):
    return pl.BlockSpec(block_shape, index_map, pipeline_mode=pl.Buffered(1))


def _ln_rows(y, g, b, eps):
    mu = jnp.mean(y, axis=-1, keepdims=True)
    yc = y - mu
    var = jnp.mean(yc * yc, axis=-1, keepdims=True)
    return yc * lax.rsqrt(var + eps) * g + b


def _dot(a, b):
    return jnp.dot(a, b, preferred_element_type=F32)


def _split_dot(x, g):
    hi = x.astype(BF16)
    mid = (x - hi.astype(F32)).astype(BF16)
    return _dot(hi, g) + _dot(mid, g)


def _head_sum(x, gsel, gselt):
    return _split_dot(_split_dot(x, gsel), gselt)


def _ffn_ln_kernel(x_ref, *refs, alpha, n_chunk, row_chunk, cast_x):
    if cast_x:
        w1_ref, w3_ref, w2_ref, g_ref, b_ref, o_ref, ob_ref, xb_ref = refs
    else:
        xb_ref, w1_ref, w3_ref, w2_ref, g_ref, b_ref, o_ref, ob_ref = refs
    j = pl.program_id(1)
    tm, d = o_ref.shape

    @pl.when(j == 0)
    def _():
        o_ref[...] = jnp.zeros_like(o_ref)
        if cast_x:
            xb_ref[...] = x_ref[...].astype(BF16)

    xb = xb_ref[...]
    h1 = _dot(xb, w1_ref[...].astype(BF16))
    h3 = _dot(xb, w3_ref[...].astype(BF16))
    hb = (h1 * jax.nn.sigmoid(h1) * h3).astype(BF16)
    w2b = w2_ref[...].astype(BF16)
    cw = d // n_chunk
    for c in range(n_chunk):
        o_ref[:, c * cw:(c + 1) * cw] += _dot(hb, w2b[:, c * cw:(c + 1) * cw])

    @pl.when(j == pl.num_programs(1) - 1)
    def _():
        def body(r, carry):
            rows = pl.ds(pl.multiple_of(r * row_chunk, row_chunk), row_chunk)
            y = alpha * x_ref[rows, :] + 0.5 * o_ref[rows, :]
            y = _ln_rows(y, g_ref[...], b_ref[...], LN_EPS)
            o_ref[rows, :] = y
            ob_ref[rows, :] = y.astype(BF16)
            return carry

        lax.fori_loop(0, tm // row_chunk, body, 0)


def _ffn_ln(x, xb, w1, w3, w2, g, b, alpha, tm, tf):
    m, d = x.shape
    f = w1.shape[1]
    cast_x = xb is None
    kern = functools.partial(_ffn_ln_kernel, alpha=alpha, n_chunk=4, row_chunk=min(tm, 128), cast_x=cast_x)
    row_tile = lambda i, j: (i, 0)
    return pl.pallas_call(
        kern,
        out_shape=[jax.ShapeDtypeStruct((m, d), F32), jax.ShapeDtypeStruct((m, d), BF16)],
        grid=(m // tm, f // tf),
        in_specs=[_resident((tm, d), row_tile)] * (1 if cast_x else 2) + [
            pl.BlockSpec((d, tf), lambda i, j: (0, j)),
            pl.BlockSpec((d, tf), lambda i, j: (0, j)),
            pl.BlockSpec((tf, d), lambda i, j: (j, 0)),
            pl.BlockSpec((1, d), lambda i, j: (0, 0)),
            pl.BlockSpec((1, d), lambda i, j: (0, 0)),
        ],
        out_specs=[_resident((tm, d), row_tile), _resident((tm, d), row_tile)],
        scratch_shapes=[pltpu.VMEM((tm, d), BF16)] if cast_x else [],
        compiler_params=_params("parallel", "arbitrary"),
        name="ffn_ln",
    )(*((x,) if cast_x else (x, xb)), w1, w3, w2, g, b)


def _mm_kernel(x_ref, w_ref, o_ref):
    o_ref[...] = _dot(x_ref[...].astype(BF16), w_ref[...].astype(BF16)).astype(o_ref.dtype)


def _mm(x, w, tm, tn, out_dtype=F32):
    m, k = x.shape
    n = w.shape[1]
    return pl.pallas_call(
        _mm_kernel,
        out_shape=jax.ShapeDtypeStruct((m, n), out_dtype),
        grid=(m // tm, n // tn),
        in_specs=[
            pl.BlockSpec((tm, k), lambda i, j: (i, 0)),
            pl.BlockSpec((k, tn), lambda i, j: (0, j)),
        ],
        out_specs=pl.BlockSpec((tm, tn), lambda i, j: (i, j)),
        compiler_params=_params("parallel", "arbitrary"),
        name="mm",
    )(x, w)


def _cast_kernel(x_ref, o_ref):
    o_ref[...] = x_ref[...].astype(o_ref.dtype)


def _cast_bf16(w, tr):
    rows, cols = w.shape
    return pl.pallas_call(
        _cast_kernel,
        out_shape=jax.ShapeDtypeStruct(w.shape, BF16),
        grid=(rows // tr,),
        in_specs=[pl.BlockSpec((tr, cols), lambda i: (i, 0))],
        out_specs=pl.BlockSpec((tr, cols), lambda i: (i, 0)),
        compiler_params=_params("parallel"),
        name="cast_bf16",
    )(w)


def _mm_res_ln_kernel(*refs, n_a, alpha, row_chunk):
    a_refs = refs[:n_a]
    w_refs = refs[n_a:2 * n_a]
    res_ref, g_ref, b_ref, o_ref, ob_ref, acc_ref = refs[2 * n_a:]
    tm, d = acc_ref.shape

    acc = _dot(a_refs[0][...], w_refs[0][...])
    for a_ref, w_ref in zip(a_refs[1:], w_refs[1:]):
        acc += _dot(a_ref[...], w_ref[...])
    acc_ref[...] = acc

    def body(r, carry):
        rows = pl.ds(pl.multiple_of(r * row_chunk, row_chunk), row_chunk)
        y = _ln_rows(alpha * res_ref[rows, :] + acc_ref[rows, :], g_ref[...], b_ref[...], LN_EPS)
        o_ref[rows, :] = y
        ob_ref[rows, :] = y.astype(BF16)
        return carry

    lax.fori_loop(0, tm // row_chunk, body, 0)


def _mm_res_ln(a_list, w_list, res, g, b, alpha, tm):
    m, d = res.shape
    n_a = len(a_list)
    kern = functools.partial(_mm_res_ln_kernel, n_a=n_a, alpha=alpha, row_chunk=min(tm, 64))
    row_tile = lambda i: (i, 0)
    in_specs = [pl.BlockSpec((tm, a.shape[1]), row_tile) for a in a_list]
    in_specs += [_resident((a.shape[1], d), functools.partial(lambda i, r: (r, 0), r=r))
                 for a, (_, r) in zip(a_list, w_list)]
    in_specs += [
        pl.BlockSpec((tm, d), row_tile),
        pl.BlockSpec((1, d), lambda i: (0, 0)),
        pl.BlockSpec((1, d), lambda i: (0, 0)),
    ]
    return pl.pallas_call(
        kern,
        out_shape=[jax.ShapeDtypeStruct((m, d), F32), jax.ShapeDtypeStruct((m, d), BF16)],
        grid=(m // tm,),
        in_specs=in_specs,
        out_specs=[pl.BlockSpec((tm, d), row_tile), pl.BlockSpec((tm, d), row_tile)],
        scratch_shapes=[pltpu.VMEM((tm, d), F32)],
        compiler_params=_params("parallel"),
        name="mm_res_ln",
    )(*a_list, *[w for w, _ in w_list], res, g, b)


def _shift_rows(cur, first_row):
    rolled = pltpu.roll(cur, 1, 0)
    row_id = lax.broadcasted_iota(jnp.int32, cur.shape, 0)
    return jnp.where(row_id == 0, first_row, rolled)


def _rwkv_prep_math(cur_rkv, prev_rkv, cur_lo, prev_lo, mu_rkv, mu_lo, w0, a0, k_k, k_a, r_k,
                    w2d, a2, g2, gsel, gselt, d_rwkv):
    pm = cur_rkv + mu_rkv * (prev_rkv - cur_rkv)
    pl_ = cur_lo + mu_lo * (prev_lo - cur_lo)
    r = pm[:, :d_rwkv]
    k = pm[:, d_rwkv:2 * d_rwkv]
    v = pm[:, 2 * d_rwkv:]
    wd = pl_[:, :DECAY_LORA]
    ad = pl_[:, DECAY_LORA:DECAY_LORA + ICLR_LORA]
    gd = pl_[:, DECAY_LORA + ICLR_LORA:]
    z = w0 + _dot(jnp.tanh(wd).astype(BF16), w2d.astype(BF16))
    w_log = -jax.nn.softplus(-z) - 0.5
    decay = jnp.exp(-jnp.exp(w_log))
    a = jax.nn.sigmoid(a0 + _dot(ad.astype(BF16), a2.astype(BF16)))
    g = _dot(jax.nn.sigmoid(gd).astype(BF16), g2.astype(BF16))
    kk = k * k_k
    norm = jnp.sqrt(_head_sum(kk * kk, gsel, gselt))
    kk = kk / jnp.maximum(norm, 1e-12)
    kmod = k * (1.0 + (a - 1.0) * k_a)
    bonus = _head_sum(r * kmod * r_k, gsel, gselt) * v
    return r, decay, kmod, v, kk, kk * a, g, bonus


def _rwkv_prep_seq_kernel(cur_rkv_ref, prev8_rkv_ref, cur_lo_ref, prev8_lo_ref, sp_rkv_ref, sp_lo_ref,
                          mu_rkv_ref, mu_lo_ref, w0_ref, a0_ref, kk_ref, ka_ref, rk_ref,
                          w2d_ref, a2_ref, g2_ref, gsel_ref, gselt_ref,
                          r_o, w_o, k_o, kk_o, b_o, v_o, g_o, bonus_o):
    first = pl.program_id(1) == 0
    cur_rkv = cur_rkv_ref[...]
    cur_lo = cur_lo_ref[...]
    last = V7X_SUBLANES - 1
    lanes = V7X_LANES
    first_rkv = jnp.where(first, sp_rkv_ref[...], prev8_rkv_ref[last:last + 1, :])
    first_lo = jnp.where(first, sp_lo_ref[...], prev8_lo_ref[last:last + 1, :])
    r, w, k, v, kk, b, g, bonus = _rwkv_prep_math(
        cur_rkv, _shift_rows(cur_rkv, first_rkv), cur_lo, _shift_rows(cur_lo, first_lo),
        mu_rkv_ref[...], mu_lo_ref[...], w0_ref[...], a0_ref[...], kk_ref[...], ka_ref[...], rk_ref[...],
        w2d_ref[...], a2_ref[...], g2_ref[...], gsel_ref[...], gselt_ref[...], g_o.shape[-1])
    g_o[...] = g
    bonus_o[...] = bonus
    for c in range(r_o.shape[1]):
        cols = slice(c * lanes, (c + 1) * lanes)
        for o_ref, val in zip((r_o, w_o, k_o, kk_o, b_o, v_o), (r, w, k, kk, b, v)):
            o_ref[:, c, :] = val[:, cols]


def _rwkv_prep_step_kernel(cur_rkv_ref, prev_rkv_ref, cur_lo_ref, prev_lo_ref,
                           mu_rkv_ref, mu_lo_ref, w0_ref, a0_ref, kk_ref, ka_ref, rk_ref,
                           w2d_ref, a2_ref, g2_ref, gsel_ref, gselt_ref,
                           r_o, w_o, k_o, v_o, kk_o, b_o, g_o, bonus_o):
    outs = _rwkv_prep_math(
        cur_rkv_ref[...], prev_rkv_ref[...], cur_lo_ref[...], prev_lo_ref[...],
        mu_rkv_ref[...], mu_lo_ref[...], w0_ref[...], a0_ref[...], kk_ref[...], ka_ref[...], rk_ref[...],
        w2d_ref[...], a2_ref[...], g2_ref[...], gsel_ref[...], gselt_ref[...], r_o.shape[-1])
    for o_ref, val in zip((r_o, w_o, k_o, v_o, kk_o, b_o, g_o, bonus_o), outs):
        o_ref[...] = val


def _const_spec(arr):
    nd = arr.ndim
    return pl.BlockSpec(arr.shape, lambda *_: (0,) * nd)


def _rwkv_prep_seq(p_rkv, p_lo, sp_rkv, sp_lo, consts, batch, seq, tt):
    m, n_rkv = p_rkv.shape
    n_lo = p_lo.shape[1]
    d_rwkv = n_rkv // 3
    nt = seq // tt
    sub = V7X_SUBLANES

    def cur_map(b, i):
        return (b * nt + i, 0)

    def prev_map(b, i):
        return (jnp.maximum((b * seq + i * tt) // sub - 1, 0), 0)

    lanes = V7X_LANES
    n_rows = d_rwkv // lanes
    cur_map3 = lambda b, i: (b * nt + i, 0, 0)
    out_shape = [jax.ShapeDtypeStruct((m, n_rows, lanes), F32)] * 6 + [jax.ShapeDtypeStruct((m, d_rwkv), F32)] * 2
    out_specs = [pl.BlockSpec((tt, n_rows, lanes), cur_map3)] * 6 + [pl.BlockSpec((tt, d_rwkv), cur_map)] * 2
    return pl.pallas_call(
        _rwkv_prep_seq_kernel,
        out_shape=out_shape,
        grid=(batch, nt),
        in_specs=[
            pl.BlockSpec((tt, n_rkv), cur_map),
            pl.BlockSpec((sub, n_rkv), prev_map),
            pl.BlockSpec((tt, n_lo), cur_map),
            pl.BlockSpec((sub, n_lo), prev_map),
            pl.BlockSpec((None, 1, n_rkv), lambda b, i: (b, 0, 0)),
            pl.BlockSpec((None, 1, n_lo), lambda b, i: (b, 0, 0)),
        ] + [_const_spec(c) for c in consts],
        out_specs=out_specs,
        compiler_params=_params("parallel", "arbitrary"),
        name="rwkv_prep_seq",
    )(p_rkv, p_rkv, p_lo, p_lo, sp_rkv, sp_lo, *consts)


def _rwkv_prep_step(p_rkv, prev_rkv, p_lo, prev_lo, consts, tb):
    m, n_rkv = p_rkv.shape
    n_lo = p_lo.shape[1]
    d_rwkv = n_rkv // 3
    out_sds = jax.ShapeDtypeStruct((m, d_rwkv), F32)
    row = lambda i: (i, 0)
    return pl.pallas_call(
        _rwkv_prep_step_kernel,
        out_shape=[out_sds] * 8,
        grid=(m // tb,),
        in_specs=[
            pl.BlockSpec((tb, n_rkv), row), pl.BlockSpec((tb, n_rkv), row),
            pl.BlockSpec((tb, n_lo), row), pl.BlockSpec((tb, n_lo), row),
        ] + [_const_spec(c) for c in consts],
        out_specs=[pl.BlockSpec((tb, d_rwkv), row)] * 8,
        compiler_params=_params("parallel"),
        name="rwkv_prep_step",
    )(p_rkv, prev_rkv, p_lo, prev_lo, *consts)


def _split_dot_stacked(x, g):
    hi = x.astype(BF16)
    mid = (x - hi.astype(F32)).astype(BF16)
    both = _dot(jnp.concatenate([hi, mid], axis=0), g)
    return both[:x.shape[0]] + both[x.shape[0]:]


def _rwkv_seq_kernel(r_ref, w_ref, k_ref, kk_ref, b_ref, v_ref, gsum_ref, y_ref, s_ref, *, group, n_chain):
    nb, tb, n_kq, lanes = r_ref.shape
    sub = V7X_SUBLANES
    n_v = s_ref.shape[1] // nb
    n_vq = n_v // sub
    nbc = nb // n_chain

    @pl.when(pl.program_id(0) == 0)
    def _():
        s_ref[...] = jnp.zeros_like(s_ref)

    lane_grp = lax.broadcasted_iota(jnp.int32, (sub, lanes), 1) // group
    gsum = gsum_ref[...]

    def update_state(t, c):
        seqs = range(c * nbc, (c + 1) * nbc)
        srows = slice(c * nbc * n_v, (c + 1) * nbc * n_v)

        def rows(ref, kq):
            return jnp.concatenate(
                [jnp.broadcast_to(ref[b, t, pl.ds(kq, 1), :], (n_v, lanes)) for b in seqs], axis=0)

        v_all = _split_dot_stacked(jnp.concatenate(
            [jnp.where(lane_grp == vq, v_ref[b, t], 0.0) for b in seqs for vq in range(n_vq)], axis=0), gsum)
        sa = s_ref[0, srows, :] * rows(kk_ref, 0)
        for kq in range(1, n_kq):
            sa += s_ref[kq, srows, :] * rows(kk_ref, kq)
        sa = _split_dot_stacked(sa, gsum)
        y = None
        for kq in range(n_kq):
            s_new = s_ref[kq, srows, :] * rows(w_ref, kq) - rows(b_ref, kq) * sa + rows(k_ref, kq) * v_all
            s_ref[kq, srows, :] = s_new
            term = s_new * rows(r_ref, kq)
            y = term if y is None else y + term
        return y

    def emit_y(t, c, y):
        y = _split_dot_stacked(y, gsum)
        for i in range(nbc):
            yc = jnp.zeros((sub, lanes), F32)
            for vq in range(n_vq):
                lo = (i * n_vq + vq) * sub
                yc = jnp.where(lane_grp == vq, y[lo:lo + sub, :], yc)
            y_ref[c * nbc + i, t] = yc

    def step(t, y_prev):
        y_cur = tuple(update_state(t, c) for c in range(n_chain))
        for c in range(n_chain):
            emit_y(t - 1, c, y_prev[c])
        return y_cur

    y_last = lax.fori_loop(1, tb, step, tuple(update_state(0, c) for c in range(n_chain)))
    for c in range(n_chain):
        emit_y(tb - 1, c, y_last[c])


def _rwkv_seq(r, w, k, kk, b, v, gsum, n_heads, tb):
    batch, seq, n_kq, lanes = r.shape
    sub = V7X_SUBLANES
    d_rwkv = n_kq * lanes
    assert lanes == V7X_LANES == sub * n_heads and d_rwkv == HEAD_DIM * n_heads and HEAD_DIM == sub * sub
    n_chain = 2 if batch % 2 == 0 else 1
    spec = pl.BlockSpec((batch, tb, n_kq, lanes), lambda i: (0, i, 0, 0))
    state_shape = (n_kq, batch * HEAD_DIM, lanes)
    return pl.pallas_call(
        functools.partial(_rwkv_seq_kernel, group=n_heads, n_chain=n_chain),
        out_shape=[jax.ShapeDtypeStruct((batch, seq, sub, lanes), F32),
                   jax.ShapeDtypeStruct(state_shape, F32)],
        grid=(seq // tb,),
        in_specs=[spec] * 6 + [_const_spec(gsum)],
        out_specs=[spec, pl.BlockSpec(state_shape, lambda i: (0, 0, 0))],
        compiler_params=_params("arbitrary"),
        name="rwkv_seq",
    )(r, w, k, kk, b, v, gsum)


def _rwkv_step_kernel(s_ref, r_ref, w_ref, k_ref, kk_ref, b_ref, v_ref, y_ref, so_ref):
    n = s_ref.shape[-1]
    eye = (lax.broadcasted_iota(jnp.int32, (n, n), 0) == lax.broadcasted_iota(jnp.int32, (n, n), 1)).astype(F32)
    s = s_ref[...]
    v_col = jnp.sum(eye * v_ref[...], axis=-1, keepdims=True)
    sa = jnp.sum(s * kk_ref[...], axis=-1, keepdims=True)
    s_new = s * w_ref[...] - sa * b_ref[...] + v_col * k_ref[...]
    so_ref[...] = s_new
    y_col = jnp.sum(s_new * r_ref[...], axis=-1, keepdims=True)
    y_ref[...] = jnp.sum(eye * y_col, axis=-2, keepdims=True)


def _rwkv_step(state, r, w, k, kk, b, v, sb):
    n_s, n_h, n, _ = state.shape
    sspec = pl.BlockSpec((sb, n_h, n, n), lambda i: (i, 0, 0, 0))
    vspec = pl.BlockSpec((sb, n_h, 1, n), lambda i: (i, 0, 0, 0))
    return pl.pallas_call(
        _rwkv_step_kernel,
        out_shape=[jax.ShapeDtypeStruct((n_s, n_h, 1, n), F32), jax.ShapeDtypeStruct(state.shape, F32)],
        grid=(n_s // sb,),
        in_specs=[sspec] + [vspec] * 6,
        out_specs=[vspec, sspec],
        compiler_params=_params("parallel"),
        name="rwkv_step",
    )(state, r, w, k, kk, b, v)


def _rwkv_post_kernel(y_ref, g_ref, bonus_ref, gng_ref, gnb_ref, beta_ref, gsel_ref, gselt_ref, o_ref):
    y = jnp.concatenate([y_ref[:, c, :] for c in range(y_ref.shape[1])], axis=-1)
    gsel = gsel_ref[...]
    gselt = gselt_ref[...]
    mu = _head_sum(y, gsel, gselt) * (1.0 / HEAD_DIM)
    yc = y - mu
    var = _head_sum(yc * yc, gsel, gselt) * (1.0 / HEAD_DIM)
    yn = yc * lax.rsqrt(var + GN_EPS) * gng_ref[...] + gnb_ref[...]
    o_ref[...] = ((yn + bonus_ref[...]) * g_ref[...] * beta_ref[...]).astype(o_ref.dtype)


def _rwkv_post(y, g, bonus, gn_g, gn_b, beta, gsel, gselt, tt):
    m, n_rows, lanes = y.shape
    d = n_rows * lanes
    row = pl.BlockSpec((tt, d), lambda i: (i, 0))
    consts = (gn_g, gn_b, beta, gsel, gselt)
    return pl.pallas_call(
        _rwkv_post_kernel,
        out_shape=jax.ShapeDtypeStruct((m, d), BF16),
        grid=(m // tt,),
        in_specs=[pl.BlockSpec((tt, n_rows, lanes), lambda i: (i, 0, 0)), row, row] + [_const_spec(c) for c in consts],
        out_specs=row,
        compiler_params=_params("parallel"),
        name="rwkv_post",
    )(y, g, bonus, *consts)


CONV_HALO = 32


def _glu(pc, d_conv):
    return pc[:, :d_conv] * jax.nn.sigmoid(pc[:, d_conv:])


def _conv_seq_kernel(pc_ref, halo_ref, cw_ref, cb_ref, lng_ref, lnb_ref, beta_ref, o_ref, st_ref, ubuf_ref,
                     ushift_ref, *, row_chunk):
    tt, d_conv = o_ref.shape
    i = pl.program_id(1)
    sub = V7X_SUBLANES
    n_hist = CONV_WIDTH - 1
    pad = CONV_HALO - n_hist
    u_halo = _glu(halo_ref[...], d_conv)
    ubuf_ref[0:CONV_HALO, :] = jnp.where(i == 0, jnp.zeros_like(u_halo), u_halo)
    ubuf_ref[CONV_HALO:, :] = _glu(pc_ref[...], d_conv)
    n_shift_rows = ushift_ref.shape[1]
    for s in range(1, sub):
        ushift_ref[s - 1] = ubuf_ref[s:s + n_shift_rows, :]

    def tap(base, w):
        off = pad + w
        q, s = off // sub, off % sub
        rows = slice(base + q * sub, base + q * sub + row_chunk)
        return (ubuf_ref[rows, :] if s == 0 else ushift_ref[s - 1, rows, :]) * cw_ref[w:w + 1, :]

    for base in range(0, tt, row_chunk):
        acc = tap(base, 0)
        for w in range(1, CONV_WIDTH):
            acc += tap(base, w)
        c = _ln_rows(acc + cb_ref[...], lng_ref[...], lnb_ref[...], LN_EPS)
        o_ref[base:base + row_chunk, :] = (c * jax.nn.sigmoid(c) * beta_ref[...]).astype(o_ref.dtype)

    @pl.when(i == pl.num_programs(1) - 1)
    def _():
        st_ref[...] = ubuf_ref[CONV_HALO + tt - n_hist:, :]


def _conv_seq(pc, cw, cb, ln_g, ln_b, beta, batch, seq, tt):
    m, two_d = pc.shape
    d_conv = two_d // 2
    nt = seq // tt
    hb = tt // CONV_HALO
    consts = (cw, cb, ln_g, ln_b, beta)
    kern = functools.partial(_conv_seq_kernel, row_chunk=16)
    return pl.pallas_call(
        kern,
        out_shape=[jax.ShapeDtypeStruct((m, d_conv), BF16),
                   jax.ShapeDtypeStruct((batch, CONV_WIDTH - 1, d_conv), F32)],
        grid=(batch, nt),
        in_specs=[
            pl.BlockSpec((tt, two_d), lambda b, i: (b * nt + i, 0)),
            pl.BlockSpec((CONV_HALO, two_d), lambda b, i: (jnp.maximum((b * nt + i) * hb - 1, 0), 0)),
        ] + [_const_spec(c) for c in consts],
        out_specs=[pl.BlockSpec((tt, d_conv), lambda b, i: (b * nt + i, 0)),
                   pl.BlockSpec((None, CONV_WIDTH - 1, d_conv), lambda b, i: (b, 0, 0))],
        scratch_shapes=[pltpu.VMEM((CONV_HALO + tt, d_conv), F32),
                        pltpu.VMEM((V7X_SUBLANES - 1, CONV_HALO + tt - V7X_SUBLANES, d_conv), F32)],
        compiler_params=_params("parallel", "arbitrary"),
        name="conv_seq",
    )(pc, pc, *consts)


def _conv_step_kernel(pc_ref, st_ref, cw_ref, cb_ref, lng_ref, lnb_ref, beta_ref, o_ref, so_ref):
    d_conv = o_ref.shape[-1]
    n_hist = CONV_WIDTH - 1
    pc = pc_ref[...]
    u = pc[:, :, :d_conv] * jax.nn.sigmoid(pc[:, :, d_conv:])
    st = st_ref[...]
    c = (jnp.sum(st * cw_ref[0:n_hist, :], axis=1, keepdims=True)
         + u * cw_ref[n_hist:CONV_WIDTH, :] + cb_ref[...])
    c = _ln_rows(c, lng_ref[...], lnb_ref[...], LN_EPS)
    o_ref[...] = c * jax.nn.sigmoid(c) * beta_ref[...]
    so_ref[:, 0:n_hist - 1, :] = st[:, 1:, :]
    so_ref[:, n_hist - 1:n_hist, :] = u


def _conv_step(pc, state, cw, cb, ln_g, ln_b, beta, sb):
    n_s, _, two_d = pc.shape
    d_conv = two_d // 2
    consts = (cw, cb, ln_g, ln_b, beta)
    sspec = pl.BlockSpec((sb, CONV_WIDTH - 1, d_conv), lambda i: (i, 0, 0))
    return pl.pallas_call(
        _conv_step_kernel,
        out_shape=[jax.ShapeDtypeStruct((n_s, 1, d_conv), F32), jax.ShapeDtypeStruct(state.shape, F32)],
        grid=(n_s // sb,),
        in_specs=[pl.BlockSpec((sb, 1, two_d), lambda i: (i, 0, 0)), sspec] + [_const_spec(c) for c in consts],
        out_specs=[pl.BlockSpec((sb, 1, d_conv), lambda i: (i, 0, 0)), sspec],
        compiler_params=_params("parallel"),
        name="conv_step",
    )(pc, state, *consts)


def _attn_seq_kernel(q_ref, k_ref, v_ref, o_ref, *, scale):
    d = q_ref.shape[-1]
    dh = d // N_MEM_HEADS
    for h in range(N_MEM_HEADS):
        cols = slice(h * dh, (h + 1) * dh)
        k = k_ref[:, cols].astype(BF16)
        s = lax.dot_general(q_ref[:, cols], k, (((1,), (1,)), ((), ())), preferred_element_type=F32) * scale
        s = s - jnp.max(s, axis=-1, keepdims=True)
        e = jnp.exp(s)
        p = e / jnp.sum(e, axis=-1, keepdims=True)
        o_ref[:, cols] = _dot(p.astype(BF16), v_ref[:, cols].astype(BF16)).astype(o_ref.dtype)


def _attn_seq(q, mk, mv, batch, seq, tq):
    m, d = q.shape
    n_mem = mk.shape[1]
    nt = seq // tq
    kern = functools.partial(_attn_seq_kernel, scale=1.0 / math.sqrt(d // N_MEM_HEADS))
    kv_spec = pl.BlockSpec((None, n_mem, d), lambda b, i: (b, 0, 0))
    return pl.pallas_call(
        kern,
        out_shape=jax.ShapeDtypeStruct((m, d), BF16),
        grid=(batch, nt),
        in_specs=[pl.BlockSpec((tq, d), lambda b, i: (b * nt + i, 0)), kv_spec, kv_spec],
        out_specs=pl.BlockSpec((tq, d), lambda b, i: (b * nt + i, 0)),
        compiler_params=_params("parallel", "arbitrary"),
        name="attn_seq",
    )(q, mk, mv)


def _attn_step_kernel(q_ref, k_ref, v_ref, o_ref, *, scale):
    for i in range(q_ref.shape[0]):
        s = jnp.sum(k_ref[i] * q_ref[i], axis=-1, keepdims=True) * scale
        s = s - jnp.max(s, axis=0, keepdims=True)
        e = jnp.exp(s)
        p = e / jnp.sum(e, axis=0, keepdims=True)
        o_ref[i] = jnp.sum(p * v_ref[i], axis=0, keepdims=True)


def _attn_step(q, mk, mv, sb):
    n_s, _, n_h, dh = q.shape
    n_mem = mk.shape[1]
    kern = functools.partial(_attn_step_kernel, scale=1.0 / math.sqrt(dh))
    qspec = pl.BlockSpec((sb, 1, n_h, dh), lambda i: (i, 0, 0, 0))
    kvspec = pl.BlockSpec((sb, n_mem, n_h, dh), lambda i: (i, 0, 0, 0))
    return pl.pallas_call(
        kern,
        out_shape=jax.ShapeDtypeStruct(q.shape, F32),
        grid=(n_s // sb,),
        in_specs=[qspec, kvspec, kvspec],
        out_specs=qspec,
        compiler_params=_params("parallel"),
        name="attn_step",
    )(q, mk, mv)


def _head_selectors(d_rwkv, n_heads):
    assert n_heads <= V7X_LANES
    lane = np.arange(V7X_LANES)
    chan = np.arange(d_rwkv)
    gsel = chan[:, None] % n_heads == lane[None, :]
    gsum = lane[:, None] % n_heads == lane[None, :] % n_heads
    as_bf16 = lambda a: jnp.asarray(a.astype(np.float32), BF16)
    return as_bf16(gsel), as_bf16(gsel.T), as_bf16(gsum)


def _to_key_order(x, n_heads):
    lead = x.shape[:-1]
    return jnp.swapaxes(x.reshape(*lead, n_heads, HEAD_DIM), -1, -2).reshape(*lead, n_heads * HEAD_DIM)


def _from_key_order(x, n_heads):
    lead = x.shape[:-1]
    return jnp.swapaxes(x.reshape(*lead, HEAD_DIM, n_heads), -1, -2).reshape(*lead, n_heads * HEAD_DIM)


def _to_value_order(x, n_heads):
    lead = x.shape[:-1]
    nd = len(lead)
    sub = V7X_SUBLANES
    x = x.reshape(*lead, n_heads, HEAD_DIM // sub, sub)
    return x.transpose(*range(nd), nd + 2, nd + 1, nd).reshape(*lead, n_heads * HEAD_DIM)


def _from_value_order(x, n_heads):
    lead = x.shape[:-1]
    nd = len(lead)
    sub = V7X_SUBLANES
    x = x.reshape(*lead, sub, HEAD_DIM // sub, n_heads)
    return x.transpose(*range(nd), nd + 2, nd + 1, nd).reshape(*lead, n_heads * HEAD_DIM)


def kernel(x_prompt, x_sample, mem_prompt, state_shift, state_conv, state_wkv, cache_mem_k, cache_mem_v,
           ffn1_w1, ffn1_w3, ffn1_w2, ln1_g, ln1_b, w_in, mu_shift, w0, w2_decay, a0, a2_iclr, g2_gate,
           k_k, k_a, r_k, gn_g, gn_b, conv_w, conv_b, conv_ln_g, conv_ln_b, beta_rwkv, beta_conv, w_out,
           ln2_g, ln2_b, w_mq, w_mk, w_mv, w_mo, ln3_g, ln3_b, ffn2_w1, ffn2_w3, ffn2_w2, ln4_g, ln4_b):
    depth = ffn1_w1.shape[0]
    assert depth == 1
    batch, seq, d = x_prompt.shape
    n_s = x_sample.shape[0]
    assert x_sample.shape[1] == 1
    n_mem = mem_prompt.shape[1]
    d_rwkv = w0.shape[-1]
    d_conv = conv_b.shape[-1]
    n_heads = d_rwkv // HEAD_DIM
    n_rkv = 3 * d_rwkv
    shift_cols = mu_shift.shape[-1]
    alpha = (2.0 * depth) ** 0.25
    l = 0

    tok = functools.partial(_to_key_order, n_heads=n_heads)
    tov = functools.partial(_to_value_order, n_heads=n_heads)
    fromk = functools.partial(_from_key_order, n_heads=n_heads)
    fromv = functools.partial(_from_value_order, n_heads=n_heads)

    def rkv_to_kernel_order(x):
        return jnp.concatenate([tok(x[..., :d_rwkv]), tok(x[..., d_rwkv:2 * d_rwkv]), tov(x[..., 2 * d_rwkv:n_rkv])],
                               axis=-1)

    def rkv_from_kernel_order(x):
        return jnp.concatenate([fromk(x[..., :d_rwkv]), fromk(x[..., d_rwkv:2 * d_rwkv]),
                                fromv(x[..., 2 * d_rwkv:n_rkv])], axis=-1)

    w_rkv = rkv_to_kernel_order(w_in[l, :, :n_rkv])
    w_lo = w_in[l, :, n_rkv:shift_cols]
    w_cv = w_in[l, :, shift_cols:]
    mu_rkv = rkv_to_kernel_order(mu_shift[l:l + 1, :n_rkv])
    mu_lo = mu_shift[l:l + 1, n_rkv:]
    gsel, gselt, gsum = _head_selectors(d_rwkv, n_heads)
    prep_consts = (mu_rkv, mu_lo, tok(w0[l:l + 1]), tok(a0[l:l + 1]), tok(k_k[l:l + 1]), tok(k_a[l:l + 1]),
                   tok(r_k[l].reshape(1, d_rwkv)), tok(w2_decay[l]), tok(a2_iclr[l]), tov(g2_gate[l]), gsel, gselt)
    cw = conv_w[l].reshape(CONV_WIDTH, d_conv)
    conv_consts = (cw, conv_b[l:l + 1], conv_ln_g[l:l + 1], conv_ln_b[l:l + 1], beta_conv[l:l + 1])
    post_consts = (tov(gn_g[l:l + 1]), tov(gn_b[l:l + 1]), tov(beta_rwkv[l:l + 1]), gsel, gselt)
    w_out_rows = jnp.concatenate([tov(w_out[l, :d_rwkv].T).T, w_out[l, d_rwkv:]], axis=0)
    w_out_b = _cast_bf16(w_out_rows, 512)
    w_mo_b = _cast_bf16(w_mo[l], 512)

    def trunk_front(x, tm):
        x1, x1b = _ffn_ln(x, None, ffn1_w1[l], ffn1_w3[l], ffn1_w2[l], ln1_g[l:l + 1], ln1_b[l:l + 1], alpha, tm, 512)
        p_rkv = _mm(x1b, w_rkv, tm, 1024)
        p_lo = _mm(x1b, w_lo, tm, w_lo.shape[1])
        p_cv = _mm(x1b, w_cv, tm, 1024)
        return x1, p_rkv, p_lo, p_cv

    def trunk_back(x1, o_a, o_b, attend, tm):
        tm_ln = min(tm, 512)
        x2, x2b = _mm_res_ln([o_a, o_b], [(w_out_b, 0), (w_out_b, 1)], x1, ln2_g[l:l + 1], ln2_b[l:l + 1], alpha, tm_ln)
        q = _mm(x2b, w_mq[l], tm, 1024, BF16)
        att = attend(q)
        x3, x3b = _mm_res_ln([att], [(w_mo_b, 0)], x2, ln3_g[l:l + 1], ln3_b[l:l + 1], alpha, tm_ln)
        return _ffn_ln(x3, x3b, ffn2_w1[l], ffn2_w3[l], ffn2_w2[l], ln4_g[l:l + 1], ln4_b[l:l + 1], alpha, tm, 512)[0]

    m_p = batch * seq
    tm_p = 1024
    x1, p_rkv, p_lo, p_cv = trunk_front(x_prompt.reshape(m_p, d), tm_p)
    zeros_rkv = jnp.zeros((batch, 1, n_rkv), F32)
    zeros_lo = jnp.zeros((batch, 1, shift_cols - n_rkv), F32)
    sub = V7X_SUBLANES
    r, w, k, kk, bb, v, g, bonus = _rwkv_prep_seq(p_rkv, p_lo, zeros_rkv, zeros_lo, prep_consts, batch, seq, 256)
    bt = lambda t: t.reshape(batch, seq, *t.shape[1:])
    y4, s_t = _rwkv_seq(bt(r), bt(w), bt(k), bt(kk), bt(bb), bt(v), gsum, n_heads, 64)
    wkv_p = s_t.reshape(HEAD_DIM // sub, batch, HEAD_DIM, sub, n_heads).transpose(1, 4, 2, 0, 3).reshape(
        batch, n_heads, HEAD_DIM, HEAD_DIM)
    o_a = _rwkv_post(y4.reshape(m_p, sub, V7X_LANES), g, bonus, *post_consts, 256)
    o_b, conv_p = _conv_seq(p_cv, *conv_consts, batch, seq, 256)

    mem2 = mem_prompt.reshape(batch * n_mem, d)
    mk_p = _mm(mem2, w_mk[l], batch * n_mem, 512)
    mv_p = _mm(mem2, w_mv[l], batch * n_mem, 512)
    attend_p = lambda q: _attn_seq(q, mk_p.reshape(batch, n_mem, d), mv_p.reshape(batch, n_mem, d), batch, seq, 512)
    y_prompt = trunk_back(x1, o_a, o_b, attend_p, tm_p).reshape(batch, seq, d)
    shift_p = jnp.concatenate([rkv_from_kernel_order(p_rkv.reshape(batch, seq, n_rkv)[:, -1]),
                               p_lo.reshape(batch, seq, -1)[:, -1]], axis=-1)
    dh = d // N_MEM_HEADS

    xs1, ps_rkv, ps_lo, ps_cv = trunk_front(x_sample.reshape(n_s, d), n_s)
    prev = state_shift[l]
    rs, ws, ks, vs, kks, bs, gs, bonus_s = _rwkv_prep_step(ps_rkv, rkv_to_kernel_order(prev[:, :n_rkv]), ps_lo,
                                                          prev[:, n_rkv:], prep_consts, n_s)
    hdk = lambda t: fromk(t).reshape(n_s, n_heads, 1, HEAD_DIM)
    ys, wkv_s = _rwkv_step(state_wkv[l], hdk(rs), hdk(ws), hdk(ks), hdk(kks), hdk(bs),
                           fromv(vs).reshape(n_s, n_heads, 1, HEAD_DIM), 8)
    ys = tov(ys.reshape(n_s, d_rwkv)).reshape(n_s, sub, V7X_LANES)
    o_as = _rwkv_post(ys, gs, bonus_s, *post_consts, n_s)
    o_bs, conv_s = _conv_step(ps_cv.reshape(n_s, 1, 2 * d_conv), state_conv[l], *conv_consts, 8)
    o_bs = o_bs.reshape(n_s, d_conv).astype(BF16)
    attend_s = lambda q: _attn_step(q.astype(F32).reshape(n_s, 1, N_MEM_HEADS, dh), cache_mem_k[l], cache_mem_v[l],
                                    2).reshape(n_s, d).astype(BF16)
    y_sample = trunk_back(xs1, o_as, o_bs, attend_s, n_s).reshape(n_s, 1, d)
    shift_s = jnp.concatenate([rkv_from_kernel_order(ps_rkv), ps_lo], axis=-1)

    return (y_prompt, y_sample,
            shift_p[None], conv_p[None], wkv_p[None],
            mk_p.reshape(1, batch, n_mem, N_MEM_HEADS, dh), mv_p.reshape(1, batch, n_mem, N_MEM_HEADS, dh),
            shift_s[None], conv_s[None], wkv_s[None])
```

```python
import functools
import math

import jax
import jax.numpy as jnp
import numpy as np
from jax import lax
from jax.experimental import pallas as pl
from jax.experimental.pallas import tpu as pltpu

F32 = jnp.float32
BF16 = jnp.bfloat16

V7X_LANES = 128
V7X_SUBLANES = 8
V7X_VMEM_LIMIT_BYTES = 56 * 1024 * 1024

HEAD_DIM = 64
CONV_WIDTH = 31
DECAY_LORA = 64
ICLR_LORA = 64
GATE_LORA = 160
N_MEM_HEADS = 4
LN_EPS = 1e-5
GN_EPS = 64e-5


def _params(*sem):
    return pltpu.CompilerParams(dimension_semantics=sem, vmem_limit_bytes=V7X_VMEM_LIMIT_BYTES)


def _resident(block_shape, index_map):
    return pl.BlockSpec(block_shape, index_map, pipeline_mode=pl.Buffered(1))


def _ln_rows(y, g, b, eps):
    mu = jnp.mean(y, axis=-1, keepdims=True)
    yc = y - mu
    var = jnp.mean(yc * yc, axis=-1, keepdims=True)
    return yc * lax.rsqrt(var + eps) * g + b


def _dot(a, b):
    return jnp.dot(a, b, preferred_element_type=F32)


def _split_dot(x, g):
    hi = x.astype(BF16)
    mid = (x - hi.astype(F32)).astype(BF16)
    return _dot(hi, g) + _dot(mid, g)


def _head_sum(x, gsel, gselt):
    return _split_dot(_split_dot(x, gsel), gselt)


def _ffn_ln_kernel(x_ref, *refs, alpha, n_chunk, row_chunk, cast_x):
    if cast_x:
        w1_ref, w3_ref, w2_ref, g_ref, b_ref, o_ref, ob_ref, xb_ref = refs
    else:
        xb_ref, w1_ref, w3_ref, w2_ref, g_ref, b_ref, o_ref, ob_ref = refs
    j = pl.program_id(1)
    tm, d = o_ref.shape

    @pl.when(j == 0)
    def _():
        o_ref[...] = jnp.zeros_like(o_ref)
        if cast_x:
            xb_ref[...] = x_ref[...].astype(BF16)

    xb = xb_ref[...]
    h1 = _dot(xb, w1_ref[...].astype(BF16))
    h3 = _dot(xb, w3_ref[...].astype(BF16))
    hb = (h1 * jax.nn.sigmoid(h1) * h3).astype(BF16)
    w2b = w2_ref[...].astype(BF16)
    cw = d // n_chunk
    for c in range(n_chunk):
        o_ref[:, c * cw:(c + 1) * cw] += _dot(hb, w2b[:, c * cw:(c + 1) * cw])

    @pl.when(j == pl.num_programs(1) - 1)
    def _():
        def body(r, carry):
            rows = pl.ds(pl.multiple_of(r * row_chunk, row_chunk), row_chunk)
            y = alpha * x_ref[rows, :] + 0.5 * o_ref[rows, :]
            y = _ln_rows(y, g_ref[...], b_ref[...], LN_EPS)
            o_ref[rows, :] = y
            ob_ref[rows, :] = y.astype(BF16)
            return carry

        lax.fori_loop(0, tm // row_chunk, body, 0)


def _ffn_ln(x, xb, w1, w3, w2, g, b, alpha, tm, tf):
    m, d = x.shape
    f = w1.shape[1]
    cast_x = xb is None
    kern = functools.partial(_ffn_ln_kernel, alpha=alpha, n_chunk=4, row_chunk=min(tm, 128), cast_x=cast_x)
    row_tile = lambda i, j: (i, 0)
    return pl.pallas_call(
        kern,
        out_shape=[jax.ShapeDtypeStruct((m, d), F32), jax.ShapeDtypeStruct((m, d), BF16)],
        grid=(m // tm, f // tf),
        in_specs=[pl.BlockSpec((tm, d), row_tile)] + ([] if cast_x else [_resident((tm, d), row_tile)]) + [
            pl.BlockSpec((d, tf), lambda i, j: (0, j)),
            pl.BlockSpec((d, tf), lambda i, j: (0, j)),
            pl.BlockSpec((tf, d), lambda i, j: (j, 0)),
            pl.BlockSpec((1, d), lambda i, j: (0, 0)),
            pl.BlockSpec((1, d), lambda i, j: (0, 0)),
        ],
        out_specs=[pl.BlockSpec((tm, d), row_tile), _resident((tm, d), row_tile)],
        scratch_shapes=[pltpu.VMEM((tm, d), BF16)] if cast_x else [],
        compiler_params=_params("parallel", "arbitrary"),
        name="ffn_ln",
    )(*((x,) if cast_x else (x, xb)), w1, w3, w2, g, b)


def _mm_kernel(x_ref, w_ref, o_ref):
    o_ref[...] = _dot(x_ref[...].astype(BF16), w_ref[...].astype(BF16)).astype(o_ref.dtype)


def _mm(x, w, tm, tn, out_dtype=F32):
    m, k = x.shape
    n = w.shape[1]
    return pl.pallas_call(
        _mm_kernel,
        out_shape=jax.ShapeDtypeStruct((m, n), out_dtype),
        grid=(m // tm, n // tn),
        in_specs=[
            pl.BlockSpec((tm, k), lambda i, j: (i, 0)),
            pl.BlockSpec((k, tn), lambda i, j: (0, j)),
        ],
        out_specs=pl.BlockSpec((tm, tn), lambda i, j: (i, j)),
        compiler_params=_params("parallel", "arbitrary"),
        name="mm",
    )(x, w)


def _cast_kernel(x_ref, o_ref):
    o_ref[...] = x_ref[...].astype(o_ref.dtype)


def _cast_bf16(w, tr):
    rows, cols = w.shape
    return pl.pallas_call(
        _cast_kernel,
        out_shape=jax.ShapeDtypeStruct(w.shape, BF16),
        grid=(rows // tr,),
        in_specs=[pl.BlockSpec((tr, cols), lambda i: (i, 0))],
        out_specs=pl.BlockSpec((tr, cols), lambda i: (i, 0)),
        compiler_params=_params("parallel"),
        name="cast_bf16",
    )(w)


def _take_cols_kernel(x_ref, *o_refs, starts):
    for o_ref, start in zip(o_refs, starts):
        o_ref[...] = x_ref[:, start:start + o_ref.shape[1]]


def _take_cols(w, windows, tr):
    rows, cols = w.shape
    return pl.pallas_call(
        functools.partial(_take_cols_kernel, starts=tuple(start for start, _ in windows)),
        out_shape=[jax.ShapeDtypeStruct((rows, width), w.dtype) for _, width in windows],
        grid=(rows // tr,),
        in_specs=[pl.BlockSpec((tr, cols), lambda i: (i, 0))],
        out_specs=[pl.BlockSpec((tr, width), lambda i: (i, 0)) for _, width in windows],
        compiler_params=_params("parallel"),
        name="take_cols",
    )(w)


def _mm_res_ln_kernel(*refs, n_a, alpha, row_chunk):
    a_refs = refs[:n_a]
    w_refs = refs[n_a:2 * n_a]
    res_ref, g_ref, b_ref, o_ref, ob_ref, acc_ref = refs[2 * n_a:]
    tm, d = acc_ref.shape

    acc = _dot(a_refs[0][...], w_refs[0][...])
    for a_ref, w_ref in zip(a_refs[1:], w_refs[1:]):
        acc += _dot(a_ref[...], w_ref[...])
    acc_ref[...] = acc

    def body(r, carry):
        rows = pl.ds(pl.multiple_of(r * row_chunk, row_chunk), row_chunk)
        y = _ln_rows(alpha * res_ref[rows, :] + acc_ref[rows, :], g_ref[...], b_ref[...], LN_EPS)
        o_ref[rows, :] = y
        ob_ref[rows, :] = y.astype(BF16)
        return carry

    lax.fori_loop(0, tm // row_chunk, body, 0)


def _mm_res_ln(a_list, w_list, res, g, b, alpha, tm):
    m, d = res.shape
    n_a = len(a_list)
    kern = functools.partial(_mm_res_ln_kernel, n_a=n_a, alpha=alpha, row_chunk=min(tm, 64))
    row_tile = lambda i: (i, 0)
    in_specs = [pl.BlockSpec((tm, a.shape[1]), row_tile) for a in a_list]
    in_specs += [_resident((a.shape[1], d), functools.partial(lambda i, r: (r, 0), r=r))
                 for a, (_, r) in zip(a_list, w_list)]
    in_specs += [
        pl.BlockSpec((tm, d), row_tile),
        pl.BlockSpec((1, d), lambda i: (0, 0)),
        pl.BlockSpec((1, d), lambda i: (0, 0)),
    ]
    return pl.pallas_call(
        kern,
        out_shape=[jax.ShapeDtypeStruct((m, d), F32), jax.ShapeDtypeStruct((m, d), BF16)],
        grid=(m // tm,),
        in_specs=in_specs,
        out_specs=[pl.BlockSpec((tm, d), row_tile), pl.BlockSpec((tm, d), row_tile)],
        scratch_shapes=[pltpu.VMEM((tm, d), F32)],
        compiler_params=_params("parallel"),
        name="mm_res_ln",
    )(*a_list, *[w for w, _ in w_list], res, g, b)


def _shift_rows(cur, first_row):
    rolled = pltpu.roll(cur, 1, 0)
    row_id = lax.broadcasted_iota(jnp.int32, cur.shape, 0)
    return jnp.where(row_id == 0, first_row, rolled)


def _rwkv_prep_math(cur_rkv, prev_rkv, cur_lo, prev_lo, mu_rkv, mu_lo, w0, a0, k_k, k_a, r_k,
                    w2d, a2, g2, gsel, gselt, d_rwkv):
    pm = cur_rkv + mu_rkv * (prev_rkv - cur_rkv)
    pl_ = cur_lo + mu_lo * (prev_lo - cur_lo)
    r = pm[:, :d_rwkv]
    k = pm[:, d_rwkv:2 * d_rwkv]
    v = pm[:, 2 * d_rwkv:]
    wd = pl_[:, :DECAY_LORA]
    ad = pl_[:, DECAY_LORA:DECAY_LORA + ICLR_LORA]
    gd = pl_[:, DECAY_LORA + ICLR_LORA:]
    z = w0 + _dot(jnp.tanh(wd).astype(BF16), w2d.astype(BF16))
    w_log = -jax.nn.softplus(-z) - 0.5
    decay = jnp.exp(-jnp.exp(w_log))
    a = jax.nn.sigmoid(a0 + _dot(ad.astype(BF16), a2.astype(BF16)))
    g = _dot(jax.nn.sigmoid(gd).astype(BF16), g2.astype(BF16))
    kk = k * k_k
    norm = jnp.sqrt(_head_sum(kk * kk, gsel, gselt))
    kk = kk / jnp.maximum(norm, 1e-12)
    kmod = k * (1.0 + (a - 1.0) * k_a)
    bonus = _head_sum(r * kmod * r_k, gsel, gselt) * v
    return r, decay, kmod, v, kk, kk * a, g, bonus


def _rwkv_prep_seq_kernel(cur_rkv_ref, prev8_rkv_ref, cur_lo_ref, prev8_lo_ref, sp_rkv_ref, sp_lo_ref,
                          mu_rkv_ref, mu_lo_ref, w0_ref, a0_ref, kk_ref, ka_ref, rk_ref,
                          w2d_ref, a2_ref, g2_ref, gsel_ref, gselt_ref,
                          r_o, w_o, k_o, kk_o, b_o, v_o, g_o, bonus_o):
    first = pl.program_id(1) == 0
    cur_rkv = cur_rkv_ref[...]
    cur_lo = cur_lo_ref[...]
    last = V7X_SUBLANES - 1
    lanes = V7X_LANES
    first_rkv = jnp.where(first, sp_rkv_ref[...], prev8_rkv_ref[last:last + 1, :])
    first_lo = jnp.where(first, sp_lo_ref[...], prev8_lo_ref[last:last + 1, :])
    r, w, k, v, kk, b, g, bonus = _rwkv_prep_math(
        cur_rkv, _shift_rows(cur_rkv, first_rkv), cur_lo, _shift_rows(cur_lo, first_lo),
        mu_rkv_ref[...], mu_lo_ref[...], w0_ref[...], a0_ref[...], kk_ref[...], ka_ref[...], rk_ref[...],
        w2d_ref[...], a2_ref[...], g2_ref[...], gsel_ref[...], gselt_ref[...], g_o.shape[-1])
    g_o[...] = g
    bonus_o[...] = bonus
    for c in range(r_o.shape[1]):
        cols = slice(c * lanes, (c + 1) * lanes)
        for o_ref, val in zip((r_o, w_o, k_o, kk_o, b_o, v_o), (r, w, k, kk, b, v)):
            o_ref[:, c, :] = val[:, cols]


def _rwkv_prep_step_kernel(cur_rkv_ref, prev_rkv_ref, cur_lo_ref, prev_lo_ref,
                           mu_rkv_ref, mu_lo_ref, w0_ref, a0_ref, kk_ref, ka_ref, rk_ref,
                           w2d_ref, a2_ref, g2_ref, gsel_ref, gselt_ref,
                           r_o, w_o, k_o, v_o, kk_o, b_o, g_o, bonus_o):
    outs = _rwkv_prep_math(
        cur_rkv_ref[...], prev_rkv_ref[...], cur_lo_ref[...], prev_lo_ref[...],
        mu_rkv_ref[...], mu_lo_ref[...], w0_ref[...], a0_ref[...], kk_ref[...], ka_ref[...], rk_ref[...],
        w2d_ref[...], a2_ref[...], g2_ref[...], gsel_ref[...], gselt_ref[...], r_o.shape[-1])
    for o_ref, val in zip((r_o, w_o, k_o, v_o, kk_o, b_o, g_o, bonus_o), outs):
        o_ref[...] = val


def _const_spec(arr):
    nd = arr.ndim
    return pl.BlockSpec(arr.shape, lambda *_: (0,) * nd)


def _rwkv_prep_seq(p_rkv, p_lo, sp_rkv, sp_lo, consts, batch, seq, tt):
    m, n_rkv = p_rkv.shape
    n_lo = p_lo.shape[1]
    d_rwkv = n_rkv // 3
    nt = seq // tt
    sub = V7X_SUBLANES

    def cur_map(b, i):
        return (b * nt + i, 0)

    def prev_map(b, i):
        return (jnp.maximum((b * seq + i * tt) // sub - 1, 0), 0)

    lanes = V7X_LANES
    n_rows = d_rwkv // lanes
    cur_map3 = lambda b, i: (b * nt + i, 0, 0)
    out_shape = [jax.ShapeDtypeStruct((m, n_rows, lanes), F32)] * 6 + [jax.ShapeDtypeStruct((m, d_rwkv), F32)] * 2
    out_specs = [pl.BlockSpec((tt, n_rows, lanes), cur_map3)] * 6 + [pl.BlockSpec((tt, d_rwkv), cur_map)] * 2
    return pl.pallas_call(
        _rwkv_prep_seq_kernel,
        out_shape=out_shape,
        grid=(batch, nt),
        in_specs=[
            pl.BlockSpec((tt, n_rkv), cur_map),
            pl.BlockSpec((sub, n_rkv), prev_map),
            pl.BlockSpec((tt, n_lo), cur_map),
            pl.BlockSpec((sub, n_lo), prev_map),
            pl.BlockSpec((None, 1, n_rkv), lambda b, i: (b, 0, 0)),
            pl.BlockSpec((None, 1, n_lo), lambda b, i: (b, 0, 0)),
        ] + [_const_spec(c) for c in consts],
        out_specs=out_specs,
        compiler_params=_params("parallel", "arbitrary"),
        name="rwkv_prep_seq",
    )(p_rkv, p_rkv, p_lo, p_lo, sp_rkv, sp_lo, *consts)


def _rwkv_prep_step(p_rkv, prev_rkv, p_lo, prev_lo, consts, tb):
    m, n_rkv = p_rkv.shape
    n_lo = p_lo.shape[1]
    d_rwkv = n_rkv // 3
    out_sds = jax.ShapeDtypeStruct((m, d_rwkv), F32)
    row = lambda i: (i, 0)
    return pl.pallas_call(
        _rwkv_prep_step_kernel,
        out_shape=[out_sds] * 8,
        grid=(m // tb,),
        in_specs=[
            pl.BlockSpec((tb, n_rkv), row), pl.BlockSpec((tb, n_rkv), row),
            pl.BlockSpec((tb, n_lo), row), pl.BlockSpec((tb, n_lo), row),
        ] + [_const_spec(c) for c in consts],
        out_specs=[pl.BlockSpec((tb, d_rwkv), row)] * 8,
        compiler_params=_params("parallel"),
        name="rwkv_prep_step",
    )(p_rkv, prev_rkv, p_lo, prev_lo, *consts)


def _split_dot_stacked(x, g):
    hi = x.astype(BF16)
    mid = (x - hi.astype(F32)).astype(BF16)
    both = _dot(jnp.concatenate([hi, mid], axis=0), g)
    return both[:x.shape[0]] + both[x.shape[0]:]


def _rwkv_seq_kernel(r_ref, w_ref, k_ref, kk_ref, b_ref, v_ref, gsum_ref, y_ref, s_ref, *, group, n_chain):
    nb, tb, n_kq, lanes = r_ref.shape
    sub = V7X_SUBLANES
    n_v = s_ref.shape[1] // nb
    n_vq = n_v // sub
    nbc = nb // n_chain

    @pl.when(pl.program_id(0) == 0)
    def _():
        s_ref[...] = jnp.zeros_like(s_ref)

    lane_grp = lax.broadcasted_iota(jnp.int32, (sub, lanes), 1) // group
    gsum = gsum_ref[...]

    def update_state(t, c):
        seqs = range(c * nbc, (c + 1) * nbc)
        srows = slice(c * nbc * n_v, (c + 1) * nbc * n_v)

        def rows(ref, kq):
            return jnp.concatenate(
                [jnp.broadcast_to(ref[b, t, pl.ds(kq, 1), :], (n_v, lanes)) for b in seqs], axis=0)

        v_all = _split_dot_stacked(jnp.concatenate(
            [jnp.where(lane_grp == vq, v_ref[b, t], 0.0) for b in seqs for vq in range(n_vq)], axis=0), gsum)
        sa = s_ref[0, srows, :] * rows(kk_ref, 0)
        for kq in range(1, n_kq):
            sa += s_ref[kq, srows, :] * rows(kk_ref, kq)
        sa = _split_dot_stacked(sa, gsum)
        y = None
        for kq in range(n_kq):
            s_new = s_ref[kq, srows, :] * rows(w_ref, kq) - rows(b_ref, kq) * sa + rows(k_ref, kq) * v_all
            s_ref[kq, srows, :] = s_new
            term = s_new * rows(r_ref, kq)
            y = term if y is None else y + term
        return y

    def emit_y(t, c, y):
        y = _split_dot_stacked(y, gsum)
        for i in range(nbc):
            yc = jnp.zeros((sub, lanes), F32)
            for vq in range(n_vq):
                lo = (i * n_vq + vq) * sub
                yc = jnp.where(lane_grp == vq, y[lo:lo + sub, :], yc)
            y_ref[c * nbc + i, t] = yc

    def step(t, y_prev):
        y_cur = tuple(update_state(t, c) for c in range(n_chain))
        for c in range(n_chain):
            emit_y(t - 1, c, y_prev[c])
        return y_cur

    y_last = lax.fori_loop(1, tb, step, tuple(update_state(0, c) for c in range(n_chain)))
    for c in range(n_chain):
        emit_y(tb - 1, c, y_last[c])


def _rwkv_seq(r, w, k, kk, b, v, gsum, n_heads, tb):
    batch, seq, n_kq, lanes = r.shape
    sub = V7X_SUBLANES
    d_rwkv = n_kq * lanes
    assert lanes == V7X_LANES == sub * n_heads and d_rwkv == HEAD_DIM * n_heads and HEAD_DIM == sub * sub
    n_chain = 2 if batch % 2 == 0 else 1
    spec = pl.BlockSpec((batch, tb, n_kq, lanes), lambda i: (0, i, 0, 0))
    state_shape = (n_kq, batch * HEAD_DIM, lanes)
    return pl.pallas_call(
        functools.partial(_rwkv_seq_kernel, group=n_heads, n_chain=n_chain),
        out_shape=[jax.ShapeDtypeStruct((batch, seq, sub, lanes), F32),
                   jax.ShapeDtypeStruct(state_shape, F32)],
        grid=(seq // tb,),
        in_specs=[spec] * 6 + [_const_spec(gsum)],
        out_specs=[spec, pl.BlockSpec(state_shape, lambda i: (0, 0, 0))],
        compiler_params=_params("arbitrary"),
        name="rwkv_seq",
    )(r, w, k, kk, b, v, gsum)


def _rwkv_step_kernel(s_ref, r_ref, w_ref, k_ref, kk_ref, b_ref, v_ref, y_ref, so_ref):
    n = s_ref.shape[-1]
    eye = (lax.broadcasted_iota(jnp.int32, (n, n), 0) == lax.broadcasted_iota(jnp.int32, (n, n), 1)).astype(F32)
    s = s_ref[...]
    v_col = jnp.sum(eye * v_ref[...], axis=-1, keepdims=True)
    sa = jnp.sum(s * kk_ref[...], axis=-1, keepdims=True)
    s_new = s * w_ref[...] - sa * b_ref[...] + v_col * k_ref[...]
    so_ref[...] = s_new
    y_col = jnp.sum(s_new * r_ref[...], axis=-1, keepdims=True)
    y_ref[...] = jnp.sum(eye * y_col, axis=-2, keepdims=True)


def _rwkv_step(state, r, w, k, kk, b, v, sb):
    n_s, n_h, n, _ = state.shape
    sspec = pl.BlockSpec((sb, n_h, n, n), lambda i: (i, 0, 0, 0))
    vspec = pl.BlockSpec((sb, n_h, 1, n), lambda i: (i, 0, 0, 0))
    return pl.pallas_call(
        _rwkv_step_kernel,
        out_shape=[jax.ShapeDtypeStruct((n_s, n_h, 1, n), F32), jax.ShapeDtypeStruct(state.shape, F32)],
        grid=(n_s // sb,),
        in_specs=[sspec] + [vspec] * 6,
        out_specs=[vspec, sspec],
        compiler_params=_params("parallel"),
        name="rwkv_step",
    )(state, r, w, k, kk, b, v)


def _rwkv_post_kernel(y_ref, g_ref, bonus_ref, gng_ref, gnb_ref, beta_ref, gsel_ref, gselt_ref, o_ref):
    y = jnp.concatenate([y_ref[:, c, :] for c in range(y_ref.shape[1])], axis=-1)
    gsel = gsel_ref[...]
    gselt = gselt_ref[...]
    mu = _head_sum(y, gsel, gselt) * (1.0 / HEAD_DIM)
    yc = y - mu
    var = _head_sum(yc * yc, gsel, gselt) * (1.0 / HEAD_DIM)
    yn = yc * lax.rsqrt(var + GN_EPS) * gng_ref[...] + gnb_ref[...]
    o_ref[...] = ((yn + bonus_ref[...]) * g_ref[...] * beta_ref[...]).astype(o_ref.dtype)


def _rwkv_post(y, g, bonus, gn_g, gn_b, beta, gsel, gselt, tt):
    m, n_rows, lanes = y.shape
    d = n_rows * lanes
    row = pl.BlockSpec((tt, d), lambda i: (i, 0))
    consts = (gn_g, gn_b, beta, gsel, gselt)
    return pl.pallas_call(
        _rwkv_post_kernel,
        out_shape=jax.ShapeDtypeStruct((m, d), BF16),
        grid=(m // tt,),
        in_specs=[pl.BlockSpec((tt, n_rows, lanes), lambda i: (i, 0, 0)), row, row] + [_const_spec(c) for c in consts],
        out_specs=row,
        compiler_params=_params("parallel"),
        name="rwkv_post",
    )(y, g, bonus, *consts)


CONV_HALO = 32


def _glu(pc, d_conv):
    return pc[:, :d_conv] * jax.nn.sigmoid(pc[:, d_conv:])


def _conv_seq_kernel(pc_ref, halo_ref, cw_ref, cb_ref, lng_ref, lnb_ref, beta_ref, o_ref, st_ref, ubuf_ref,
                     ushift_ref, *, row_chunk):
    tt, d_conv = o_ref.shape
    i = pl.program_id(1)
    sub = V7X_SUBLANES
    n_hist = CONV_WIDTH - 1
    pad = CONV_HALO - n_hist
    u_halo = _glu(halo_ref[...], d_conv)
    ubuf_ref[0:CONV_HALO, :] = jnp.where(i == 0, jnp.zeros_like(u_halo), u_halo)
    ubuf_ref[CONV_HALO:, :] = _glu(pc_ref[...], d_conv)
    n_shift_rows = ushift_ref.shape[1]
    for s in range(1, sub):
        ushift_ref[s - 1] = ubuf_ref[s:s + n_shift_rows, :]

    def tap(base, w):
        off = pad + w
        q, s = off // sub, off % sub
        rows = slice(base + q * sub, base + q * sub + row_chunk)
        return (ubuf_ref[rows, :] if s == 0 else ushift_ref[s - 1, rows, :]) * cw_ref[w:w + 1, :]

    for base in range(0, tt, row_chunk):
        acc = tap(base, 0)
        for w in range(1, CONV_WIDTH):
            acc += tap(base, w)
        c = _ln_rows(acc + cb_ref[...], lng_ref[...], lnb_ref[...], LN_EPS)
        o_ref[base:base + row_chunk, :] = (c * jax.nn.sigmoid(c) * beta_ref[...]).astype(o_ref.dtype)

    @pl.when(i == pl.num_programs(1) - 1)
    def _():
        st_ref[...] = ubuf_ref[CONV_HALO + tt - n_hist:, :]


def _conv_seq(pc, cw, cb, ln_g, ln_b, beta, batch, seq, tt):
    m, two_d = pc.shape
    d_conv = two_d // 2
    nt = seq // tt
    hb = tt // CONV_HALO
    consts = (cw, cb, ln_g, ln_b, beta)
    kern = functools.partial(_conv_seq_kernel, row_chunk=16)
    return pl.pallas_call(
        kern,
        out_shape=[jax.ShapeDtypeStruct((m, d_conv), BF16),
                   jax.ShapeDtypeStruct((batch, CONV_WIDTH - 1, d_conv), F32)],
        grid=(batch, nt),
        in_specs=[
            pl.BlockSpec((tt, two_d), lambda b, i: (b * nt + i, 0)),
            pl.BlockSpec((CONV_HALO, two_d), lambda b, i: (jnp.maximum((b * nt + i) * hb - 1, 0), 0)),
        ] + [_const_spec(c) for c in consts],
        out_specs=[pl.BlockSpec((tt, d_conv), lambda b, i: (b * nt + i, 0)),
                   pl.BlockSpec((None, CONV_WIDTH - 1, d_conv), lambda b, i: (b, 0, 0))],
        scratch_shapes=[pltpu.VMEM((CONV_HALO + tt, d_conv), F32),
                        pltpu.VMEM((V7X_SUBLANES - 1, CONV_HALO + tt - V7X_SUBLANES, d_conv), F32)],
        compiler_params=_params("parallel", "arbitrary"),
        name="conv_seq",
    )(pc, pc, *consts)


def _conv_step_kernel(pc_ref, st_ref, cw_ref, cb_ref, lng_ref, lnb_ref, beta_ref, o_ref, so_ref):
    d_conv = o_ref.shape[-1]
    n_hist = CONV_WIDTH - 1
    pc = pc_ref[...]
    u = pc[:, :, :d_conv] * jax.nn.sigmoid(pc[:, :, d_conv:])
    st = st_ref[...]
    c = (jnp.sum(st * cw_ref[0:n_hist, :], axis=1, keepdims=True)
         + u * cw_ref[n_hist:CONV_WIDTH, :] + cb_ref[...])
    c = _ln_rows(c, lng_ref[...], lnb_ref[...], LN_EPS)
    o_ref[...] = c * jax.nn.sigmoid(c) * beta_ref[...]
    so_ref[:, 0:n_hist - 1, :] = st[:, 1:, :]
    so_ref[:, n_hist - 1:n_hist, :] = u


def _conv_step(pc, state, cw, cb, ln_g, ln_b, beta, sb):
    n_s, _, two_d = pc.shape
    d_conv = two_d // 2
    consts = (cw, cb, ln_g, ln_b, beta)
    sspec = pl.BlockSpec((sb, CONV_WIDTH - 1, d_conv), lambda i: (i, 0, 0))
    return pl.pallas_call(
        _conv_step_kernel,
        out_shape=[jax.ShapeDtypeStruct((n_s, 1, d_conv), F32), jax.ShapeDtypeStruct(state.shape, F32)],
        grid=(n_s // sb,),
        in_specs=[pl.BlockSpec((sb, 1, two_d), lambda i: (i, 0, 0)), sspec] + [_const_spec(c) for c in consts],
        out_specs=[pl.BlockSpec((sb, 1, d_conv), lambda i: (i, 0, 0)), sspec],
        compiler_params=_params("parallel"),
        name="conv_step",
    )(pc, state, *consts)


def _attn_seq_kernel(q_ref, k_ref, v_ref, o_ref, *, scale):
    d = q_ref.shape[-1]
    dh = d // N_MEM_HEADS
    for h in range(N_MEM_HEADS):
        cols = slice(h * dh, (h + 1) * dh)
        k = k_ref[:, cols].astype(BF16)
        s = lax.dot_general(q_ref[:, cols], k, (((1,), (1,)), ((), ())), preferred_element_type=F32) * scale
        s = s - jnp.max(s, axis=-1, keepdims=True)
        e = jnp.exp(s)
        p = e / jnp.sum(e, axis=-1, keepdims=True)
        o_ref[:, cols] = _dot(p.astype(BF16), v_ref[:, cols].astype(BF16)).astype(o_ref.dtype)


def _attn_seq(q, mk, mv, batch, seq, tq):
    m, d = q.shape
    n_mem = mk.shape[1]
    nt = seq // tq
    kern = functools.partial(_attn_seq_kernel, scale=1.0 / math.sqrt(d // N_MEM_HEADS))
    kv_spec = pl.BlockSpec((None, n_mem, d), lambda b, i: (b, 0, 0))
    return pl.pallas_call(
        kern,
        out_shape=jax.ShapeDtypeStruct((m, d), BF16),
        grid=(batch, nt),
        in_specs=[pl.BlockSpec((tq, d), lambda b, i: (b * nt + i, 0)), kv_spec, kv_spec],
        out_specs=pl.BlockSpec((tq, d), lambda b, i: (b * nt + i, 0)),
        compiler_params=_params("parallel", "arbitrary"),
        name="attn_seq",
    )(q, mk, mv)


def _attn_step_kernel(q_ref, k_ref, v_ref, o_ref, *, scale):
    for i in range(q_ref.shape[0]):
        s = jnp.sum(k_ref[i] * q_ref[i], axis=-1, keepdims=True) * scale
        s = s - jnp.max(s, axis=0, keepdims=True)
        e = jnp.exp(s)
        p = e / jnp.sum(e, axis=0, keepdims=True)
        o_ref[i] = jnp.sum(p * v_ref[i], axis=0, keepdims=True)


def _attn_step(q, mk, mv, sb):
    n_s, _, n_h, dh = q.shape
    n_mem = mk.shape[1]
    kern = functools.partial(_attn_step_kernel, scale=1.0 / math.sqrt(dh))
    qspec = pl.BlockSpec((sb, 1, n_h, dh), lambda i: (i, 0, 0, 0))
    kvspec = pl.BlockSpec((sb, n_mem, n_h, dh), lambda i: (i, 0, 0, 0))
    return pl.pallas_call(
        kern,
        out_shape=jax.ShapeDtypeStruct(q.shape, F32),
        grid=(n_s // sb,),
        in_specs=[qspec, kvspec, kvspec],
        out_specs=qspec,
        compiler_params=_params("parallel"),
        name="attn_step",
    )(q, mk, mv)


def _head_selectors(d_rwkv, n_heads):
    assert n_heads <= V7X_LANES
    lane = np.arange(V7X_LANES)
    chan = np.arange(d_rwkv)
    gsel = chan[:, None] % n_heads == lane[None, :]
    gsum = lane[:, None] % n_heads == lane[None, :] % n_heads
    as_bf16 = lambda a: jnp.asarray(a.astype(np.float32), BF16)
    return as_bf16(gsel), as_bf16(gsel.T), as_bf16(gsum)


def _to_key_order(x, n_heads):
    lead = x.shape[:-1]
    return jnp.swapaxes(x.reshape(*lead, n_heads, HEAD_DIM), -1, -2).reshape(*lead, n_heads * HEAD_DIM)


def _from_key_order(x, n_heads):
    lead = x.shape[:-1]
    return jnp.swapaxes(x.reshape(*lead, HEAD_DIM, n_heads), -1, -2).reshape(*lead, n_heads * HEAD_DIM)


def _to_value_order(x, n_heads):
    lead = x.shape[:-1]
    nd = len(lead)
    sub = V7X_SUBLANES
    x = x.reshape(*lead, n_heads, HEAD_DIM // sub, sub)
    return x.transpose(*range(nd), nd + 2, nd + 1, nd).reshape(*lead, n_heads * HEAD_DIM)


def _from_value_order(x, n_heads):
    lead = x.shape[:-1]
    nd = len(lead)
    sub = V7X_SUBLANES
    x = x.reshape(*lead, sub, HEAD_DIM // sub, n_heads)
    return x.transpose(*range(nd), nd + 2, nd + 1, nd).reshape(*lead, n_heads * HEAD_DIM)


def kernel(x_prompt, x_sample, mem_prompt, state_shift, state_conv, state_wkv, cache_mem_k, cache_mem_v,
           ffn1_w1, ffn1_w3, ffn1_w2, ln1_g, ln1_b, w_in, mu_shift, w0, w2_decay, a0, a2_iclr, g2_gate,
           k_k, k_a, r_k, gn_g, gn_b, conv_w, conv_b, conv_ln_g, conv_ln_b, beta_rwkv, beta_conv, w_out,
           ln2_g, ln2_b, w_mq, w_mk, w_mv, w_mo, ln3_g, ln3_b, ffn2_w1, ffn2_w3, ffn2_w2, ln4_g, ln4_b):
    depth = ffn1_w1.shape[0]
    assert depth == 1
    batch, seq, d = x_prompt.shape
    n_s = x_sample.shape[0]
    assert x_sample.shape[1] == 1
    n_mem = mem_prompt.shape[1]
    d_rwkv = w0.shape[-1]
    d_conv = conv_b.shape[-1]
    n_heads = d_rwkv // HEAD_DIM
    n_rkv = 3 * d_rwkv
    shift_cols = mu_shift.shape[-1]
    alpha = (2.0 * depth) ** 0.25
    l = 0

    tok = functools.partial(_to_key_order, n_heads=n_heads)
    tov = functools.partial(_to_value_order, n_heads=n_heads)
    fromk = functools.partial(_from_key_order, n_heads=n_heads)
    fromv = functools.partial(_from_value_order, n_heads=n_heads)

    def rkv_to_kernel_order(x):
        return jnp.concatenate([tok(x[..., :d_rwkv]), tok(x[..., d_rwkv:2 * d_rwkv]), tov(x[..., 2 * d_rwkv:n_rkv])],
                               axis=-1)

    def rkv_from_kernel_order(x):
        return jnp.concatenate([fromk(x[..., :d_rwkv]), fromk(x[..., d_rwkv:2 * d_rwkv]),
                                fromv(x[..., 2 * d_rwkv:n_rkv])], axis=-1)

    w_rkv = rkv_to_kernel_order(w_in[l, :, :n_rkv])
    w_lo, w_cv = _take_cols(w_in[l], [(n_rkv, shift_cols - n_rkv), (shift_cols, 2 * d_conv)], 256)
    mu_rkv = rkv_to_kernel_order(mu_shift[l:l + 1, :n_rkv])
    mu_lo = mu_shift[l:l + 1, n_rkv:]
    gsel, gselt, gsum = _head_selectors(d_rwkv, n_heads)
    prep_consts = (mu_rkv, mu_lo, tok(w0[l:l + 1]), tok(a0[l:l + 1]), tok(k_k[l:l + 1]), tok(k_a[l:l + 1]),
                   tok(r_k[l].reshape(1, d_rwkv)), tok(w2_decay[l]), tok(a2_iclr[l]), tov(g2_gate[l]), gsel, gselt)
    cw = conv_w[l].reshape(CONV_WIDTH, d_conv)
    conv_consts = (cw, conv_b[l:l + 1], conv_ln_g[l:l + 1], conv_ln_b[l:l + 1], beta_conv[l:l + 1])
    post_consts = (tov(gn_g[l:l + 1]), tov(gn_b[l:l + 1]), tov(beta_rwkv[l:l + 1]), gsel, gselt)
    w_out_top_b = _cast_bf16(tov(w_out[l, :d_rwkv].T).T, 512)
    w_out_b = _cast_bf16(w_out[l], 512)
    w_mo_b = _cast_bf16(w_mo[l], 512)

    def trunk_front(x, tm):
        x1, x1b = _ffn_ln(x, None, ffn1_w1[l], ffn1_w3[l], ffn1_w2[l], ln1_g[l:l + 1], ln1_b[l:l + 1], alpha, tm, 256)
        p_rkv = _mm(x1b, w_rkv, tm, 1024)
        p_lo = _mm(x1b, w_lo, tm, w_lo.shape[1])
        p_cv = _mm(x1b, w_cv, tm, 1024)
        return x1, p_rkv, p_lo, p_cv

    def trunk_back(x1, o_a, o_b, attend, tm):
        tm_ln = min(tm, 512)
        x2, x2b = _mm_res_ln([o_a, o_b], [(w_out_top_b, 0), (w_out_b, 1)], x1, ln2_g[l:l + 1], ln2_b[l:l + 1], alpha, tm_ln)
        q = _mm(x2b, w_mq[l], tm, 1024, BF16)
        att = attend(q)
        x3, x3b = _mm_res_ln([att], [(w_mo_b, 0)], x2, ln3_g[l:l + 1], ln3_b[l:l + 1], alpha, tm_ln)
        return _ffn_ln(x3, x3b, ffn2_w1[l], ffn2_w3[l], ffn2_w2[l], ln4_g[l:l + 1], ln4_b[l:l + 1], alpha, tm, 256)[0]

    m_p = batch * seq
    tm_p = 1024
    x1, p_rkv, p_lo, p_cv = trunk_front(x_prompt.reshape(m_p, d), tm_p)
    zeros_rkv = jnp.zeros((batch, 1, n_rkv), F32)
    zeros_lo = jnp.zeros((batch, 1, shift_cols - n_rkv), F32)
    sub = V7X_SUBLANES
    r, w, k, kk, bb, v, g, bonus = _rwkv_prep_seq(p_rkv, p_lo, zeros_rkv, zeros_lo, prep_consts, batch, seq, 256)
    bt = lambda t: t.reshape(batch, seq, *t.shape[1:])
    y4, s_t = _rwkv_seq(bt(r), bt(w), bt(k), bt(kk), bt(bb), bt(v), gsum, n_heads, 64)
    wkv_p = s_t.reshape(HEAD_DIM // sub, batch, HEAD_DIM, sub, n_heads).transpose(1, 4, 2, 0, 3).reshape(
        batch, n_heads, HEAD_DIM, HEAD_DIM)
    o_a = _rwkv_post(y4.reshape(m_p, sub, V7X_LANES), g, bonus, *post_consts, 256)
    o_b, conv_p = _conv_seq(p_cv, *conv_consts, batch, seq, 256)

    mem2 = mem_prompt.reshape(batch * n_mem, d)
    mk_p = _mm(mem2, w_mk[l], batch * n_mem, 512)
    mv_p = _mm(mem2, w_mv[l], batch * n_mem, 512)
    attend_p = lambda q: _attn_seq(q, mk_p.reshape(batch, n_mem, d), mv_p.reshape(batch, n_mem, d), batch, seq, 512)
    y_prompt = trunk_back(x1, o_a, o_b, attend_p, tm_p).reshape(batch, seq, d)
    shift_p = jnp.concatenate([rkv_from_kernel_order(p_rkv.reshape(batch, seq, n_rkv)[:, -1]),
                               p_lo.reshape(batch, seq, -1)[:, -1]], axis=-1)
    dh = d // N_MEM_HEADS

    xs1, ps_rkv, ps_lo, ps_cv = trunk_front(x_sample.reshape(n_s, d), n_s)
    prev = state_shift[l]
    rs, ws, ks, vs, kks, bs, gs, bonus_s = _rwkv_prep_step(ps_rkv, rkv_to_kernel_order(prev[:, :n_rkv]), ps_lo,
                                                          prev[:, n_rkv:], prep_consts, n_s)
    hdk = lambda t: fromk(t).reshape(n_s, n_heads, 1, HEAD_DIM)
    ys, wkv_s = _rwkv_step(state_wkv[l], hdk(rs), hdk(ws), hdk(ks), hdk(kks), hdk(bs),
                           fromv(vs).reshape(n_s, n_heads, 1, HEAD_DIM), 8)
    ys = tov(ys.reshape(n_s, d_rwkv)).reshape(n_s, sub, V7X_LANES)
    o_as = _rwkv_post(ys, gs, bonus_s, *post_consts, n_s)
    o_bs, conv_s = _conv_step(ps_cv.reshape(n_s, 1, 2 * d_conv), state_conv[l], *conv_consts, 8)
    o_bs = o_bs.reshape(n_s, d_conv).astype(BF16)
    attend_s = lambda q: _attn_step(q.astype(F32).reshape(n_s, 1, N_MEM_HEADS, dh), cache_mem_k[l], cache_mem_v[l],
                                    2).reshape(n_s, d).astype(BF16)
    y_sample = trunk_back(xs1, o_as, o_bs, attend_s, n_s).reshape(n_s, 1, d)
    shift_s = jnp.concatenate([rkv_from_kernel_order(ps_rkv), ps_lo], axis=-1)

    return (y_prompt, y_sample,
            shift_p[None], conv_p[None], wkv_p[None],
            mk_p.reshape(1, batch, n_mem, N_MEM_HEADS, dh), mv_p.reshape(1, batch, n_mem, N_MEM_HEADS, dh),
            shift_s[None], conv_s[None], wkv_s[None])
```

```python
import functools
import math

import jax
import jax.numpy as jnp
import numpy as np
from jax import lax
from jax.experimental import pallas as pl
from jax.experimental.pallas import tpu as pltpu

F32 = jnp.float32
BF16 = jnp.bfloat16

V7X_LANES = 128
V7X_SUBLANES = 8
V7X_VMEM_LIMIT_BYTES = 56 * 1024 * 1024

HEAD_DIM = 64
CONV_WIDTH = 31
DECAY_LORA = 64
ICLR_LORA = 64
GATE_LORA = 160
N_MEM_HEADS = 4
LN_EPS = 1e-5
GN_EPS = 64e-5


def _params(*sem):
    return pltpu.CompilerParams(dimension_semantics=sem, vmem_limit_bytes=V7X_VMEM_LIMIT_BYTES)


def _resident(block_shape, index_map):
    return pl.BlockSpec(block_shape, index_map, pipeline_mode=pl.Buffered(1))


def _ln_rows(y, g, b, eps):
    mu = jnp.mean(y, axis=-1, keepdims=True)
    yc = y - mu
    var = jnp.mean(yc * yc, axis=-1, keepdims=True)
    return yc * lax.rsqrt(var + eps) * g + b


def _dot(a, b):
    return jnp.dot(a, b, preferred_element_type=F32)


def _split_dot(x, g):
    hi = x.astype(BF16)
    mid = (x - hi.astype(F32)).astype(BF16)
    return _dot(hi, g) + _dot(mid, g)


def _head_sum(x, gsel, gselt):
    return _split_dot(_split_dot(x, gsel), gselt)


def _ffn_ln_kernel(x_ref, *refs, alpha, n_chunk, row_chunk, cast_x):
    if cast_x:
        w1_ref, w3_ref, w2_ref, g_ref, b_ref, o_ref, ob_ref, xb_ref = refs
    else:
        xb_ref, w1_ref, w3_ref, w2_ref, g_ref, b_ref, o_ref, ob_ref = refs
    j = pl.program_id(1)
    tm, d = o_ref.shape

    @pl.when(j == 0)
    def _():
        o_ref[...] = jnp.zeros_like(o_ref)
        if cast_x:
            xb_ref[...] = x_ref[...].astype(BF16)

    xb = xb_ref[...]
    h1 = _dot(xb, w1_ref[...].astype(BF16))
    h3 = _dot(xb, w3_ref[...].astype(BF16))
    hb = (h1 * jax.nn.sigmoid(h1) * h3).astype(BF16)
    w2b = w2_ref[...].astype(BF16)
    cw = d // n_chunk
    for c in range(n_chunk):
        o_ref[:, c * cw:(c + 1) * cw] += _dot(hb, w2b[:, c * cw:(c + 1) * cw])

    @pl.when(j == pl.num_programs(1) - 1)
    def _():
        def body(r, carry):
            rows = pl.ds(pl.multiple_of(r * row_chunk, row_chunk), row_chunk)
            y = alpha * x_ref[rows, :] + 0.5 * o_ref[rows, :]
            y = _ln_rows(y, g_ref[...], b_ref[...], LN_EPS)
            o_ref[rows, :] = y
            ob_ref[rows, :] = y.astype(BF16)
            return carry

        lax.fori_loop(0, tm // row_chunk, body, 0)


def _ffn_ln(x, xb, w1, w3, w2, g, b, alpha, tm, tf):
    m, d = x.shape
    f = w1.shape[1]
    cast_x = xb is None
    kern = functools.partial(_ffn_ln_kernel, alpha=alpha, n_chunk=4, row_chunk=min(tm, 128), cast_x=cast_x)
    row_tile = lambda i, j: (i, 0)
    return pl.pallas_call(
        kern,
        out_shape=[jax.ShapeDtypeStruct((m, d), F32), jax.ShapeDtypeStruct((m, d), BF16)],
        grid=(m // tm, f // tf),
        in_specs=[pl.BlockSpec((tm, d), row_tile)] + ([] if cast_x else [_resident((tm, d), row_tile)]) + [
            pl.BlockSpec((d, tf), lambda i, j: (0, j)),
            pl.BlockSpec((d, tf), lambda i, j: (0, j)),
            pl.BlockSpec((tf, d), lambda i, j: (j, 0)),
            pl.BlockSpec((1, d), lambda i, j: (0, 0)),
            pl.BlockSpec((1, d), lambda i, j: (0, 0)),
        ],
        out_specs=[pl.BlockSpec((tm, d), row_tile), _resident((tm, d), row_tile)],
        scratch_shapes=[pltpu.VMEM((tm, d), BF16)] if cast_x else [],
        compiler_params=_params("parallel", "arbitrary"),
        name="ffn_ln",
    )(*((x,) if cast_x else (x, xb)), w1, w3, w2, g, b)


def _mm_kernel(x_ref, w_ref, o_ref):
    o_ref[...] = _dot(x_ref[...].astype(BF16), w_ref[...].astype(BF16)).astype(o_ref.dtype)


def _mm(x, w, tm, tn, out_dtype=F32):
    m, k = x.shape
    n = w.shape[1]
    return pl.pallas_call(
        _mm_kernel,
        out_shape=jax.ShapeDtypeStruct((m, n), out_dtype),
        grid=(m // tm, n // tn),
        in_specs=[
            pl.BlockSpec((tm, k), lambda i, j: (i, 0)),
            pl.BlockSpec((k, tn), lambda i, j: (0, j)),
        ],
        out_specs=pl.BlockSpec((tm, tn), lambda i, j: (i, j)),
        compiler_params=_params("parallel", "arbitrary"),
        name="mm",
    )(x, w)


def _cast_kernel(x_ref, o_ref):
    o_ref[...] = x_ref[...].astype(o_ref.dtype)


def _cast_bf16(w, tr):
    rows, cols = w.shape
    return pl.pallas_call(
        _cast_kernel,
        out_shape=jax.ShapeDtypeStruct(w.shape, BF16),
        grid=(rows // tr,),
        in_specs=[pl.BlockSpec((tr, cols), lambda i: (i, 0))],
        out_specs=pl.BlockSpec((tr, cols), lambda i: (i, 0)),
        compiler_params=_params("parallel"),
        name="cast_bf16",
    )(w)


def _mm_res_ln_kernel(*refs, n_a, alpha, row_chunk):
    a_refs = refs[:n_a]
    w_refs = refs[n_a:2 * n_a]
    res_ref, g_ref, b_ref, o_ref, ob_ref, acc_ref = refs[2 * n_a:]
    tm, d = acc_ref.shape

    acc = _dot(a_refs[0][...], w_refs[0][...])
    for a_ref, w_ref in zip(a_refs[1:], w_refs[1:]):
        acc += _dot(a_ref[...], w_ref[...])
    acc_ref[...] = acc

    def body(r, carry):
        rows = pl.ds(pl.multiple_of(r * row_chunk, row_chunk), row_chunk)
        y = _ln_rows(alpha * res_ref[rows, :] + acc_ref[rows, :], g_ref[...], b_ref[...], LN_EPS)
        o_ref[rows, :] = y
        ob_ref[rows, :] = y.astype(BF16)
        return carry

    lax.fori_loop(0, tm // row_chunk, body, 0)


def _mm_res_ln(a_list, w_list, res, g, b, alpha, tm):
    m, d = res.shape
    n_a = len(a_list)
    kern = functools.partial(_mm_res_ln_kernel, n_a=n_a, alpha=alpha, row_chunk=min(tm, 64))
    row_tile = lambda i: (i, 0)
    in_specs = [pl.BlockSpec((tm, a.shape[1]), row_tile) for a in a_list]
    in_specs += [_resident((a.shape[1], d), functools.partial(lambda i, r: (r, 0), r=r))
                 for a, (_, r) in zip(a_list, w_list)]
    in_specs += [
        pl.BlockSpec((tm, d), row_tile),
        pl.BlockSpec((1, d), lambda i: (0, 0)),
        pl.BlockSpec((1, d), lambda i: (0, 0)),
    ]
    return pl.pallas_call(
        kern,
        out_shape=[jax.ShapeDtypeStruct((m, d), F32), jax.ShapeDtypeStruct((m, d), BF16)],
        grid=(m // tm,),
        in_specs=in_specs,
        out_specs=[pl.BlockSpec((tm, d), row_tile), pl.BlockSpec((tm, d), row_tile)],
        scratch_shapes=[pltpu.VMEM((tm, d), F32)],
        compiler_params=_params("parallel"),
        name="mm_res_ln",
    )(*a_list, *[w for w, _ in w_list], res, g, b)


def _shift_rows(cur, first_row):
    rolled = pltpu.roll(cur, 1, 0)
    row_id = lax.broadcasted_iota(jnp.int32, cur.shape, 0)
    return jnp.where(row_id == 0, first_row, rolled)


def _rwkv_prep_math(cur_rkv, prev_rkv, cur_lo, prev_lo, mu_rkv, mu_lo, w0, a0, k_k, k_a, r_k,
                    w2d, a2, g2, gsel, gselt, d_rwkv):
    pm = cur_rkv + mu_rkv * (prev_rkv - cur_rkv)
    pl_ = cur_lo + mu_lo * (prev_lo - cur_lo)
    r = pm[:, :d_rwkv]
    k = pm[:, d_rwkv:2 * d_rwkv]
    v = pm[:, 2 * d_rwkv:]
    wd = pl_[:, :DECAY_LORA]
    ad = pl_[:, DECAY_LORA:DECAY_LORA + ICLR_LORA]
    gd = pl_[:, DECAY_LORA + ICLR_LORA:]
    z = w0 + _dot(jnp.tanh(wd).astype(BF16), w2d.astype(BF16))
    w_log = -jax.nn.softplus(-z) - 0.5
    decay = jnp.exp(-jnp.exp(w_log))
    a = jax.nn.sigmoid(a0 + _dot(ad.astype(BF16), a2.astype(BF16)))
    g = _dot(jax.nn.sigmoid(gd).astype(BF16), g2.astype(BF16))
    kk = k * k_k
    norm = jnp.sqrt(_head_sum(kk * kk, gsel, gselt))
    kk = kk / jnp.maximum(norm, 1e-12)
    kmod = k * (1.0 + (a - 1.0) * k_a)
    bonus = _head_sum(r * kmod * r_k, gsel, gselt) * v
    return r, decay, kmod, v, kk, kk * a, g, bonus


def _rwkv_prep_seq_kernel(cur_rkv_ref, prev8_rkv_ref, cur_lo_ref, prev8_lo_ref, sp_rkv_ref, sp_lo_ref,
                          mu_rkv_ref, mu_lo_ref, w0_ref, a0_ref, kk_ref, ka_ref, rk_ref,
                          w2d_ref, a2_ref, g2_ref, gsel_ref, gselt_ref,
                          r_o, w_o, k_o, kk_o, b_o, v_o, g_o, bonus_o):
    first = pl.program_id(1) == 0
    cur_rkv = cur_rkv_ref[...]
    cur_lo = cur_lo_ref[...]
    last = V7X_SUBLANES - 1
    lanes = V7X_LANES
    first_rkv = jnp.where(first, sp_rkv_ref[...], prev8_rkv_ref[last:last + 1, :])
    first_lo = jnp.where(first, sp_lo_ref[...], prev8_lo_ref[last:last + 1, :])
    r, w, k, v, kk, b, g, bonus = _rwkv_prep_math(
        cur_rkv, _shift_rows(cur_rkv, first_rkv), cur_lo, _shift_rows(cur_lo, first_lo),
        mu_rkv_ref[...], mu_lo_ref[...], w0_ref[...], a0_ref[...], kk_ref[...], ka_ref[...], rk_ref[...],
        w2d_ref[...], a2_ref[...], g2_ref[...], gsel_ref[...], gselt_ref[...], g_o.shape[-1])
    g_o[...] = g
    bonus_o[...] = bonus
    for c in range(r_o.shape[1]):
        cols = slice(c * lanes, (c + 1) * lanes)
        for o_ref, val in zip((r_o, w_o, k_o, kk_o, b_o, v_o), (r, w, k, kk, b, v)):
            o_ref[:, c, :] = val[:, cols]


def _rwkv_prep_step_kernel(cur_rkv_ref, prev_rkv_ref, cur_lo_ref, prev_lo_ref,
                           mu_rkv_ref, mu_lo_ref, w0_ref, a0_ref, kk_ref, ka_ref, rk_ref,
                           w2d_ref, a2_ref, g2_ref, gsel_ref, gselt_ref,
                           r_o, w_o, k_o, v_o, kk_o, b_o, g_o, bonus_o):
    outs = _rwkv_prep_math(
        cur_rkv_ref[...], prev_rkv_ref[...], cur_lo_ref[...], prev_lo_ref[...],
        mu_rkv_ref[...], mu_lo_ref[...], w0_ref[...], a0_ref[...], kk_ref[...], ka_ref[...], rk_ref[...],
        w2d_ref[...], a2_ref[...], g2_ref[...], gsel_ref[...], gselt_ref[...], r_o.shape[-1])
    for o_ref, val in zip((r_o, w_o, k_o, v_o, kk_o, b_o, g_o, bonus_o), outs):
        o_ref[...] = val


def _const_spec(arr):
    nd = arr.ndim
    return pl.BlockSpec(arr.shape, lambda *_: (0,) * nd)


def _rwkv_prep_seq(p_rkv, p_lo, sp_rkv, sp_lo, consts, batch, seq, tt):
    m, n_rkv = p_rkv.shape
    n_lo = p_lo.shape[1]
    d_rwkv = n_rkv // 3
    nt = seq // tt
    sub = V7X_SUBLANES

    def cur_map(b, i):
        return (b * nt + i, 0)

    def prev_map(b, i):
        return (jnp.maximum((b * seq + i * tt) // sub - 1, 0), 0)

    lanes = V7X_LANES
    n_rows = d_rwkv // lanes
    cur_map3 = lambda b, i: (b * nt + i, 0, 0)
    out_shape = [jax.ShapeDtypeStruct((m, n_rows, lanes), F32)] * 6 + [jax.ShapeDtypeStruct((m, d_rwkv), F32)] * 2
    out_specs = [pl.BlockSpec((tt, n_rows, lanes), cur_map3)] * 6 + [pl.BlockSpec((tt, d_rwkv), cur_map)] * 2
    return pl.pallas_call(
        _rwkv_prep_seq_kernel,
        out_shape=out_shape,
        grid=(batch, nt),
        in_specs=[
            pl.BlockSpec((tt, n_rkv), cur_map),
            pl.BlockSpec((sub, n_rkv), prev_map),
            pl.BlockSpec((tt, n_lo), cur_map),
            pl.BlockSpec((sub, n_lo), prev_map),
            pl.BlockSpec((None, 1, n_rkv), lambda b, i: (b, 0, 0)),
            pl.BlockSpec((None, 1, n_lo), lambda b, i: (b, 0, 0)),
        ] + [_const_spec(c) for c in consts],
        out_specs=out_specs,
        compiler_params=_params("parallel", "arbitrary"),
        name="rwkv_prep_seq",
    )(p_rkv, p_rkv, p_lo, p_lo, sp_rkv, sp_lo, *consts)


def _rwkv_prep_step(p_rkv, prev_rkv, p_lo, prev_lo, consts, tb):
    m, n_rkv = p_rkv.shape
    n_lo = p_lo.shape[1]
    d_rwkv = n_rkv // 3
    out_sds = jax.ShapeDtypeStruct((m, d_rwkv), F32)
    row = lambda i: (i, 0)
    return pl.pallas_call(
        _rwkv_prep_step_kernel,
        out_shape=[out_sds] * 8,
        grid=(m // tb,),
        in_specs=[
            pl.BlockSpec((tb, n_rkv), row), pl.BlockSpec((tb, n_rkv), row),
            pl.BlockSpec((tb, n_lo), row), pl.BlockSpec((tb, n_lo), row),
        ] + [_const_spec(c) for c in consts],
        out_specs=[pl.BlockSpec((tb, d_rwkv), row)] * 8,
        compiler_params=_params("parallel"),
        name="rwkv_prep_step",
    )(p_rkv, prev_rkv, p_lo, prev_lo, *consts)


def _split_dot_stacked(x, g):
    hi = x.astype(BF16)
    mid = (x - hi.astype(F32)).astype(BF16)
    both = _dot(jnp.concatenate([hi, mid], axis=0), g)
    return both[:x.shape[0]] + both[x.shape[0]:]


def _rwkv_seq_kernel(r_ref, w_ref, k_ref, kk_ref, b_ref, v_ref, gsum_ref, y_ref, s_ref, *, group, n_chain):
    nb, tb, n_kq, lanes = r_ref.shape
    sub = V7X_SUBLANES
    n_v = s_ref.shape[1] // nb
    n_vq = n_v // sub
    nbc = nb // n_chain

    @pl.when(pl.program_id(0) == 0)
    def _():
        s_ref[...] = jnp.zeros_like(s_ref)

    lane_grp = lax.broadcasted_iota(jnp.int32, (sub, lanes), 1) // group
    gsum = gsum_ref[...]

    def update_state(t, c):
        seqs = range(c * nbc, (c + 1) * nbc)
        srows = slice(c * nbc * n_v, (c + 1) * nbc * n_v)

        def rows(ref, kq):
            return jnp.concatenate(
                [jnp.broadcast_to(ref[b, t, pl.ds(kq, 1), :], (n_v, lanes)) for b in seqs], axis=0)

        v_all = _split_dot_stacked(jnp.concatenate(
            [jnp.where(lane_grp == vq, v_ref[b, t], 0.0) for b in seqs for vq in range(n_vq)], axis=0), gsum)
        sa = s_ref[0, srows, :] * rows(kk_ref, 0)
        for kq in range(1, n_kq):
            sa += s_ref[kq, srows, :] * rows(kk_ref, kq)
        sa = _split_dot_stacked(sa, gsum)
        y = None
        for kq in range(n_kq):
            s_new = s_ref[kq, srows, :] * rows(w_ref, kq) - rows(b_ref, kq) * sa + rows(k_ref, kq) * v_all
            s_ref[kq, srows, :] = s_new
            term = s_new * rows(r_ref, kq)
            y = term if y is None else y + term
        return y

    def emit_y(t, c, y):
        y = _split_dot_stacked(y, gsum)
        for i in range(nbc):
            yc = jnp.zeros((sub, lanes), F32)
            for vq in range(n_vq):
                lo = (i * n_vq + vq) * sub
                yc = jnp.where(lane_grp == vq, y[lo:lo + sub, :], yc)
            y_ref[c * nbc + i, t] = yc

    def step(t, y_prev):
        y_cur = tuple(update_state(t, c) for c in range(n_chain))
        for c in range(n_chain):
            emit_y(t - 1, c, y_prev[c])
        return y_cur

    y_last = lax.fori_loop(1, tb, step, tuple(update_state(0, c) for c in range(n_chain)))
    for c in range(n_chain):
        emit_y(tb - 1, c, y_last[c])


def _rwkv_seq(r, w, k, kk, b, v, gsum, n_heads, tb):
    batch, seq, n_kq, lanes = r.shape
    sub = V7X_SUBLANES
    d_rwkv = n_kq * lanes
    assert lanes == V7X_LANES == sub * n_heads and d_rwkv == HEAD_DIM * n_heads and HEAD_DIM == sub * sub
    n_chain = 2 if batch % 2 == 0 else 1
    spec = pl.BlockSpec((batch, tb, n_kq, lanes), lambda i: (0, i, 0, 0))
    state_shape = (n_kq, batch * HEAD_DIM, lanes)
    return pl.pallas_call(
        functools.partial(_rwkv_seq_kernel, group=n_heads, n_chain=n_chain),
        out_shape=[jax.ShapeDtypeStruct((batch, seq, sub, lanes), F32),
                   jax.ShapeDtypeStruct(state_shape, F32)],
        grid=(seq // tb,),
        in_specs=[spec] * 6 + [_const_spec(gsum)],
        out_specs=[spec, pl.BlockSpec(state_shape, lambda i: (0, 0, 0))],
        compiler_params=_params("arbitrary"),
        name="rwkv_seq",
    )(r, w, k, kk, b, v, gsum)


def _rwkv_step_kernel(s_ref, r_ref, w_ref, k_ref, kk_ref, b_ref, v_ref, y_ref, so_ref):
    n = s_ref.shape[-1]
    eye = (lax.broadcasted_iota(jnp.int32, (n, n), 0) == lax.broadcasted_iota(jnp.int32, (n, n), 1)).astype(F32)
    s = s_ref[...]
    v_col = jnp.sum(eye * v_ref[...], axis=-1, keepdims=True)
    sa = jnp.sum(s * kk_ref[...], axis=-1, keepdims=True)
    s_new = s * w_ref[...] - sa * b_ref[...] + v_col * k_ref[...]
    so_ref[...] = s_new
    y_col = jnp.sum(s_new * r_ref[...], axis=-1, keepdims=True)
    y_ref[...] = jnp.sum(eye * y_col, axis=-2, keepdims=True)


def _rwkv_step(state, r, w, k, kk, b, v, sb):
    n_s, n_h, n, _ = state.shape
    sspec = pl.BlockSpec((sb, n_h, n, n), lambda i: (i, 0, 0, 0))
    vspec = pl.BlockSpec((sb, n_h, 1, n), lambda i: (i, 0, 0, 0))
    return pl.pallas_call(
        _rwkv_step_kernel,
        out_shape=[jax.ShapeDtypeStruct((n_s, n_h, 1, n), F32), jax.ShapeDtypeStruct(state.shape, F32)],
        grid=(n_s // sb,),
        in_specs=[sspec] + [vspec] * 6,
        out_specs=[vspec, sspec],
        compiler_params=_params("parallel"),
        name="rwkv_step",
    )(state, r, w, k, kk, b, v)


def _rwkv_post_kernel(y_ref, g_ref, bonus_ref, gng_ref, gnb_ref, beta_ref, gsel_ref, gselt_ref, o_ref):
    y = jnp.concatenate([y_ref[:, c, :] for c in range(y_ref.shape[1])], axis=-1)
    gsel = gsel_ref[...]
    gselt = gselt_ref[...]
    mu = _head_sum(y, gsel, gselt) * (1.0 / HEAD_DIM)
    yc = y - mu
    var = _head_sum(yc * yc, gsel, gselt) * (1.0 / HEAD_DIM)
    yn = yc * lax.rsqrt(var + GN_EPS) * gng_ref[...] + gnb_ref[...]
    o_ref[...] = ((yn + bonus_ref[...]) * g_ref[...] * beta_ref[...]).astype(o_ref.dtype)


def _rwkv_post(y, g, bonus, gn_g, gn_b, beta, gsel, gselt, tt):
    m, n_rows, lanes = y.shape
    d = n_rows * lanes
    row = pl.BlockSpec((tt, d), lambda i: (i, 0))
    consts = (gn_g, gn_b, beta, gsel, gselt)
    return pl.pallas_call(
        _rwkv_post_kernel,
        out_shape=jax.ShapeDtypeStruct((m, d), BF16),
        grid=(m // tt,),
        in_specs=[pl.BlockSpec((tt, n_rows, lanes), lambda i: (i, 0, 0)), row, row] + [_const_spec(c) for c in consts],
        out_specs=row,
        compiler_params=_params("parallel"),
        name="rwkv_post",
    )(y, g, bonus, *consts)


CONV_HALO = 32


def _glu(pc, d_conv):
    return pc[:, :d_conv] * jax.nn.sigmoid(pc[:, d_conv:])


def _conv_seq_kernel(pc_ref, halo_ref, cw_ref, cb_ref, lng_ref, lnb_ref, beta_ref, o_ref, st_ref, ubuf_ref,
                     ushift_ref, *, row_chunk):
    tt, d_conv = o_ref.shape
    i = pl.program_id(1)
    sub = V7X_SUBLANES
    n_hist = CONV_WIDTH - 1
    pad = CONV_HALO - n_hist
    u_halo = _glu(halo_ref[...], d_conv)
    ubuf_ref[0:CONV_HALO, :] = jnp.where(i == 0, jnp.zeros_like(u_halo), u_halo)
    ubuf_ref[CONV_HALO:, :] = _glu(pc_ref[...], d_conv)
    n_shift_rows = ushift_ref.shape[1]
    for s in range(1, sub):
        ushift_ref[s - 1] = ubuf_ref[s:s + n_shift_rows, :]

    def tap(base, w):
        off = pad + w
        q, s = off // sub, off % sub
        rows = slice(base + q * sub, base + q * sub + row_chunk)
        return (ubuf_ref[rows, :] if s == 0 else ushift_ref[s - 1, rows, :]) * cw_ref[w:w + 1, :]

    for base in range(0, tt, row_chunk):
        acc = tap(base, 0)
        for w in range(1, CONV_WIDTH):
            acc += tap(base, w)
        c = _ln_rows(acc + cb_ref[...], lng_ref[...], lnb_ref[...], LN_EPS)
        o_ref[base:base + row_chunk, :] = (c * jax.nn.sigmoid(c) * beta_ref[...]).astype(o_ref.dtype)

    @pl.when(i == pl.num_programs(1) - 1)
    def _():
        st_ref[...] = ubuf_ref[CONV_HALO + tt - n_hist:, :]


def _conv_seq(pc, cw, cb, ln_g, ln_b, beta, batch, seq, tt):
    m, two_d = pc.shape
    d_conv = two_d // 2
    nt = seq // tt
    hb = tt // CONV_HALO
    consts = (cw, cb, ln_g, ln_b, beta)
    kern = functools.partial(_conv_seq_kernel, row_chunk=16)
    return pl.pallas_call(
        kern,
        out_shape=[jax.ShapeDtypeStruct((m, d_conv), BF16),
                   jax.ShapeDtypeStruct((batch, CONV_WIDTH - 1, d_conv), F32)],
        grid=(batch, nt),
        in_specs=[
            pl.BlockSpec((tt, two_d), lambda b, i: (b * nt + i, 0)),
            pl.BlockSpec((CONV_HALO, two_d), lambda b, i: (jnp.maximum((b * nt + i) * hb - 1, 0), 0)),
        ] + [_const_spec(c) for c in consts],
        out_specs=[pl.BlockSpec((tt, d_conv), lambda b, i: (b * nt + i, 0)),
                   pl.BlockSpec((None, CONV_WIDTH - 1, d_conv), lambda b, i: (b, 0, 0))],
        scratch_shapes=[pltpu.VMEM((CONV_HALO + tt, d_conv), F32),
                        pltpu.VMEM((V7X_SUBLANES - 1, CONV_HALO + tt - V7X_SUBLANES, d_conv), F32)],
        compiler_params=_params("parallel", "arbitrary"),
        name="conv_seq",
    )(pc, pc, *consts)


def _conv_step_kernel(pc_ref, st_ref, cw_ref, cb_ref, lng_ref, lnb_ref, beta_ref, o_ref, so_ref):
    d_conv = o_ref.shape[-1]
    n_hist = CONV_WIDTH - 1
    pc = pc_ref[...]
    u = pc[:, :, :d_conv] * jax.nn.sigmoid(pc[:, :, d_conv:])
    st = st_ref[...]
    c = (jnp.sum(st * cw_ref[0:n_hist, :], axis=1, keepdims=True)
         + u * cw_ref[n_hist:CONV_WIDTH, :] + cb_ref[...])
    c = _ln_rows(c, lng_ref[...], lnb_ref[...], LN_EPS)
    o_ref[...] = c * jax.nn.sigmoid(c) * beta_ref[...]
    so_ref[:, 0:n_hist - 1, :] = st[:, 1:, :]
    so_ref[:, n_hist - 1:n_hist, :] = u


def _conv_step(pc, state, cw, cb, ln_g, ln_b, beta, sb):
    n_s, _, two_d = pc.shape
    d_conv = two_d // 2
    consts = (cw, cb, ln_g, ln_b, beta)
    sspec = pl.BlockSpec((sb, CONV_WIDTH - 1, d_conv), lambda i: (i, 0, 0))
    return pl.pallas_call(
        _conv_step_kernel,
        out_shape=[jax.ShapeDtypeStruct((n_s, 1, d_conv), F32), jax.ShapeDtypeStruct(state.shape, F32)],
        grid=(n_s // sb,),
        in_specs=[pl.BlockSpec((sb, 1, two_d), lambda i: (i, 0, 0)), sspec] + [_const_spec(c) for c in consts],
        out_specs=[pl.BlockSpec((sb, 1, d_conv), lambda i: (i, 0, 0)), sspec],
        compiler_params=_params("parallel"),
        name="conv_step",
    )(pc, state, *consts)


def _attn_seq_kernel(q_ref, k_ref, v_ref, o_ref, *, scale):
    d = q_ref.shape[-1]
    dh = d // N_MEM_HEADS
    for h in range(N_MEM_HEADS):
        cols = slice(h * dh, (h + 1) * dh)
        k = k_ref[:, cols].astype(BF16)
        s = lax.dot_general(q_ref[:, cols], k, (((1,), (1,)), ((), ())), preferred_element_type=F32) * scale
        s = s - jnp.max(s, axis=-1, keepdims=True)
        e = jnp.exp(s)
        p = e / jnp.sum(e, axis=-1, keepdims=True)
        o_ref[:, cols] = _dot(p.astype(BF16), v_ref[:, cols].astype(BF16)).astype(o_ref.dtype)


def _attn_seq(q, mk, mv, batch, seq, tq):
    m, d = q.shape
    n_mem = mk.shape[1]
    nt = seq // tq
    kern = functools.partial(_attn_seq_kernel, scale=1.0 / math.sqrt(d // N_MEM_HEADS))
    kv_spec = pl.BlockSpec((None, n_mem, d), lambda b, i: (b, 0, 0))
    return pl.pallas_call(
        kern,
        out_shape=jax.ShapeDtypeStruct((m, d), BF16),
        grid=(batch, nt),
        in_specs=[pl.BlockSpec((tq, d), lambda b, i: (b * nt + i, 0)), kv_spec, kv_spec],
        out_specs=pl.BlockSpec((tq, d), lambda b, i: (b * nt + i, 0)),
        compiler_params=_params("parallel", "arbitrary"),
        name="attn_seq",
    )(q, mk, mv)


def _attn_step_kernel(q_ref, k_ref, v_ref, o_ref, *, scale):
    for i in range(q_ref.shape[0]):
        s = jnp.sum(k_ref[i] * q_ref[i], axis=-1, keepdims=True) * scale
        s = s - jnp.max(s, axis=0, keepdims=True)
        e = jnp.exp(s)
        p = e / jnp.sum(e, axis=0, keepdims=True)
        o_ref[i] = jnp.sum(p * v_ref[i], axis=0, keepdims=True)


def _attn_step(q, mk, mv, sb):
    n_s, _, n_h, dh = q.shape
    n_mem = mk.shape[1]
    kern = functools.partial(_attn_step_kernel, scale=1.0 / math.sqrt(dh))
    qspec = pl.BlockSpec((sb, 1, n_h, dh), lambda i: (i, 0, 0, 0))
    kvspec = pl.BlockSpec((sb, n_mem, n_h, dh), lambda i: (i, 0, 0, 0))
    return pl.pallas_call(
        kern,
        out_shape=jax.ShapeDtypeStruct(q.shape, F32),
        grid=(n_s // sb,),
        in_specs=[qspec, kvspec, kvspec],
        out_specs=qspec,
        compiler_params=_params("parallel"),
        name="attn_step",
    )(q, mk, mv)


def _head_selectors(d_rwkv, n_heads):
    assert n_heads <= V7X_LANES
    lane = np.arange(V7X_LANES)
    chan = np.arange(d_rwkv)
    gsel = chan[:, None] % n_heads == lane[None, :]
    gsum = lane[:, None] % n_heads == lane[None, :] % n_heads
    as_bf16 = lambda a: jnp.asarray(a.astype(np.float32), BF16)
    return as_bf16(gsel), as_bf16(gsel.T), as_bf16(gsum)


def _to_key_order(x, n_heads):
    lead = x.shape[:-1]
    return jnp.swapaxes(x.reshape(*lead, n_heads, HEAD_DIM), -1, -2).reshape(*lead, n_heads * HEAD_DIM)


def _from_key_order(x, n_heads):
    lead = x.shape[:-1]
    return jnp.swapaxes(x.reshape(*lead, HEAD_DIM, n_heads), -1, -2).reshape(*lead, n_heads * HEAD_DIM)


def _to_value_order(x, n_heads):
    lead = x.shape[:-1]
    nd = len(lead)
    sub = V7X_SUBLANES
    x = x.reshape(*lead, n_heads, HEAD_DIM // sub, sub)
    return x.transpose(*range(nd), nd + 2, nd + 1, nd).reshape(*lead, n_heads * HEAD_DIM)


def _from_value_order(x, n_heads):
    lead = x.shape[:-1]
    nd = len(lead)
    sub = V7X_SUBLANES
    x = x.reshape(*lead, sub, HEAD_DIM // sub, n_heads)
    return x.transpose(*range(nd), nd + 2, nd + 1, nd).reshape(*lead, n_heads * HEAD_DIM)


def kernel(x_prompt, x_sample, mem_prompt, state_shift, state_conv, state_wkv, cache_mem_k, cache_mem_v,
           ffn1_w1, ffn1_w3, ffn1_w2, ln1_g, ln1_b, w_in, mu_shift, w0, w2_decay, a0, a2_iclr, g2_gate,
           k_k, k_a, r_k, gn_g, gn_b, conv_w, conv_b, conv_ln_g, conv_ln_b, beta_rwkv, beta_conv, w_out,
           ln2_g, ln2_b, w_mq, w_mk, w_mv, w_mo, ln3_g, ln3_b, ffn2_w1, ffn2_w3, ffn2_w2, ln4_g, ln4_b):
    depth = ffn1_w1.shape[0]
    assert depth == 1
    batch, seq, d = x_prompt.shape
    n_s = x_sample.shape[0]
    assert x_sample.shape[1] == 1
    n_mem = mem_prompt.shape[1]
    d_rwkv = w0.shape[-1]
    d_conv = conv_b.shape[-1]
    n_heads = d_rwkv // HEAD_DIM
    n_rkv = 3 * d_rwkv
    shift_cols = mu_shift.shape[-1]
    alpha = (2.0 * depth) ** 0.25
    l = 0

    tok = functools.partial(_to_key_order, n_heads=n_heads)
    tov = functools.partial(_to_value_order, n_heads=n_heads)
    fromk = functools.partial(_from_key_order, n_heads=n_heads)
    fromv = functools.partial(_from_value_order, n_heads=n_heads)

    def rkv_to_kernel_order(x):
        return jnp.concatenate([tok(x[..., :d_rwkv]), tok(x[..., d_rwkv:2 * d_rwkv]), tov(x[..., 2 * d_rwkv:n_rkv])],
                               axis=-1)

    def rkv_from_kernel_order(x):
        return jnp.concatenate([fromk(x[..., :d_rwkv]), fromk(x[..., d_rwkv:2 * d_rwkv]),
                                fromv(x[..., 2 * d_rwkv:n_rkv])], axis=-1)

    w_rkv = rkv_to_kernel_order(w_in[l, :, :n_rkv])
    w_lo = w_in[l, :, n_rkv:shift_cols]
    w_cv = w_in[l, :, shift_cols:]
    mu_rkv = rkv_to_kernel_order(mu_shift[l:l + 1, :n_rkv])
    mu_lo = mu_shift[l:l + 1, n_rkv:]
    gsel, gselt, gsum = _head_selectors(d_rwkv, n_heads)
    prep_consts = (mu_rkv, mu_lo, tok(w0[l:l + 1]), tok(a0[l:l + 1]), tok(k_k[l:l + 1]), tok(k_a[l:l + 1]),
                   tok(r_k[l].reshape(1, d_rwkv)), tok(w2_decay[l]), tok(a2_iclr[l]), tov(g2_gate[l]), gsel, gselt)
    cw = conv_w[l].reshape(CONV_WIDTH, d_conv)
    conv_consts = (cw, conv_b[l:l + 1], conv_ln_g[l:l + 1], conv_ln_b[l:l + 1], beta_conv[l:l + 1])
    post_consts = (tov(gn_g[l:l + 1]), tov(gn_b[l:l + 1]), tov(beta_rwkv[l:l + 1]), gsel, gselt)
    w_out_top_b = _cast_bf16(tov(w_out[l, :d_rwkv].T).T, 512)
    w_out_b = _cast_bf16(w_out[l], 512)
    w_mo_b = _cast_bf16(w_mo[l], 512)

    def trunk_front(x, tm):
        x1, x1b = _ffn_ln(x, None, ffn1_w1[l], ffn1_w3[l], ffn1_w2[l], ln1_g[l:l + 1], ln1_b[l:l + 1], alpha, tm, 256)
        p_rkv = _mm(x1b, w_rkv, tm, 1024)
        p_lo = _mm(x1b, w_lo, tm, w_lo.shape[1])
        p_cv = _mm(x1b, w_cv, tm, 1024)
        return x1, p_rkv, p_lo, p_cv

    def trunk_back(x1, o_a, o_b, attend, tm):
        tm_ln = min(tm, 512)
        x2, x2b = _mm_res_ln([o_a, o_b], [(w_out_top_b, 0), (w_out_b, 1)], x1, ln2_g[l:l + 1], ln2_b[l:l + 1], alpha, tm_ln)
        q = _mm(x2b, w_mq[l], tm, 1024, BF16)
        att = attend(q)
        x3, x3b = _mm_res_ln([att], [(w_mo_b, 0)], x2, ln3_g[l:l + 1], ln3_b[l:l + 1], alpha, tm_ln)
        return _ffn_ln(x3, x3b, ffn2_w1[l], ffn2_w3[l], ffn2_w2[l], ln4_g[l:l + 1], ln4_b[l:l + 1], alpha, tm, 256)[0]

    m_p = batch * seq
    tm_p = 1024
    x1, p_rkv, p_lo, p_cv = trunk_front(x_prompt.reshape(m_p, d), tm_p)
    zeros_rkv = jnp.zeros((batch, 1, n_rkv), F32)
    zeros_lo = jnp.zeros((batch, 1, shift_cols - n_rkv), F32)
    sub = V7X_SUBLANES
    r, w, k, kk, bb, v, g, bonus = _rwkv_prep_seq(p_rkv, p_lo, zeros_rkv, zeros_lo, prep_consts, batch, seq, 256)
    bt = lambda t: t.reshape(batch, seq, *t.shape[1:])
    y4, s_t = _rwkv_seq(bt(r), bt(w), bt(k), bt(kk), bt(bb), bt(v), gsum, n_heads, 64)
    wkv_p = s_t.reshape(HEAD_DIM // sub, batch, HEAD_DIM, sub, n_heads).transpose(1, 4, 2, 0, 3).reshape(
        batch, n_heads, HEAD_DIM, HEAD_DIM)
    o_a = _rwkv_post(y4.reshape(m_p, sub, V7X_LANES), g, bonus, *post_consts, 256)
    o_b, conv_p = _conv_seq(p_cv, *conv_consts, batch, seq, 256)

    mem2 = mem_prompt.reshape(batch * n_mem, d)
    mk_p = _mm(mem2, w_mk[l], batch * n_mem, 512)
    mv_p = _mm(mem2, w_mv[l], batch * n_mem, 512)
    attend_p = lambda q: _attn_seq(q, mk_p.reshape(batch, n_mem, d), mv_p.reshape(batch, n_mem, d), batch, seq, 512)
    y_prompt = trunk_back(x1, o_a, o_b, attend_p, tm_p).reshape(batch, seq, d)
    shift_p = jnp.concatenate([rkv_from_kernel_order(p_rkv.reshape(batch, seq, n_rkv)[:, -1]),
                               p_lo.reshape(batch, seq, -1)[:, -1]], axis=-1)
    dh = d // N_MEM_HEADS

    xs1, ps_rkv, ps_lo, ps_cv = trunk_front(x_sample.reshape(n_s, d), n_s)
    prev = state_shift[l]
    rs, ws, ks, vs, kks, bs, gs, bonus_s = _rwkv_prep_step(ps_rkv, rkv_to_kernel_order(prev[:, :n_rkv]), ps_lo,
                                                          prev[:, n_rkv:], prep_consts, n_s)
    hdk = lambda t: fromk(t).reshape(n_s, n_heads, 1, HEAD_DIM)
    ys, wkv_s = _rwkv_step(state_wkv[l], hdk(rs), hdk(ws), hdk(ks), hdk(kks), hdk(bs),
                           fromv(vs).reshape(n_s, n_heads, 1, HEAD_DIM), 8)
    ys = tov(ys.reshape(n_s, d_rwkv)).reshape(n_s, sub, V7X_LANES)
    o_as = _rwkv_post(ys, gs, bonus_s, *post_consts, n_s)
    o_bs, conv_s = _conv_step(ps_cv.reshape(n_s, 1, 2 * d_conv), state_conv[l], *conv_consts, 8)
    o_bs = o_bs.reshape(n_s, d_conv).astype(BF16)
    attend_s = lambda q: _attn_step(q.astype(F32).reshape(n_s, 1, N_MEM_HEADS, dh), cache_mem_k[l], cache_mem_v[l],
                                    2).reshape(n_s, d).astype(BF16)
    y_sample = trunk_back(xs1, o_as, o_bs, attend_s, n_s).reshape(n_s, 1, d)
    shift_s = jnp.concatenate([rkv_from_kernel_order(ps_rkv), ps_lo], axis=-1)

    return (y_prompt, y_sample,
            shift_p[None], conv_p[None], wkv_p[None],
            mk_p.reshape(1, batch, n_mem, N_MEM_HEADS, dh), mv_p.reshape(1, batch, n_mem, N_MEM_HEADS, dh),
            shift_s[None], conv_s[None], wkv_s[None])
```

```python
import functools
import math

import jax
import jax.numpy as jnp
import numpy as np
from jax import lax
from jax.experimental import pallas as pl
from jax.experimental.pallas import tpu as pltpu

F32 = jnp.float32
BF16 = jnp.bfloat16

V7X_LANES = 128
V7X_SUBLANES = 8
V7X_VMEM_LIMIT_BYTES = 56 * 1024 * 1024

HEAD_DIM = 64
CONV_WIDTH = 31
DECAY_LORA = 64
ICLR_LORA = 64
GATE_LORA = 160
N_MEM_HEADS = 4
LN_EPS = 1e-5
GN_EPS = 64e-5


def _params(*sem):
    return pltpu.CompilerParams(dimension_semantics=sem, vmem_limit_bytes=V7X_VMEM_LIMIT_BYTES)


def _resident(block_shape, index_map):
    return pl.BlockSpec(block_shape, index_map, pipeline_mode=pl.Buffered(1))


def _ln_rows(y, g, b, eps):
    mu = jnp.mean(y, axis=-1, keepdims=True)
    yc = y - mu
    var = jnp.mean(yc * yc, axis=-1, keepdims=True)
    return yc * lax.rsqrt(var + eps) * g + b


def _dot(a, b):
    return jnp.dot(a, b, preferred_element_type=F32)


def _split_dot(x, g):
    hi = x.astype(BF16)
    mid = (x - hi.astype(F32)).astype(BF16)
    return _dot(hi, g) + _dot(mid, g)


def _head_sum(x, gsel, gselt):
    return _split_dot(_split_dot(x, gsel), gselt)


def _ffn_ln_kernel(x_ref, *refs, alpha, n_chunk, row_chunk, cast_x):
    if cast_x:
        w1_ref, w3_ref, w2_ref, g_ref, b_ref, o_ref, ob_ref, xb_ref = refs
    else:
        xb_ref, w1_ref, w3_ref, w2_ref, g_ref, b_ref, o_ref, ob_ref = refs
    j = pl.program_id(1)
    tm, d = o_ref.shape

    @pl.when(j == 0)
    def _():
        o_ref[...] = jnp.zeros_like(o_ref)
        if cast_x:
            xb_ref[...] = x_ref[...].astype(BF16)

    xb = xb_ref[...]
    h1 = _dot(xb, w1_ref[...].astype(BF16))
    h3 = _dot(xb, w3_ref[...].astype(BF16))
    hb = (h1 * jax.nn.sigmoid(h1) * h3).astype(BF16)
    w2b = w2_ref[...].astype(BF16)
    cw = d // n_chunk
    for c in range(n_chunk):
        o_ref[:, c * cw:(c + 1) * cw] += _dot(hb, w2b[:, c * cw:(c + 1) * cw])

    @pl.when(j == pl.num_programs(1) - 1)
    def _():
        def body(r, carry):
            rows = pl.ds(pl.multiple_of(r * row_chunk, row_chunk), row_chunk)
            y = alpha * x_ref[rows, :] + 0.5 * o_ref[rows, :]
            y = _ln_rows(y, g_ref[...], b_ref[...], LN_EPS)
            o_ref[rows, :] = y
            ob_ref[rows, :] = y.astype(BF16)
            return carry

        lax.fori_loop(0, tm // row_chunk, body, 0)


def _ffn_ln(x, xb, w1, w3, w2, g, b, alpha, tm, tf):
    m, d = x.shape
    f = w1.shape[1]
    cast_x = xb is None
    kern = functools.partial(_ffn_ln_kernel, alpha=alpha, n_chunk=4, row_chunk=min(tm, 128), cast_x=cast_x)
    row_tile = lambda i, j: (i, 0)
    return pl.pallas_call(
        kern,
        out_shape=[jax.ShapeDtypeStruct((m, d), F32), jax.ShapeDtypeStruct((m, d), BF16)],
        grid=(m // tm, f // tf),
        in_specs=[pl.BlockSpec((tm, d), row_tile)] + ([] if cast_x else [_resident((tm, d), row_tile)]) + [
            pl.BlockSpec((d, tf), lambda i, j: (0, j)),
            pl.BlockSpec((d, tf), lambda i, j: (0, j)),
            pl.BlockSpec((tf, d), lambda i, j: (j, 0)),
            pl.BlockSpec((1, d), lambda i, j: (0, 0)),
            pl.BlockSpec((1, d), lambda i, j: (0, 0)),
        ],
        out_specs=[pl.BlockSpec((tm, d), row_tile), _resident((tm, d), row_tile)],
        scratch_shapes=[pltpu.VMEM((tm, d), BF16)] if cast_x else [],
        compiler_params=_params("parallel", "arbitrary"),
        name="ffn_ln",
    )(*((x,) if cast_x else (x, xb)), w1, w3, w2, g, b)


def _mm_kernel(x_ref, w_ref, o_ref):
    o_ref[...] = _dot(x_ref[...].astype(BF16), w_ref[...].astype(BF16)).astype(o_ref.dtype)


def _mm(x, w, tm, tn, out_dtype=F32):
    m, k = x.shape
    n = w.shape[1]
    return pl.pallas_call(
        _mm_kernel,
        out_shape=jax.ShapeDtypeStruct((m, n), out_dtype),
        grid=(m // tm, n // tn),
        in_specs=[
            pl.BlockSpec((tm, k), lambda i, j: (i, 0)),
            pl.BlockSpec((k, tn), lambda i, j: (0, j)),
        ],
        out_specs=pl.BlockSpec((tm, tn), lambda i, j: (i, j)),
        compiler_params=_params("parallel", "arbitrary"),
        name="mm",
    )(x, w)


def _cast_kernel(x_ref, o_ref):
    o_ref[...] = x_ref[...].astype(o_ref.dtype)


def _cast_bf16(w, tr):
    rows, cols = w.shape
    return pl.pallas_call(
        _cast_kernel,
        out_shape=jax.ShapeDtypeStruct(w.shape, BF16),
        grid=(rows // tr,),
        in_specs=[pl.BlockSpec((tr, cols), lambda i: (i, 0))],
        out_specs=pl.BlockSpec((tr, cols), lambda i: (i, 0)),
        compiler_params=_params("parallel"),
        name="cast_bf16",
    )(w)


def _mm_res_ln_kernel(*refs, n_a, alpha, row_chunk):
    a_refs = refs[:n_a]
    w_refs = refs[n_a:2 * n_a]
    res_ref, g_ref, b_ref, o_ref, ob_ref, acc_ref = refs[2 * n_a:]
    tm, d = acc_ref.shape

    acc = _dot(a_refs[0][...], w_refs[0][...])
    for a_ref, w_ref in zip(a_refs[1:], w_refs[1:]):
        acc += _dot(a_ref[...], w_ref[...])
    acc_ref[...] = acc

    def body(r, carry):
        rows = pl.ds(pl.multiple_of(r * row_chunk, row_chunk), row_chunk)
        y = _ln_rows(alpha * res_ref[rows, :] + acc_ref[rows, :], g_ref[...], b_ref[...], LN_EPS)
        o_ref[rows, :] = y
        ob_ref[rows, :] = y.astype(BF16)
        return carry

    lax.fori_loop(0, tm // row_chunk, body, 0)


def _mm_res_ln(a_list, w_list, res, g, b, alpha, tm):
    m, d = res.shape
    n_a = len(a_list)
    kern = functools.partial(_mm_res_ln_kernel, n_a=n_a, alpha=alpha, row_chunk=min(tm, 64))
    row_tile = lambda i: (i, 0)
    in_specs = [pl.BlockSpec((tm, a.shape[1]), row_tile) for a in a_list]
    in_specs += [_resident((a.shape[1], d), functools.partial(lambda i, r: (r, 0), r=r))
                 for a, (_, r) in zip(a_list, w_list)]
    in_specs += [
        pl.BlockSpec((tm, d), row_tile),
        pl.BlockSpec((1, d), lambda i: (0, 0)),
        pl.BlockSpec((1, d), lambda i: (0, 0)),
    ]
    return pl.pallas_call(
        kern,
        out_shape=[jax.ShapeDtypeStruct((m, d), F32), jax.ShapeDtypeStruct((m, d), BF16)],
        grid=(m // tm,),
        in_specs=in_specs,
        out_specs=[pl.BlockSpec((tm, d), row_tile), pl.BlockSpec((tm, d), row_tile)],
        scratch_shapes=[pltpu.VMEM((tm, d), F32)],
        compiler_params=_params("parallel"),
        name="mm_res_ln",
    )(*a_list, *[w for w, _ in w_list], res, g, b)


def _shift_rows(cur, first_row):
    rolled = pltpu.roll(cur, 1, 0)
    row_id = lax.broadcasted_iota(jnp.int32, cur.shape, 0)
    return jnp.where(row_id == 0, first_row, rolled)


def _rwkv_prep_math(cur_rkv, prev_rkv, cur_lo, prev_lo, mu_rkv, mu_lo, w0, a0, k_k, k_a, r_k,
                    w2d, a2, g2, gsel, gselt, d_rwkv):
    pm = cur_rkv + mu_rkv * (prev_rkv - cur_rkv)
    pl_ = cur_lo + mu_lo * (prev_lo - cur_lo)
    r = pm[:, :d_rwkv]
    k = pm[:, d_rwkv:2 * d_rwkv]
    v = pm[:, 2 * d_rwkv:]
    wd = pl_[:, :DECAY_LORA]
    ad = pl_[:, DECAY_LORA:DECAY_LORA + ICLR_LORA]
    gd = pl_[:, DECAY_LORA + ICLR_LORA:]
    z = w0 + _dot(jnp.tanh(wd).astype(BF16), w2d.astype(BF16))
    w_log = -jax.nn.softplus(-z) - 0.5
    decay = jnp.exp(-jnp.exp(w_log))
    a = jax.nn.sigmoid(a0 + _dot(ad.astype(BF16), a2.astype(BF16)))
    g = _dot(jax.nn.sigmoid(gd).astype(BF16), g2.astype(BF16))
    kk = k * k_k
    norm = jnp.sqrt(_head_sum(kk * kk, gsel, gselt))
    kk = kk / jnp.maximum(norm, 1e-12)
    kmod = k * (1.0 + (a - 1.0) * k_a)
    bonus = _head_sum(r * kmod * r_k, gsel, gselt) * v
    return r, decay, kmod, v, kk, kk * a, g, bonus


def _rwkv_prep_seq_kernel(cur_rkv_ref, prev8_rkv_ref, cur_lo_ref, prev8_lo_ref, sp_rkv_ref, sp_lo_ref,
                          mu_rkv_ref, mu_lo_ref, w0_ref, a0_ref, kk_ref, ka_ref, rk_ref,
                          w2d_ref, a2_ref, g2_ref, gsel_ref, gselt_ref,
                          r_o, w_o, k_o, kk_o, b_o, v_o, g_o, bonus_o):
    first = pl.program_id(1) == 0
    cur_rkv = cur_rkv_ref[...]
    cur_lo = cur_lo_ref[...]
    last = V7X_SUBLANES - 1
    first_rkv = jnp.where(first, sp_rkv_ref[...], prev8_rkv_ref[last:last + 1, :])
    first_lo = jnp.where(first, sp_lo_ref[...], prev8_lo_ref[last:last + 1, :])
    r, w, k, v, kk, b, g, bonus = _rwkv_prep_math(
        cur_rkv, _shift_rows(cur_rkv, first_rkv), cur_lo, _shift_rows(cur_lo, first_lo),
        mu_rkv_ref[...], mu_lo_ref[...], w0_ref[...], a0_ref[...], kk_ref[...], ka_ref[...], rk_ref[...],
        w2d_ref[...], a2_ref[...], g2_ref[...], gsel_ref[...], gselt_ref[...], g_o.shape[-1])
    for o_ref, val in zip((r_o, w_o, k_o, kk_o, b_o, v_o, g_o, bonus_o), (r, w, k, kk, b, v, g, bonus)):
        o_ref[...] = val


def _rwkv_prep_step_kernel(cur_rkv_ref, prev_rkv_ref, cur_lo_ref, prev_lo_ref,
                           mu_rkv_ref, mu_lo_ref, w0_ref, a0_ref, kk_ref, ka_ref, rk_ref,
                           w2d_ref, a2_ref, g2_ref, gsel_ref, gselt_ref,
                           r_o, w_o, k_o, v_o, kk_o, b_o, g_o, bonus_o):
    outs = _rwkv_prep_math(
        cur_rkv_ref[...], prev_rkv_ref[...], cur_lo_ref[...], prev_lo_ref[...],
        mu_rkv_ref[...], mu_lo_ref[...], w0_ref[...], a0_ref[...], kk_ref[...], ka_ref[...], rk_ref[...],
        w2d_ref[...], a2_ref[...], g2_ref[...], gsel_ref[...], gselt_ref[...], r_o.shape[-1])
    for o_ref, val in zip((r_o, w_o, k_o, v_o, kk_o, b_o, g_o, bonus_o), outs):
        o_ref[...] = val


def _const_spec(arr):
    nd = arr.ndim
    return pl.BlockSpec(arr.shape, lambda *_: (0,) * nd)


def _rwkv_prep_seq(p_rkv, p_lo, sp_rkv, sp_lo, consts, batch, seq, tt):
    m, n_rkv = p_rkv.shape
    n_lo = p_lo.shape[1]
    d_rwkv = n_rkv // 3
    nt = seq // tt
    sub = V7X_SUBLANES

    def cur_map(b, i):
        return (b * nt + i, 0)

    def prev_map(b, i):
        return (jnp.maximum((b * seq + i * tt) // sub - 1, 0), 0)

    out_shape = [jax.ShapeDtypeStruct((m, d_rwkv), F32)] * 8
    out_specs = [pl.BlockSpec((tt, d_rwkv), cur_map)] * 8
    return pl.pallas_call(
        _rwkv_prep_seq_kernel,
        out_shape=out_shape,
        grid=(batch, nt),
        in_specs=[
            pl.BlockSpec((tt, n_rkv), cur_map),
            pl.BlockSpec((sub, n_rkv), prev_map),
            pl.BlockSpec((tt, n_lo), cur_map),
            pl.BlockSpec((sub, n_lo), prev_map),
            pl.BlockSpec((None, 1, n_rkv), lambda b, i: (b, 0, 0)),
            pl.BlockSpec((None, 1, n_lo), lambda b, i: (b, 0, 0)),
        ] + [_const_spec(c) for c in consts],
        out_specs=out_specs,
        compiler_params=_params("parallel", "arbitrary"),
        name="rwkv_prep_seq",
    )(p_rkv, p_rkv, p_lo, p_lo, sp_rkv, sp_lo, *consts)


def _rwkv_prep_step(p_rkv, prev_rkv, p_lo, prev_lo, consts, tb):
    m, n_rkv = p_rkv.shape
    n_lo = p_lo.shape[1]
    d_rwkv = n_rkv // 3
    out_sds = jax.ShapeDtypeStruct((m, d_rwkv), F32)
    row = lambda i: (i, 0)
    return pl.pallas_call(
        _rwkv_prep_step_kernel,
        out_shape=[out_sds] * 8,
        grid=(m // tb,),
        in_specs=[
            pl.BlockSpec((tb, n_rkv), row), pl.BlockSpec((tb, n_rkv), row),
            pl.BlockSpec((tb, n_lo), row), pl.BlockSpec((tb, n_lo), row),
        ] + [_const_spec(c) for c in consts],
        out_specs=[pl.BlockSpec((tb, d_rwkv), row)] * 8,
        compiler_params=_params("parallel"),
        name="rwkv_prep_step",
    )(p_rkv, prev_rkv, p_lo, prev_lo, *consts)


def _split_dot_stacked(x, g):
    hi = x.astype(BF16)
    mid = (x - hi.astype(F32)).astype(BF16)
    both = _dot(jnp.concatenate([hi, mid], axis=0), g)
    return both[:x.shape[0]] + both[x.shape[0]:]


def _rwkv_seq_kernel(r_ref, w_ref, k_ref, kk_ref, b_ref, v_ref, gsum_ref, y_ref, s_ref, *, group, n_chain):
    nb, n_t8, sub, _ = r_ref.shape
    n_kq, _, lanes = s_ref.shape
    n_v = s_ref.shape[1] // nb
    n_vq = n_v // sub
    nbc = nb // n_chain

    @pl.when(pl.program_id(0) == 0)
    def _():
        s_ref[...] = jnp.zeros_like(s_ref)

    lane_grp = lax.broadcasted_iota(jnp.int32, (sub, lanes), 1) // group
    sub_id = lax.broadcasted_iota(jnp.int32, (sub, lanes), 0)
    gsum = gsum_ref[...]

    def update_state(t8, j, c):
        seqs = range(c * nbc, (c + 1) * nbc)
        srows = slice(c * nbc * n_v, (c + 1) * nbc * n_v)

        def row(ref, b, cidx):
            return ref[b, t8, pl.ds(j, 1), cidx * lanes:(cidx + 1) * lanes]

        def rows(ref, kq):
            return jnp.concatenate([jnp.broadcast_to(row(ref, b, kq), (n_v, lanes)) for b in seqs], axis=0)

        def v_tile(b):
            vt = jnp.broadcast_to(row(v_ref, b, 0), (sub, lanes))
            for v8 in range(1, sub):
                vt = jnp.where(sub_id == v8, row(v_ref, b, v8), vt)
            return vt

        v_all = _split_dot_stacked(jnp.concatenate(
            [jnp.where(lane_grp == vq, vt, 0.0) for vt in map(v_tile, seqs) for vq in range(n_vq)], axis=0), gsum)
        sa = s_ref[0, srows, :] * rows(kk_ref, 0)
        for kq in range(1, n_kq):
            sa += s_ref[kq, srows, :] * rows(kk_ref, kq)
        sa = _split_dot_stacked(sa, gsum)
        y = None
        for kq in range(n_kq):
            s_new = s_ref[kq, srows, :] * rows(w_ref, kq) - rows(b_ref, kq) * sa + rows(k_ref, kq) * v_all
            s_ref[kq, srows, :] = s_new
            term = s_new * rows(r_ref, kq)
            y = term if y is None else y + term
        return y

    def emit_y(t8, j, c, y):
        y = _split_dot_stacked(y, gsum)
        for i in range(nbc):
            yc = jnp.zeros((sub, lanes), F32)
            for vq in range(n_vq):
                lo = (i * n_vq + vq) * sub
                yc = jnp.where(lane_grp == vq, y[lo:lo + sub, :], yc)
            for v8 in range(sub):
                y_ref[c * nbc + i, t8, pl.ds(j, 1), v8 * lanes:(v8 + 1) * lanes] = yc[v8:v8 + 1, :]

    def tile(t8, y_prev):
        for j in range(sub):
            y_cur = tuple(update_state(t8, j, c) for c in range(n_chain))
            if y_prev is not None:
                for c in range(n_chain):
                    emit_y(t8 if j else t8 - 1, (j - 1) % sub, c, y_prev[c])
            y_prev = y_cur
        return y_prev

    y_last = lax.fori_loop(1, n_t8, tile, tile(0, None))
    for c in range(n_chain):
        emit_y(n_t8 - 1, sub - 1, c, y_last[c])


def _rwkv_seq(r, w, k, kk, b, v, gsum, n_heads, tb):
    batch, seq8, sub, d_rwkv = r.shape
    lanes = V7X_LANES
    n_kq = d_rwkv // lanes
    assert sub == V7X_SUBLANES and lanes == sub * n_heads and d_rwkv == HEAD_DIM * n_heads and HEAD_DIM == sub * sub
    n_chain = 2 if batch % 2 == 0 else 1
    spec = pl.BlockSpec((batch, tb // sub, sub, d_rwkv), lambda i: (0, i, 0, 0))
    state_shape = (n_kq, batch * HEAD_DIM, lanes)
    seq = seq8 * sub
    return pl.pallas_call(
        functools.partial(_rwkv_seq_kernel, group=n_heads, n_chain=n_chain),
        out_shape=[jax.ShapeDtypeStruct(r.shape, F32),
                   jax.ShapeDtypeStruct(state_shape, F32)],
        grid=(seq // tb,),
        in_specs=[spec] * 6 + [_const_spec(gsum)],
        out_specs=[spec, pl.BlockSpec(state_shape, lambda i: (0, 0, 0))],
        compiler_params=_params("arbitrary"),
        name="rwkv_seq",
    )(r, w, k, kk, b, v, gsum)


def _rwkv_step_kernel(s_ref, r_ref, w_ref, k_ref, kk_ref, b_ref, v_ref, y_ref, so_ref):
    n = s_ref.shape[-1]
    eye = (lax.broadcasted_iota(jnp.int32, (n, n), 0) == lax.broadcasted_iota(jnp.int32, (n, n), 1)).astype(F32)
    s = s_ref[...]
    v_col = jnp.sum(eye * v_ref[...], axis=-1, keepdims=True)
    sa = jnp.sum(s * kk_ref[...], axis=-1, keepdims=True)
    s_new = s * w_ref[...] - sa * b_ref[...] + v_col * k_ref[...]
    so_ref[...] = s_new
    y_col = jnp.sum(s_new * r_ref[...], axis=-1, keepdims=True)
    y_ref[...] = jnp.sum(eye * y_col, axis=-2, keepdims=True)


def _rwkv_step(state, r, w, k, kk, b, v, sb):
    n_s, n_h, n, _ = state.shape
    sspec = pl.BlockSpec((sb, n_h, n, n), lambda i: (i, 0, 0, 0))
    vspec = pl.BlockSpec((sb, n_h, 1, n), lambda i: (i, 0, 0, 0))
    return pl.pallas_call(
        _rwkv_step_kernel,
        out_shape=[jax.ShapeDtypeStruct((n_s, n_h, 1, n), F32), jax.ShapeDtypeStruct(state.shape, F32)],
        grid=(n_s // sb,),
        in_specs=[sspec] + [vspec] * 6,
        out_specs=[vspec, sspec],
        compiler_params=_params("parallel"),
        name="rwkv_step",
    )(state, r, w, k, kk, b, v)


def _rwkv_post_kernel(y_ref, g_ref, bonus_ref, gng_ref, gnb_ref, beta_ref, gsel_ref, gselt_ref, o_ref):
    y = y_ref[...]
    gsel = gsel_ref[...]
    gselt = gselt_ref[...]
    mu = _head_sum(y, gsel, gselt) * (1.0 / HEAD_DIM)
    yc = y - mu
    var = _head_sum(yc * yc, gsel, gselt) * (1.0 / HEAD_DIM)
    yn = yc * lax.rsqrt(var + GN_EPS) * gng_ref[...] + gnb_ref[...]
    o_ref[...] = ((yn + bonus_ref[...]) * g_ref[...] * beta_ref[...]).astype(o_ref.dtype)


def _rwkv_post(y, g, bonus, gn_g, gn_b, beta, gsel, gselt, tt):
    m, d = y.shape
    row = pl.BlockSpec((tt, d), lambda i: (i, 0))
    consts = (gn_g, gn_b, beta, gsel, gselt)
    return pl.pallas_call(
        _rwkv_post_kernel,
        out_shape=jax.ShapeDtypeStruct((m, d), BF16),
        grid=(m // tt,),
        in_specs=[row, row, row] + [_const_spec(c) for c in consts],
        out_specs=row,
        compiler_params=_params("parallel"),
        name="rwkv_post",
    )(y, g, bonus, *consts)


CONV_HALO = 32


def _glu(pc, d_conv):
    return pc[:, :d_conv] * jax.nn.sigmoid(pc[:, d_conv:])


def _conv_seq_kernel(pc_ref, halo_ref, cw_ref, cb_ref, lng_ref, lnb_ref, beta_ref, o_ref, st_ref, ubuf_ref,
                     ushift_ref, *, row_chunk):
    tt, d_conv = o_ref.shape
    i = pl.program_id(1)
    sub = V7X_SUBLANES
    n_hist = CONV_WIDTH - 1
    pad = CONV_HALO - n_hist
    u_halo = _glu(halo_ref[...], d_conv)
    ubuf_ref[0:CONV_HALO, :] = jnp.where(i == 0, jnp.zeros_like(u_halo), u_halo)
    ubuf_ref[CONV_HALO:, :] = _glu(pc_ref[...], d_conv)
    n_shift_rows = ushift_ref.shape[1]
    for s in range(1, sub):
        ushift_ref[s - 1] = ubuf_ref[s:s + n_shift_rows, :]

    def tap(base, w):
        off = pad + w
        q, s = off // sub, off % sub
        rows = slice(base + q * sub, base + q * sub + row_chunk)
        return (ubuf_ref[rows, :] if s == 0 else ushift_ref[s - 1, rows, :]) * cw_ref[w:w + 1, :]

    for base in range(0, tt, row_chunk):
        acc = tap(base, 0)
        for w in range(1, CONV_WIDTH):
            acc += tap(base, w)
        c = _ln_rows(acc + cb_ref[...], lng_ref[...], lnb_ref[...], LN_EPS)
        o_ref[base:base + row_chunk, :] = (c * jax.nn.sigmoid(c) * beta_ref[...]).astype(o_ref.dtype)

    @pl.when(i == pl.num_programs(1) - 1)
    def _():
        st_ref[...] = ubuf_ref[CONV_HALO + tt - n_hist:, :]


def _conv_seq(pc, cw, cb, ln_g, ln_b, beta, batch, seq, tt):
    m, two_d = pc.shape
    d_conv = two_d // 2
    nt = seq // tt
    hb = tt // CONV_HALO
    consts = (cw, cb, ln_g, ln_b, beta)
    kern = functools.partial(_conv_seq_kernel, row_chunk=16)
    return pl.pallas_call(
        kern,
        out_shape=[jax.ShapeDtypeStruct((m, d_conv), BF16),
                   jax.ShapeDtypeStruct((batch, CONV_WIDTH - 1, d_conv), F32)],
        grid=(batch, nt),
        in_specs=[
            pl.BlockSpec((tt, two_d), lambda b, i: (b * nt + i, 0)),
            pl.BlockSpec((CONV_HALO, two_d), lambda b, i: (jnp.maximum((b * nt + i) * hb - 1, 0), 0)),
        ] + [_const_spec(c) for c in consts],
        out_specs=[pl.BlockSpec((tt, d_conv), lambda b, i: (b * nt + i, 0)),
                   pl.BlockSpec((None, CONV_WIDTH - 1, d_conv), lambda b, i: (b, 0, 0))],
        scratch_shapes=[pltpu.VMEM((CONV_HALO + tt, d_conv), F32),
                        pltpu.VMEM((V7X_SUBLANES - 1, CONV_HALO + tt - V7X_SUBLANES, d_conv), F32)],
        compiler_params=_params("parallel", "arbitrary"),
        name="conv_seq",
    )(pc, pc, *consts)


def _conv_step_kernel(pc_ref, st_ref, cw_ref, cb_ref, lng_ref, lnb_ref, beta_ref, o_ref, so_ref):
    d_conv = o_ref.shape[-1]
    n_hist = CONV_WIDTH - 1
    pc = pc_ref[...]
    u = pc[:, :, :d_conv] * jax.nn.sigmoid(pc[:, :, d_conv:])
    st = st_ref[...]
    c = (jnp.sum(st * cw_ref[0:n_hist, :], axis=1, keepdims=True)
         + u * cw_ref[n_hist:CONV_WIDTH, :] + cb_ref[...])
    c = _ln_rows(c, lng_ref[...], lnb_ref[...], LN_EPS)
    o_ref[...] = c * jax.nn.sigmoid(c) * beta_ref[...]
    so_ref[:, 0:n_hist - 1, :] = st[:, 1:, :]
    so_ref[:, n_hist - 1:n_hist, :] = u


def _conv_step(pc, state, cw, cb, ln_g, ln_b, beta, sb):
    n_s, _, two_d = pc.shape
    d_conv = two_d // 2
    consts = (cw, cb, ln_g, ln_b, beta)
    sspec = pl.BlockSpec((sb, CONV_WIDTH - 1, d_conv), lambda i: (i, 0, 0))
    return pl.pallas_call(
        _conv_step_kernel,
        out_shape=[jax.ShapeDtypeStruct((n_s, 1, d_conv), F32), jax.ShapeDtypeStruct(state.shape, F32)],
        grid=(n_s // sb,),
        in_specs=[pl.BlockSpec((sb, 1, two_d), lambda i: (i, 0, 0)), sspec] + [_const_spec(c) for c in consts],
        out_specs=[pl.BlockSpec((sb, 1, d_conv), lambda i: (i, 0, 0)), sspec],
        compiler_params=_params("parallel"),
        name="conv_step",
    )(pc, state, *consts)


def _attn_seq_kernel(q_ref, k_ref, v_ref, o_ref, *, scale):
    d = q_ref.shape[-1]
    dh = d // N_MEM_HEADS
    for h in range(N_MEM_HEADS):
        cols = slice(h * dh, (h + 1) * dh)
        k = k_ref[:, cols].astype(BF16)
        s = lax.dot_general(q_ref[:, cols], k, (((1,), (1,)), ((), ())), preferred_element_type=F32) * scale
        s = s - jnp.max(s, axis=-1, keepdims=True)
        e = jnp.exp(s)
        p = e / jnp.sum(e, axis=-1, keepdims=True)
        o_ref[:, cols] = _dot(p.astype(BF16), v_ref[:, cols].astype(BF16)).astype(o_ref.dtype)


def _attn_seq(q, mk, mv, batch, seq, tq):
    m, d = q.shape
    n_mem = mk.shape[1]
    nt = seq // tq
    kern = functools.partial(_attn_seq_kernel, scale=1.0 / math.sqrt(d // N_MEM_HEADS))
    kv_spec = pl.BlockSpec((None, n_mem, d), lambda b, i: (b, 0, 0))
    return pl.pallas_call(
        kern,
        out_shape=jax.ShapeDtypeStruct((m, d), BF16),
        grid=(batch, nt),
        in_specs=[pl.BlockSpec((tq, d), lambda b, i: (b * nt + i, 0)), kv_spec, kv_spec],
        out_specs=pl.BlockSpec((tq, d), lambda b, i: (b * nt + i, 0)),
        compiler_params=_params("parallel", "arbitrary"),
        name="attn_seq",
    )(q, mk, mv)


def _attn_step_kernel(q_ref, k_ref, v_ref, o_ref, *, scale):
    for i in range(q_ref.shape[0]):
        s = jnp.sum(k_ref[i] * q_ref[i], axis=-1, keepdims=True) * scale
        s = s - jnp.max(s, axis=0, keepdims=True)
        e = jnp.exp(s)
        p = e / jnp.sum(e, axis=0, keepdims=True)
        o_ref[i] = jnp.sum(p * v_ref[i], axis=0, keepdims=True)


def _attn_step(q, mk, mv, sb):
    n_s, _, n_h, dh = q.shape
    n_mem = mk.shape[1]
    kern = functools.partial(_attn_step_kernel, scale=1.0 / math.sqrt(dh))
    qspec = pl.BlockSpec((sb, 1, n_h, dh), lambda i: (i, 0, 0, 0))
    kvspec = pl.BlockSpec((sb, n_mem, n_h, dh), lambda i: (i, 0, 0, 0))
    return pl.pallas_call(
        kern,
        out_shape=jax.ShapeDtypeStruct(q.shape, F32),
        grid=(n_s // sb,),
        in_specs=[qspec, kvspec, kvspec],
        out_specs=qspec,
        compiler_params=_params("parallel"),
        name="attn_step",
    )(q, mk, mv)


def _head_selectors(d_rwkv, n_heads):
    assert n_heads <= V7X_LANES
    lane = np.arange(V7X_LANES)
    chan = np.arange(d_rwkv)
    gsel = chan[:, None] % n_heads == lane[None, :]
    gsum = lane[:, None] % n_heads == lane[None, :] % n_heads
    as_bf16 = lambda a: jnp.asarray(a.astype(np.float32), BF16)
    return as_bf16(gsel), as_bf16(gsel.T), as_bf16(gsum)


def _to_key_order(x, n_heads):
    lead = x.shape[:-1]
    return jnp.swapaxes(x.reshape(*lead, n_heads, HEAD_DIM), -1, -2).reshape(*lead, n_heads * HEAD_DIM)


def _from_key_order(x, n_heads):
    lead = x.shape[:-1]
    return jnp.swapaxes(x.reshape(*lead, HEAD_DIM, n_heads), -1, -2).reshape(*lead, n_heads * HEAD_DIM)


def _to_value_order(x, n_heads):
    lead = x.shape[:-1]
    nd = len(lead)
    sub = V7X_SUBLANES
    x = x.reshape(*lead, n_heads, HEAD_DIM // sub, sub)
    return x.transpose(*range(nd), nd + 2, nd + 1, nd).reshape(*lead, n_heads * HEAD_DIM)


def _from_value_order(x, n_heads):
    lead = x.shape[:-1]
    nd = len(lead)
    sub = V7X_SUBLANES
    x = x.reshape(*lead, sub, HEAD_DIM // sub, n_heads)
    return x.transpose(*range(nd), nd + 2, nd + 1, nd).reshape(*lead, n_heads * HEAD_DIM)


def kernel(x_prompt, x_sample, mem_prompt, state_shift, state_conv, state_wkv, cache_mem_k, cache_mem_v,
           ffn1_w1, ffn1_w3, ffn1_w2, ln1_g, ln1_b, w_in, mu_shift, w0, w2_decay, a0, a2_iclr, g2_gate,
           k_k, k_a, r_k, gn_g, gn_b, conv_w, conv_b, conv_ln_g, conv_ln_b, beta_rwkv, beta_conv, w_out,
           ln2_g, ln2_b, w_mq, w_mk, w_mv, w_mo, ln3_g, ln3_b, ffn2_w1, ffn2_w3, ffn2_w2, ln4_g, ln4_b):
    depth = ffn1_w1.shape[0]
    assert depth == 1
    batch, seq, d = x_prompt.shape
    n_s = x_sample.shape[0]
    assert x_sample.shape[1] == 1
    n_mem = mem_prompt.shape[1]
    d_rwkv = w0.shape[-1]
    d_conv = conv_b.shape[-1]
    n_heads = d_rwkv // HEAD_DIM
    n_rkv = 3 * d_rwkv
    shift_cols = mu_shift.shape[-1]
    alpha = (2.0 * depth) ** 0.25
    l = 0

    tok = functools.partial(_to_key_order, n_heads=n_heads)
    tov = functools.partial(_to_value_order, n_heads=n_heads)
    fromk = functools.partial(_from_key_order, n_heads=n_heads)
    fromv = functools.partial(_from_value_order, n_heads=n_heads)

    def rkv_to_kernel_order(x):
        return jnp.concatenate([tok(x[..., :d_rwkv]), tok(x[..., d_rwkv:2 * d_rwkv]), tov(x[..., 2 * d_rwkv:n_rkv])],
                               axis=-1)

    def rkv_from_kernel_order(x):
        return jnp.concatenate([fromk(x[..., :d_rwkv]), fromk(x[..., d_rwkv:2 * d_rwkv]),
                                fromv(x[..., 2 * d_rwkv:n_rkv])], axis=-1)

    w_rkv = rkv_to_kernel_order(w_in[l, :, :n_rkv])
    w_lo = w_in[l, :, n_rkv:shift_cols]
    w_cv = w_in[l, :, shift_cols:]
    mu_rkv = rkv_to_kernel_order(mu_shift[l:l + 1, :n_rkv])
    mu_lo = mu_shift[l:l + 1, n_rkv:]
    gsel, gselt, gsum = _head_selectors(d_rwkv, n_heads)
    prep_consts = (mu_rkv, mu_lo, tok(w0[l:l + 1]), tok(a0[l:l + 1]), tok(k_k[l:l + 1]), tok(k_a[l:l + 1]),
                   tok(r_k[l].reshape(1, d_rwkv)), tok(w2_decay[l]), tok(a2_iclr[l]), tov(g2_gate[l]), gsel, gselt)
    cw = conv_w[l].reshape(CONV_WIDTH, d_conv)
    conv_consts = (cw, conv_b[l:l + 1], conv_ln_g[l:l + 1], conv_ln_b[l:l + 1], beta_conv[l:l + 1])
    post_consts = (tov(gn_g[l:l + 1]), tov(gn_b[l:l + 1]), tov(beta_rwkv[l:l + 1]), gsel, gselt)
    w_out_top_b = _cast_bf16(tov(w_out[l, :d_rwkv].T).T, 512)
    w_out_b = _cast_bf16(w_out[l], 512)
    w_mo_b = _cast_bf16(w_mo[l], 512)

    def trunk_front(x, tm):
        x1, x1b = _ffn_ln(x, None, ffn1_w1[l], ffn1_w3[l], ffn1_w2[l], ln1_g[l:l + 1], ln1_b[l:l + 1], alpha, tm, 256)
        p_rkv = _mm(x1b, w_rkv, tm, 1024)
        p_lo = _mm(x1b, w_lo, tm, w_lo.shape[1])
        p_cv = _mm(x1b, w_cv, tm, 1024)
        return x1, p_rkv, p_lo, p_cv

    def trunk_back(x1, o_a, o_b, attend, tm):
        tm_ln = min(tm, 512)
        x2, x2b = _mm_res_ln([o_a, o_b], [(w_out_top_b, 0), (w_out_b, 1)], x1, ln2_g[l:l + 1], ln2_b[l:l + 1], alpha, tm_ln)
        q = _mm(x2b, w_mq[l], tm, 1024, BF16)
        att = attend(q)
        x3, x3b = _mm_res_ln([att], [(w_mo_b, 0)], x2, ln3_g[l:l + 1], ln3_b[l:l + 1], alpha, tm_ln)
        return _ffn_ln(x3, x3b, ffn2_w1[l], ffn2_w3[l], ffn2_w2[l], ln4_g[l:l + 1], ln4_b[l:l + 1], alpha, tm, 256)[0]

    m_p = batch * seq
    tm_p = 1024
    x1, p_rkv, p_lo, p_cv = trunk_front(x_prompt.reshape(m_p, d), tm_p)
    zeros_rkv = jnp.zeros((batch, 1, n_rkv), F32)
    zeros_lo = jnp.zeros((batch, 1, shift_cols - n_rkv), F32)
    sub = V7X_SUBLANES
    r, w, k, kk, bb, v, g, bonus = _rwkv_prep_seq(p_rkv, p_lo, zeros_rkv, zeros_lo, prep_consts, batch, seq, 256)
    bt = lambda t: t.reshape(batch, seq // sub, sub, d_rwkv)
    y4, s_t = _rwkv_seq(bt(r), bt(w), bt(k), bt(kk), bt(bb), bt(v), gsum, n_heads, 64)
    wkv_p = s_t.reshape(HEAD_DIM // sub, batch, HEAD_DIM, sub, n_heads).transpose(1, 4, 2, 0, 3).reshape(
        batch, n_heads, HEAD_DIM, HEAD_DIM)
    o_a = _rwkv_post(y4.reshape(m_p, d_rwkv), g, bonus, *post_consts, 256)
    o_b, conv_p = _conv_seq(p_cv, *conv_consts, batch, seq, 256)

    mem2 = mem_prompt.reshape(batch * n_mem, d)
    mk_p = _mm(mem2, w_mk[l], batch * n_mem, 512)
    mv_p = _mm(mem2, w_mv[l], batch * n_mem, 512)
    attend_p = lambda q: _attn_seq(q, mk_p.reshape(batch, n_mem, d), mv_p.reshape(batch, n_mem, d), batch, seq, 512)
    y_prompt = trunk_back(x1, o_a, o_b, attend_p, tm_p).reshape(batch, seq, d)
    shift_p = jnp.concatenate([rkv_from_kernel_order(p_rkv.reshape(batch, seq, n_rkv)[:, -1]),
                               p_lo.reshape(batch, seq, -1)[:, -1]], axis=-1)
    dh = d // N_MEM_HEADS

    xs1, ps_rkv, ps_lo, ps_cv = trunk_front(x_sample.reshape(n_s, d), n_s)
    prev = state_shift[l]
    rs, ws, ks, vs, kks, bs, gs, bonus_s = _rwkv_prep_step(ps_rkv, rkv_to_kernel_order(prev[:, :n_rkv]), ps_lo,
                                                          prev[:, n_rkv:], prep_consts, n_s)
    hdk = lambda t: fromk(t).reshape(n_s, n_heads, 1, HEAD_DIM)
    ys, wkv_s = _rwkv_step(state_wkv[l], hdk(rs), hdk(ws), hdk(ks), hdk(kks), hdk(bs),
                           fromv(vs).reshape(n_s, n_heads, 1, HEAD_DIM), 8)
    ys = tov(ys.reshape(n_s, d_rwkv))
    o_as = _rwkv_post(ys, gs, bonus_s, *post_consts, n_s)
    o_bs, conv_s = _conv_step(ps_cv.reshape(n_s, 1, 2 * d_conv), state_conv[l], *conv_consts, 8)
    o_bs = o_bs.reshape(n_s, d_conv).astype(BF16)
    attend_s = lambda q: _attn_step(q.astype(F32).reshape(n_s, 1, N_MEM_HEADS, dh), cache_mem_k[l], cache_mem_v[l],
                                    2).reshape(n_s, d).astype(BF16)
    y_sample = trunk_back(xs1, o_as, o_bs, attend_s, n_s).reshape(n_s, 1, d)
    shift_s = jnp.concatenate([rkv_from_kernel_order(ps_rkv), ps_lo], axis=-1)

    return (y_prompt, y_sample,
            shift_p[None], conv_p[None], wkv_p[None],
            mk_p.reshape(1, batch, n_mem, N_MEM_HEADS, dh), mv_p.reshape(1, batch, n_mem, N_MEM_HEADS, dh),
            shift_s[None], conv_s[None], wkv_s[None])
```

```python
import functools
import math

import jax
import jax.numpy as jnp
import numpy as np
from jax import lax
from jax.experimental import pallas as pl
from jax.experimental.pallas import tpu as pltpu

F32 = jnp.float32
BF16 = jnp.bfloat16

V7X_LANES = 128
V7X_SUBLANES = 8
V7X_VMEM_LIMIT_BYTES = 56 * 1024 * 1024

HEAD_DIM = 64
CONV_WIDTH = 31
DECAY_LORA = 64
ICLR_LORA = 64
GATE_LORA = 160
N_MEM_HEADS = 4
LN_EPS = 1e-5
GN_EPS = 64e-5


def _params(*sem):
    return pltpu.CompilerParams(dimension_semantics=sem, vmem_limit_bytes=V7X_VMEM_LIMIT_BYTES)


def _resident(block_shape, index_map):
    return pl.BlockSpec(block_shape, index_map, pipeline_mode=pl.Buffered(1))


def _ln_rows(y, g, b, eps):
    mu = jnp.mean(y, axis=-1, keepdims=True)
    yc = y - mu
    var = jnp.mean(yc * yc, axis=-1, keepdims=True)
    return yc * lax.rsqrt(var + eps) * g + b


def _dot(a, b):
    return jnp.dot(a, b, preferred_element_type=F32)


def _split_dot(x, g):
    hi = x.astype(BF16)
    mid = (x - hi.astype(F32)).astype(BF16)
    return _dot(hi, g) + _dot(mid, g)


def _head_sum(x, gsel, gselt):
    return _split_dot(_split_dot(x, gsel), gselt)


def _ffn_ln_kernel(x_ref, *refs, alpha, n_chunk, row_chunk, cast_x):
    if cast_x:
        w1_ref, w3_ref, w2_ref, g_ref, b_ref, o_ref, ob_ref, xb_ref = refs
    else:
        xb_ref, w1_ref, w3_ref, w2_ref, g_ref, b_ref, o_ref, ob_ref = refs
    j = pl.program_id(1)
    tm, d = o_ref.shape

    @pl.when(j == 0)
    def _():
        o_ref[...] = jnp.zeros_like(o_ref)
        if cast_x:
            xb_ref[...] = x_ref[...].astype(BF16)

    xb = xb_ref[...]
    h1 = _dot(xb, w1_ref[...].astype(BF16))
    h3 = _dot(xb, w3_ref[...].astype(BF16))
    hb = (h1 * jax.nn.sigmoid(h1) * h3).astype(BF16)
    w2b = w2_ref[...].astype(BF16)
    cw = d // n_chunk
    for c in range(n_chunk):
        o_ref[:, c * cw:(c + 1) * cw] += _dot(hb, w2b[:, c * cw:(c + 1) * cw])

    @pl.when(j == pl.num_programs(1) - 1)
    def _():
        def body(r, carry):
            rows = pl.ds(pl.multiple_of(r * row_chunk, row_chunk), row_chunk)
            y = alpha * x_ref[rows, :] + 0.5 * o_ref[rows, :]
            y = _ln_rows(y, g_ref[...], b_ref[...], LN_EPS)
            o_ref[rows, :] = y
            ob_ref[rows, :] = y.astype(BF16)
            return carry

        lax.fori_loop(0, tm // row_chunk, body, 0)


def _ffn_ln(x, xb, w1, w3, w2, g, b, alpha, tm, tf):
    m, d = x.shape
    f = w1.shape[1]
    cast_x = xb is None
    kern = functools.partial(_ffn_ln_kernel, alpha=alpha, n_chunk=4, row_chunk=min(tm, 128), cast_x=cast_x)
    row_tile = lambda i, j: (i, 0)
    return pl.pallas_call(
        kern,
        out_shape=[jax.ShapeDtypeStruct((m, d), F32), jax.ShapeDtypeStruct((m, d), BF16)],
        grid=(m // tm, f // tf),
        in_specs=[pl.BlockSpec((tm, d), row_tile)] + ([] if cast_x else [_resident((tm, d), row_tile)]) + [
            pl.BlockSpec((d, tf), lambda i, j: (0, j)),
            pl.BlockSpec((d, tf), lambda i, j: (0, j)),
            pl.BlockSpec((tf, d), lambda i, j: (j, 0)),
            pl.BlockSpec((1, d), lambda i, j: (0, 0)),
            pl.BlockSpec((1, d), lambda i, j: (0, 0)),
        ],
        out_specs=[pl.BlockSpec((tm, d), row_tile), _resident((tm, d), row_tile)],
        scratch_shapes=[pltpu.VMEM((tm, d), BF16)] if cast_x else [],
        compiler_params=_params("parallel", "arbitrary"),
        name="ffn_ln",
    )(*((x,) if cast_x else (x, xb)), w1, w3, w2, g, b)


def _mm_kernel(x_ref, w_ref, o_ref):
    o_ref[...] = _dot(x_ref[...].astype(BF16), w_ref[...].astype(BF16)).astype(o_ref.dtype)


def _mm(x, w, tm, tn, out_dtype=F32):
    m, k = x.shape
    n = w.shape[1]
    return pl.pallas_call(
        _mm_kernel,
        out_shape=jax.ShapeDtypeStruct((m, n), out_dtype),
        grid=(m // tm, n // tn),
        in_specs=[
            pl.BlockSpec((tm, k), lambda i, j: (i, 0)),
            pl.BlockSpec((k, tn), lambda i, j: (0, j)),
        ],
        out_specs=pl.BlockSpec((tm, tn), lambda i, j: (i, j)),
        compiler_params=_params("parallel", "arbitrary"),
        name="mm",
    )(x, w)


def _cast_kernel(x_ref, o_ref):
    o_ref[...] = x_ref[...].astype(o_ref.dtype)


def _cast_bf16(w, tr):
    rows, cols = w.shape
    return pl.pallas_call(
        _cast_kernel,
        out_shape=jax.ShapeDtypeStruct(w.shape, BF16),
        grid=(rows // tr,),
        in_specs=[pl.BlockSpec((tr, cols), lambda i: (i, 0))],
        out_specs=pl.BlockSpec((tr, cols), lambda i: (i, 0)),
        compiler_params=_params("parallel"),
        name="cast_bf16",
    )(w)


def _mm_res_ln_kernel(*refs, n_a, alpha, row_chunk):
    a_refs = refs[:n_a]
    w_refs = refs[n_a:2 * n_a]
    res_ref, g_ref, b_ref, o_ref, ob_ref, acc_ref = refs[2 * n_a:]
    tm, d = acc_ref.shape

    acc = _dot(a_refs[0][...], w_refs[0][...])
    for a_ref, w_ref in zip(a_refs[1:], w_refs[1:]):
        acc += _dot(a_ref[...], w_ref[...])
    acc_ref[...] = acc

    def body(r, carry):
        rows = pl.ds(pl.multiple_of(r * row_chunk, row_chunk), row_chunk)
        y = _ln_rows(alpha * res_ref[rows, :] + acc_ref[rows, :], g_ref[...], b_ref[...], LN_EPS)
        o_ref[rows, :] = y
        ob_ref[rows, :] = y.astype(BF16)
        return carry

    lax.fori_loop(0, tm // row_chunk, body, 0)


def _mm_res_ln(a_list, w_list, res, g, b, alpha, tm):
    m, d = res.shape
    n_a = len(a_list)
    kern = functools.partial(_mm_res_ln_kernel, n_a=n_a, alpha=alpha, row_chunk=min(tm, 64))
    row_tile = lambda i: (i, 0)
    in_specs = [pl.BlockSpec((tm, a.shape[1]), row_tile) for a in a_list]
    in_specs += [_resident((a.shape[1], d), functools.partial(lambda i, r: (r, 0), r=r))
                 for a, (_, r) in zip(a_list, w_list)]
    in_specs += [
        pl.BlockSpec((tm, d), row_tile),
        pl.BlockSpec((1, d), lambda i: (0, 0)),
        pl.BlockSpec((1, d), lambda i: (0, 0)),
    ]
    return pl.pallas_call(
        kern,
        out_shape=[jax.ShapeDtypeStruct((m, d), F32), jax.ShapeDtypeStruct((m, d), BF16)],
        grid=(m // tm,),
        in_specs=in_specs,
        out_specs=[pl.BlockSpec((tm, d), row_tile), pl.BlockSpec((tm, d), row_tile)],
        scratch_shapes=[pltpu.VMEM((tm, d), F32)],
        compiler_params=_params("parallel"),
        name="mm_res_ln",
    )(*a_list, *[w for w, _ in w_list], res, g, b)


def _shift_rows(cur, first_row):
    rolled = pltpu.roll(cur, 1, 0)
    row_id = lax.broadcasted_iota(jnp.int32, cur.shape, 0)
    return jnp.where(row_id == 0, first_row, rolled)


def _rwkv_prep_math(cur_rkv, prev_rkv, cur_lo, prev_lo, mu_rkv, mu_lo, w0, a0, k_k, k_a, r_k,
                    w2d, a2, g2, gsel, gselt, d_rwkv):
    pm = cur_rkv + mu_rkv * (prev_rkv - cur_rkv)
    pl_ = cur_lo + mu_lo * (prev_lo - cur_lo)
    r = pm[:, :d_rwkv]
    k = pm[:, d_rwkv:2 * d_rwkv]
    v = pm[:, 2 * d_rwkv:]
    wd = pl_[:, :DECAY_LORA]
    ad = pl_[:, DECAY_LORA:DECAY_LORA + ICLR_LORA]
    gd = pl_[:, DECAY_LORA + ICLR_LORA:]
    z = w0 + _dot(jnp.tanh(wd).astype(BF16), w2d.astype(BF16))
    w_log = -jax.nn.softplus(-z) - 0.5
    decay = jnp.exp(-jnp.exp(w_log))
    a = jax.nn.sigmoid(a0 + _dot(ad.astype(BF16), a2.astype(BF16)))
    g = _dot(jax.nn.sigmoid(gd).astype(BF16), g2.astype(BF16))
    kk = k * k_k
    norm = jnp.sqrt(_head_sum(kk * kk, gsel, gselt))
    kk = kk / jnp.maximum(norm, 1e-12)
    kmod = k * (1.0 + (a - 1.0) * k_a)
    bonus = _head_sum(r * kmod * r_k, gsel, gselt) * v
    return r, decay, kmod, v, kk, kk * a, g, bonus


def _rwkv_prep_seq_kernel(cur_rkv_ref, prev8_rkv_ref, cur_lo_ref, prev8_lo_ref, sp_rkv_ref, sp_lo_ref,
                          mu_rkv_ref, mu_lo_ref, w0_ref, a0_ref, kk_ref, ka_ref, rk_ref,
                          w2d_ref, a2_ref, g2_ref, gsel_ref, gselt_ref,
                          r_o, w_o, k_o, kk_o, b_o, v_o, g_o, bonus_o):
    first = pl.program_id(1) == 0
    cur_rkv = cur_rkv_ref[...]
    cur_lo = cur_lo_ref[...]
    last = V7X_SUBLANES - 1
    first_rkv = jnp.where(first, sp_rkv_ref[...], prev8_rkv_ref[last:last + 1, :])
    first_lo = jnp.where(first, sp_lo_ref[...], prev8_lo_ref[last:last + 1, :])
    r, w, k, v, kk, b, g, bonus = _rwkv_prep_math(
        cur_rkv, _shift_rows(cur_rkv, first_rkv), cur_lo, _shift_rows(cur_lo, first_lo),
        mu_rkv_ref[...], mu_lo_ref[...], w0_ref[...], a0_ref[...], kk_ref[...], ka_ref[...], rk_ref[...],
        w2d_ref[...], a2_ref[...], g2_ref[...], gsel_ref[...], gselt_ref[...], g_o.shape[-1])
    for o_ref, val in zip((r_o, w_o, k_o, kk_o, b_o, v_o, g_o, bonus_o), (r, w, k, kk, b, v, g, bonus)):
        o_ref[...] = val


def _rwkv_prep_step_kernel(cur_rkv_ref, prev_rkv_ref, cur_lo_ref, prev_lo_ref,
                           mu_rkv_ref, mu_lo_ref, w0_ref, a0_ref, kk_ref, ka_ref, rk_ref,
                           w2d_ref, a2_ref, g2_ref, gsel_ref, gselt_ref,
                           r_o, w_o, k_o, v_o, kk_o, b_o, g_o, bonus_o):
    outs = _rwkv_prep_math(
        cur_rkv_ref[...], prev_rkv_ref[...], cur_lo_ref[...], prev_lo_ref[...],
        mu_rkv_ref[...], mu_lo_ref[...], w0_ref[...], a0_ref[...], kk_ref[...], ka_ref[...], rk_ref[...],
        w2d_ref[...], a2_ref[...], g2_ref[...], gsel_ref[...], gselt_ref[...], r_o.shape[-1])
    for o_ref, val in zip((r_o, w_o, k_o, v_o, kk_o, b_o, g_o, bonus_o), outs):
        o_ref[...] = val


def _const_spec(arr):
    nd = arr.ndim
    return pl.BlockSpec(arr.shape, lambda *_: (0,) * nd)


def _rwkv_prep_seq(p_rkv, p_lo, sp_rkv, sp_lo, consts, batch, seq, tt):
    m, n_rkv = p_rkv.shape
    n_lo = p_lo.shape[1]
    d_rwkv = n_rkv // 3
    nt = seq // tt
    sub = V7X_SUBLANES

    def cur_map(b, i):
        return (b * nt + i, 0)

    def prev_map(b, i):
        return (jnp.maximum((b * seq + i * tt) // sub - 1, 0), 0)

    out_shape = [jax.ShapeDtypeStruct((m, d_rwkv), F32)] * 8
    out_specs = [pl.BlockSpec((tt, d_rwkv), cur_map)] * 8
    return pl.pallas_call(
        _rwkv_prep_seq_kernel,
        out_shape=out_shape,
        grid=(batch, nt),
        in_specs=[
            pl.BlockSpec((tt, n_rkv), cur_map),
            pl.BlockSpec((sub, n_rkv), prev_map),
            pl.BlockSpec((tt, n_lo), cur_map),
            pl.BlockSpec((sub, n_lo), prev_map),
            pl.BlockSpec((None, 1, n_rkv), lambda b, i: (b, 0, 0)),
            pl.BlockSpec((None, 1, n_lo), lambda b, i: (b, 0, 0)),
        ] + [_const_spec(c) for c in consts],
        out_specs=out_specs,
        compiler_params=_params("parallel", "arbitrary"),
        name="rwkv_prep_seq",
    )(p_rkv, p_rkv, p_lo, p_lo, sp_rkv, sp_lo, *consts)


def _rwkv_prep_step(p_rkv, prev_rkv, p_lo, prev_lo, consts, tb):
    m, n_rkv = p_rkv.shape
    n_lo = p_lo.shape[1]
    d_rwkv = n_rkv // 3
    out_sds = jax.ShapeDtypeStruct((m, d_rwkv), F32)
    row = lambda i: (i, 0)
    return pl.pallas_call(
        _rwkv_prep_step_kernel,
        out_shape=[out_sds] * 8,
        grid=(m // tb,),
        in_specs=[
            pl.BlockSpec((tb, n_rkv), row), pl.BlockSpec((tb, n_rkv), row),
            pl.BlockSpec((tb, n_lo), row), pl.BlockSpec((tb, n_lo), row),
        ] + [_const_spec(c) for c in consts],
        out_specs=[pl.BlockSpec((tb, d_rwkv), row)] * 8,
        compiler_params=_params("parallel"),
        name="rwkv_prep_step",
    )(p_rkv, prev_rkv, p_lo, prev_lo, *consts)


def _split_dot_stacked(x, g):
    hi = x.astype(BF16)
    mid = (x - hi.astype(F32)).astype(BF16)
    both = _dot(jnp.concatenate([hi, mid], axis=0), g)
    return both[:x.shape[0]] + both[x.shape[0]:]


def _rwkv_seq_kernel(r_ref, w_ref, k_ref, kk_ref, b_ref, v_ref, gsum_ref, y_ref, s_ref, *, group, n_chain):
    nb, n_t8, sub, _ = r_ref.shape
    n_kq, _, lanes = s_ref.shape
    n_v = s_ref.shape[1] // nb
    n_vq = n_v // sub
    nbc = nb // n_chain

    @pl.when(pl.program_id(0) == 0)
    def _():
        s_ref[...] = jnp.zeros_like(s_ref)

    lane_grp = lax.broadcasted_iota(jnp.int32, (sub, lanes), 1) // group
    sub_id = lax.broadcasted_iota(jnp.int32, (sub, lanes), 0)
    gsum = gsum_ref[...]

    def update_state(t8, j, c):
        seqs = range(c * nbc, (c + 1) * nbc)
        srows = slice(c * nbc * n_v, (c + 1) * nbc * n_v)

        def row(ref, b, cidx):
            return ref[b, t8, pl.ds(j, 1), cidx * lanes:(cidx + 1) * lanes]

        def rows(ref, kq):
            return jnp.concatenate([jnp.broadcast_to(row(ref, b, kq), (n_v, lanes)) for b in seqs], axis=0)

        def v_tile(b):
            vt = jnp.broadcast_to(row(v_ref, b, 0), (sub, lanes))
            for v8 in range(1, sub):
                vt = jnp.where(sub_id == v8, row(v_ref, b, v8), vt)
            return vt

        v_all = _split_dot_stacked(jnp.concatenate(
            [jnp.where(lane_grp == vq, vt, 0.0) for vt in map(v_tile, seqs) for vq in range(n_vq)], axis=0), gsum)
        sa = s_ref[0, srows, :] * rows(kk_ref, 0)
        for kq in range(1, n_kq):
            sa += s_ref[kq, srows, :] * rows(kk_ref, kq)
        sa = _split_dot_stacked(sa, gsum)
        y = None
        for kq in range(n_kq):
            s_new = s_ref[kq, srows, :] * rows(w_ref, kq) - rows(b_ref, kq) * sa + rows(k_ref, kq) * v_all
            s_ref[kq, srows, :] = s_new
            term = s_new * rows(r_ref, kq)
            y = term if y is None else y + term
        return y

    def emit_y(t8, j, c, y):
        y = _split_dot_stacked(y, gsum)
        for i in range(nbc):
            yc = jnp.zeros((sub, lanes), F32)
            for vq in range(n_vq):
                lo = (i * n_vq + vq) * sub
                yc = jnp.where(lane_grp == vq, y[lo:lo + sub, :], yc)
            for v8 in range(sub):
                y_ref[c * nbc + i, t8, pl.ds(j, 1), v8 * lanes:(v8 + 1) * lanes] = yc[v8:v8 + 1, :]

    def tile(t8, y_prev):
        for j in range(sub):
            y_cur = tuple(update_state(t8, j, c) for c in range(n_chain))
            if y_prev is not None:
                for c in range(n_chain):
                    emit_y(t8 if j else t8 - 1, (j - 1) % sub, c, y_prev[c])
            y_prev = y_cur
        return y_prev

    y_last = lax.fori_loop(1, n_t8, tile, tile(0, None))
    for c in range(n_chain):
        emit_y(n_t8 - 1, sub - 1, c, y_last[c])


def _rwkv_seq(r, w, k, kk, b, v, gsum, n_heads, tb):
    batch, seq8, sub, d_rwkv = r.shape
    lanes = V7X_LANES
    n_kq = d_rwkv // lanes
    assert sub == V7X_SUBLANES and lanes == sub * n_heads and d_rwkv == HEAD_DIM * n_heads and HEAD_DIM == sub * sub
    n_chain = 2 if batch % 2 == 0 else 1
    spec = pl.BlockSpec((batch, tb // sub, sub, d_rwkv), lambda i: (0, i, 0, 0))
    state_shape = (n_kq, batch * HEAD_DIM, lanes)
    seq = seq8 * sub
    return pl.pallas_call(
        functools.partial(_rwkv_seq_kernel, group=n_heads, n_chain=n_chain),
        out_shape=[jax.ShapeDtypeStruct(r.shape, F32),
                   jax.ShapeDtypeStruct(state_shape, F32)],
        grid=(seq // tb,),
        in_specs=[spec] * 6 + [_const_spec(gsum)],
        out_specs=[spec, pl.BlockSpec(state_shape, lambda i: (0, 0, 0))],
        compiler_params=_params("arbitrary"),
        name="rwkv_seq",
    )(r, w, k, kk, b, v, gsum)


def _rwkv_step_kernel(s_ref, r_ref, w_ref, k_ref, kk_ref, b_ref, v_ref, y_ref, so_ref):
    n = s_ref.shape[-1]
    eye = (lax.broadcasted_iota(jnp.int32, (n, n), 0) == lax.broadcasted_iota(jnp.int32, (n, n), 1)).astype(F32)
    s = s_ref[...]
    v_col = jnp.sum(eye * v_ref[...], axis=-1, keepdims=True)
    sa = jnp.sum(s * kk_ref[...], axis=-1, keepdims=True)
    s_new = s * w_ref[...] - sa * b_ref[...] + v_col * k_ref[...]
    so_ref[...] = s_new
    y_col = jnp.sum(s_new * r_ref[...], axis=-1, keepdims=True)
    y_ref[...] = jnp.sum(eye * y_col, axis=-2, keepdims=True)


def _rwkv_step(state, r, w, k, kk, b, v, sb):
    n_s, n_h, n, _ = state.shape
    sspec = pl.BlockSpec((sb, n_h, n, n), lambda i: (i, 0, 0, 0))
    vspec = pl.BlockSpec((sb, n_h, 1, n), lambda i: (i, 0, 0, 0))
    return pl.pallas_call(
        _rwkv_step_kernel,
        out_shape=[jax.ShapeDtypeStruct((n_s, n_h, 1, n), F32), jax.ShapeDtypeStruct(state.shape, F32)],
        grid=(n_s // sb,),
        in_specs=[sspec] + [vspec] * 6,
        out_specs=[vspec, sspec],
        compiler_params=_params("parallel"),
        name="rwkv_step",
    )(state, r, w, k, kk, b, v)


def _rwkv_post_kernel(y_ref, g_ref, bonus_ref, gng_ref, gnb_ref, beta_ref, gsel_ref, gselt_ref, o_ref):
    y = y_ref[...]
    gsel = gsel_ref[...]
    gselt = gselt_ref[...]
    mu = _head_sum(y, gsel, gselt) * (1.0 / HEAD_DIM)
    yc = y - mu
    var = _head_sum(yc * yc, gsel, gselt) * (1.0 / HEAD_DIM)
    yn = yc * lax.rsqrt(var + GN_EPS) * gng_ref[...] + gnb_ref[...]
    o_ref[...] = ((yn + bonus_ref[...]) * g_ref[...] * beta_ref[...]).astype(o_ref.dtype)


def _rwkv_post(y, g, bonus, gn_g, gn_b, beta, gsel, gselt, tt):
    m, d = y.shape
    row = pl.BlockSpec((tt, d), lambda i: (i, 0))
    consts = (gn_g, gn_b, beta, gsel, gselt)
    return pl.pallas_call(
        _rwkv_post_kernel,
        out_shape=jax.ShapeDtypeStruct((m, d), BF16),
        grid=(m // tt,),
        in_specs=[row, row, row] + [_const_spec(c) for c in consts],
        out_specs=row,
        compiler_params=_params("parallel"),
        name="rwkv_post",
    )(y, g, bonus, *consts)


CONV_HALO = 32


def _glu(pc, d_conv):
    return pc[:, :d_conv] * jax.nn.sigmoid(pc[:, d_conv:])


def _conv_seq_kernel(pc_ref, halo_ref, cw_ref, cb_ref, lng_ref, lnb_ref, beta_ref, o_ref, st_ref, ubuf_ref,
                     ushift_ref, *, row_chunk):
    tt, d_conv = o_ref.shape
    i = pl.program_id(1)
    sub = V7X_SUBLANES
    n_hist = CONV_WIDTH - 1
    pad = CONV_HALO - n_hist
    u_halo = _glu(halo_ref[...], d_conv)
    ubuf_ref[0:CONV_HALO, :] = jnp.where(i == 0, jnp.zeros_like(u_halo), u_halo)
    ubuf_ref[CONV_HALO:, :] = _glu(pc_ref[...], d_conv)
    n_shift_rows = ushift_ref.shape[1]
    for s in range(1, sub):
        ushift_ref[s - 1] = ubuf_ref[s:s + n_shift_rows, :]

    def tap(base, w):
        off = pad + w
        q, s = off // sub, off % sub
        rows = slice(base + q * sub, base + q * sub + row_chunk)
        return (ubuf_ref[rows, :] if s == 0 else ushift_ref[s - 1, rows, :]) * cw_ref[w:w + 1, :]

    for base in range(0, tt, row_chunk):
        acc = tap(base, 0)
        for w in range(1, CONV_WIDTH):
            acc += tap(base, w)
        c = _ln_rows(acc + cb_ref[...], lng_ref[...], lnb_ref[...], LN_EPS)
        o_ref[base:base + row_chunk, :] = (c * jax.nn.sigmoid(c) * beta_ref[...]).astype(o_ref.dtype)

    @pl.when(i == pl.num_programs(1) - 1)
    def _():
        st_ref[...] = ubuf_ref[CONV_HALO + tt - n_hist:, :]


def _conv_seq(pc, cw, cb, ln_g, ln_b, beta, batch, seq, tt):
    m, two_d = pc.shape
    d_conv = two_d // 2
    nt = seq // tt
    hb = tt // CONV_HALO
    consts = (cw, cb, ln_g, ln_b, beta)
    kern = functools.partial(_conv_seq_kernel, row_chunk=16)
    return pl.pallas_call(
        kern,
        out_shape=[jax.ShapeDtypeStruct((m, d_conv), BF16),
                   jax.ShapeDtypeStruct((batch, CONV_WIDTH - 1, d_conv), F32)],
        grid=(batch, nt),
        in_specs=[
            pl.BlockSpec((tt, two_d), lambda b, i: (b * nt + i, 0)),
            pl.BlockSpec((CONV_HALO, two_d), lambda b, i: (jnp.maximum((b * nt + i) * hb - 1, 0), 0)),
        ] + [_const_spec(c) for c in consts],
        out_specs=[pl.BlockSpec((tt, d_conv), lambda b, i: (b * nt + i, 0)),
                   pl.BlockSpec((None, CONV_WIDTH - 1, d_conv), lambda b, i: (b, 0, 0))],
        scratch_shapes=[pltpu.VMEM((CONV_HALO + tt, d_conv), F32),
                        pltpu.VMEM((V7X_SUBLANES - 1, CONV_HALO + tt - V7X_SUBLANES, d_conv), F32)],
        compiler_params=_params("parallel", "arbitrary"),
        name="conv_seq",
    )(pc, pc, *consts)


def _conv_step_kernel(pc_ref, st_ref, cw_ref, cb_ref, lng_ref, lnb_ref, beta_ref, o_ref, so_ref):
    d_conv = o_ref.shape[-1]
    n_hist = CONV_WIDTH - 1
    pc = pc_ref[...]
    u = pc[:, :, :d_conv] * jax.nn.sigmoid(pc[:, :, d_conv:])
    st = st_ref[...]
    c = (jnp.sum(st * cw_ref[0:n_hist, :], axis=1, keepdims=True)
         + u * cw_ref[n_hist:CONV_WIDTH, :] + cb_ref[...])
    c = _ln_rows(c, lng_ref[...], lnb_ref[...], LN_EPS)
    o_ref[...] = c * jax.nn.sigmoid(c) * beta_ref[...]
    so_ref[:, 0:n_hist - 1, :] = st[:, 1:, :]
    so_ref[:, n_hist - 1:n_hist, :] = u


def _conv_step(pc, state, cw, cb, ln_g, ln_b, beta, sb):
    n_s, _, two_d = pc.shape
    d_conv = two_d // 2
    consts = (cw, cb, ln_g, ln_b, beta)
    sspec = pl.BlockSpec((sb, CONV_WIDTH - 1, d_conv), lambda i: (i, 0, 0))
    return pl.pallas_call(
        _conv_step_kernel,
        out_shape=[jax.ShapeDtypeStruct((n_s, 1, d_conv), F32), jax.ShapeDtypeStruct(state.shape, F32)],
        grid=(n_s // sb,),
        in_specs=[pl.BlockSpec((sb, 1, two_d), lambda i: (i, 0, 0)), sspec] + [_const_spec(c) for c in consts],
        out_specs=[pl.BlockSpec((sb, 1, d_conv), lambda i: (i, 0, 0)), sspec],
        compiler_params=_params("parallel"),
        name="conv_step",
    )(pc, state, *consts)


def _attn_seq_kernel(q_ref, k_ref, v_ref, o_ref, *, scale):
    d = q_ref.shape[-1]
    dh = d // N_MEM_HEADS
    for h in range(N_MEM_HEADS):
        cols = slice(h * dh, (h + 1) * dh)
        k = k_ref[:, cols].astype(BF16)
        s = lax.dot_general(q_ref[:, cols], k, (((1,), (1,)), ((), ())), preferred_element_type=F32) * scale
        s = s - jnp.max(s, axis=-1, keepdims=True)
        e = jnp.exp(s)
        p = e / jnp.sum(e, axis=-1, keepdims=True)
        o_ref[:, cols] = _dot(p.astype(BF16), v_ref[:, cols].astype(BF16)).astype(o_ref.dtype)


def _attn_seq(q, mk, mv, batch, seq, tq):
    m, d = q.shape
    n_mem = mk.shape[1]
    nt = seq // tq
    kern = functools.partial(_attn_seq_kernel, scale=1.0 / math.sqrt(d // N_MEM_HEADS))
    kv_spec = pl.BlockSpec((None, n_mem, d), lambda b, i: (b, 0, 0))
    return pl.pallas_call(
        kern,
        out_shape=jax.ShapeDtypeStruct((m, d), BF16),
        grid=(batch, nt),
        in_specs=[pl.BlockSpec((tq, d), lambda b, i: (b * nt + i, 0)), kv_spec, kv_spec],
        out_specs=pl.BlockSpec((tq, d), lambda b, i: (b * nt + i, 0)),
        compiler_params=_params("parallel", "arbitrary"),
        name="attn_seq",
    )(q, mk, mv)


def _attn_step_kernel(q_ref, k_ref, v_ref, o_ref, *, scale):
    for i in range(q_ref.shape[0]):
        s = jnp.sum(k_ref[i] * q_ref[i], axis=-1, keepdims=True) * scale
        s = s - jnp.max(s, axis=0, keepdims=True)
        e = jnp.exp(s)
        p = e / jnp.sum(e, axis=0, keepdims=True)
        n_mem, n_h, dh = v_ref.shape[1:]
        p_lanes = jnp.broadcast_to(p, (n_mem, n_h, V7X_LANES))
        for c in range(dh // V7X_LANES):
            cols = slice(c * V7X_LANES, (c + 1) * V7X_LANES)
            o_ref[i, :, :, cols] = jnp.sum(p_lanes * v_ref[i, :, :, cols], axis=0, keepdims=True)


def _attn_step(q, mk, mv, sb):
    n_s, _, n_h, dh = q.shape
    n_mem = mk.shape[1]
    kern = functools.partial(_attn_step_kernel, scale=1.0 / math.sqrt(dh))
    qspec = pl.BlockSpec((sb, 1, n_h, dh), lambda i: (i, 0, 0, 0))
    kvspec = pl.BlockSpec((sb, n_mem, n_h, dh), lambda i: (i, 0, 0, 0))
    return pl.pallas_call(
        kern,
        out_shape=jax.ShapeDtypeStruct(q.shape, F32),
        grid=(n_s // sb,),
        in_specs=[qspec, kvspec, kvspec],
        out_specs=qspec,
        compiler_params=_params("parallel"),
        name="attn_step",
    )(q, mk, mv)


def _head_selectors(d_rwkv, n_heads):
    assert n_heads <= V7X_LANES
    lane = np.arange(V7X_LANES)
    chan = np.arange(d_rwkv)
    gsel = chan[:, None] % n_heads == lane[None, :]
    gsum = lane[:, None] % n_heads == lane[None, :] % n_heads
    as_bf16 = lambda a: jnp.asarray(a.astype(np.float32), BF16)
    return as_bf16(gsel), as_bf16(gsel.T), as_bf16(gsum)


def _to_key_order(x, n_heads):
    lead = x.shape[:-1]
    return jnp.swapaxes(x.reshape(*lead, n_heads, HEAD_DIM), -1, -2).reshape(*lead, n_heads * HEAD_DIM)


def _from_key_order(x, n_heads):
    lead = x.shape[:-1]
    return jnp.swapaxes(x.reshape(*lead, HEAD_DIM, n_heads), -1, -2).reshape(*lead, n_heads * HEAD_DIM)


def _to_value_order(x, n_heads):
    lead = x.shape[:-1]
    nd = len(lead)
    sub = V7X_SUBLANES
    x = x.reshape(*lead, n_heads, HEAD_DIM // sub, sub)
    return x.transpose(*range(nd), nd + 2, nd + 1, nd).reshape(*lead, n_heads * HEAD_DIM)


def _from_value_order(x, n_heads):
    lead = x.shape[:-1]
    nd = len(lead)
    sub = V7X_SUBLANES
    x = x.reshape(*lead, sub, HEAD_DIM // sub, n_heads)
    return x.transpose(*range(nd), nd + 2, nd + 1, nd).reshape(*lead, n_heads * HEAD_DIM)


def kernel(x_prompt, x_sample, mem_prompt, state_shift, state_conv, state_wkv, cache_mem_k, cache_mem_v,
           ffn1_w1, ffn1_w3, ffn1_w2, ln1_g, ln1_b, w_in, mu_shift, w0, w2_decay, a0, a2_iclr, g2_gate,
           k_k, k_a, r_k, gn_g, gn_b, conv_w, conv_b, conv_ln_g, conv_ln_b, beta_rwkv, beta_conv, w_out,
           ln2_g, ln2_b, w_mq, w_mk, w_mv, w_mo, ln3_g, ln3_b, ffn2_w1, ffn2_w3, ffn2_w2, ln4_g, ln4_b):
    depth = ffn1_w1.shape[0]
    assert depth == 1
    batch, seq, d = x_prompt.shape
    n_s = x_sample.shape[0]
    assert x_sample.shape[1] == 1
    n_mem = mem_prompt.shape[1]
    d_rwkv = w0.shape[-1]
    d_conv = conv_b.shape[-1]
    n_heads = d_rwkv // HEAD_DIM
    n_rkv = 3 * d_rwkv
    shift_cols = mu_shift.shape[-1]
    alpha = (2.0 * depth) ** 0.25
    l = 0

    tok = functools.partial(_to_key_order, n_heads=n_heads)
    tov = functools.partial(_to_value_order, n_heads=n_heads)
    fromk = functools.partial(_from_key_order, n_heads=n_heads)
    fromv = functools.partial(_from_value_order, n_heads=n_heads)

    def rkv_to_kernel_order(x):
        return jnp.concatenate([tok(x[..., :d_rwkv]), tok(x[..., d_rwkv:2 * d_rwkv]), tov(x[..., 2 * d_rwkv:n_rkv])],
                               axis=-1)

    def rkv_from_kernel_order(x):
        return jnp.concatenate([fromk(x[..., :d_rwkv]), fromk(x[..., d_rwkv:2 * d_rwkv]),
                                fromv(x[..., 2 * d_rwkv:n_rkv])], axis=-1)

    w_rkv = rkv_to_kernel_order(w_in[l, :, :n_rkv])
    w_lo = w_in[l, :, n_rkv:shift_cols]
    w_cv = w_in[l, :, shift_cols:]
    mu_rkv = rkv_to_kernel_order(mu_shift[l:l + 1, :n_rkv])
    mu_lo = mu_shift[l:l + 1, n_rkv:]
    gsel, gselt, gsum = _head_selectors(d_rwkv, n_heads)
    prep_consts = (mu_rkv, mu_lo, tok(w0[l:l + 1]), tok(a0[l:l + 1]), tok(k_k[l:l + 1]), tok(k_a[l:l + 1]),
                   tok(r_k[l].reshape(1, d_rwkv)), tok(w2_decay[l]), tok(a2_iclr[l]), tov(g2_gate[l]), gsel, gselt)
    cw = conv_w[l].reshape(CONV_WIDTH, d_conv)
    conv_consts = (cw, conv_b[l:l + 1], conv_ln_g[l:l + 1], conv_ln_b[l:l + 1], beta_conv[l:l + 1])
    post_consts = (tov(gn_g[l:l + 1]), tov(gn_b[l:l + 1]), tov(beta_rwkv[l:l + 1]), gsel, gselt)
    w_out_top_b = _cast_bf16(tov(w_out[l, :d_rwkv].T).T, 512)
    w_out_b = _cast_bf16(w_out[l], 512)
    w_mo_b = _cast_bf16(w_mo[l], 512)

    def trunk_front(x, tm):
        x1, x1b = _ffn_ln(x, None, ffn1_w1[l], ffn1_w3[l], ffn1_w2[l], ln1_g[l:l + 1], ln1_b[l:l + 1], alpha, tm, 256)
        p_rkv = _mm(x1b, w_rkv, tm, 1024)
        p_lo = _mm(x1b, w_lo, tm, w_lo.shape[1])
        p_cv = _mm(x1b, w_cv, tm, 1024)
        return x1, p_rkv, p_lo, p_cv

    def trunk_back(x1, o_a, o_b, attend, tm):
        tm_ln = min(tm, 512)
        x2, x2b = _mm_res_ln([o_a, o_b], [(w_out_top_b, 0), (w_out_b, 1)], x1, ln2_g[l:l + 1], ln2_b[l:l + 1], alpha, tm_ln)
        q = _mm(x2b, w_mq[l], tm, 1024, BF16)
        att = attend(q)
        x3, x3b = _mm_res_ln([att], [(w_mo_b, 0)], x2, ln3_g[l:l + 1], ln3_b[l:l + 1], alpha, tm_ln)
        return _ffn_ln(x3, x3b, ffn2_w1[l], ffn2_w3[l], ffn2_w2[l], ln4_g[l:l + 1], ln4_b[l:l + 1], alpha, tm, 256)[0]

    m_p = batch * seq
    tm_p = 1024
    x1, p_rkv, p_lo, p_cv = trunk_front(x_prompt.reshape(m_p, d), tm_p)
    zeros_rkv = jnp.zeros((batch, 1, n_rkv), F32)
    zeros_lo = jnp.zeros((batch, 1, shift_cols - n_rkv), F32)
    sub = V7X_SUBLANES
    r, w, k, kk, bb, v, g, bonus = _rwkv_prep_seq(p_rkv, p_lo, zeros_rkv, zeros_lo, prep_consts, batch, seq, 256)
    bt = lambda t: t.reshape(batch, seq // sub, sub, d_rwkv)
    y4, s_t = _rwkv_seq(bt(r), bt(w), bt(k), bt(kk), bt(bb), bt(v), gsum, n_heads, 64)
    wkv_p = s_t.reshape(HEAD_DIM // sub, batch, HEAD_DIM, sub, n_heads).transpose(1, 4, 2, 0, 3).reshape(
        batch, n_heads, HEAD_DIM, HEAD_DIM)
    o_a = _rwkv_post(y4.reshape(m_p, d_rwkv), g, bonus, *post_consts, 256)
    o_b, conv_p = _conv_seq(p_cv, *conv_consts, batch, seq, 256)

    mem2 = mem_prompt.reshape(batch * n_mem, d)
    mk_p = _mm(mem2, w_mk[l], batch * n_mem, 512)
    mv_p = _mm(mem2, w_mv[l], batch * n_mem, 512)
    attend_p = lambda q: _attn_seq(q, mk_p.reshape(batch, n_mem, d), mv_p.reshape(batch, n_mem, d), batch, seq, 512)
    y_prompt = trunk_back(x1, o_a, o_b, attend_p, tm_p).reshape(batch, seq, d)
    shift_p = jnp.concatenate([rkv_from_kernel_order(p_rkv.reshape(batch, seq, n_rkv)[:, -1]),
                               p_lo.reshape(batch, seq, -1)[:, -1]], axis=-1)
    dh = d // N_MEM_HEADS

    xs1, ps_rkv, ps_lo, ps_cv = trunk_front(x_sample.reshape(n_s, d), n_s)
    prev = state_shift[l]
    rs, ws, ks, vs, kks, bs, gs, bonus_s = _rwkv_prep_step(ps_rkv, rkv_to_kernel_order(prev[:, :n_rkv]), ps_lo,
                                                          prev[:, n_rkv:], prep_consts, n_s)
    hdk = lambda t: fromk(t).reshape(n_s, n_heads, 1, HEAD_DIM)
    ys, wkv_s = _rwkv_step(state_wkv[l], hdk(rs), hdk(ws), hdk(ks), hdk(kks), hdk(bs),
                           fromv(vs).reshape(n_s, n_heads, 1, HEAD_DIM), 8)
    ys = tov(ys.reshape(n_s, d_rwkv))
    o_as = _rwkv_post(ys, gs, bonus_s, *post_consts, n_s)
    o_bs, conv_s = _conv_step(ps_cv.reshape(n_s, 1, 2 * d_conv), state_conv[l], *conv_consts, 8)
    o_bs = o_bs.reshape(n_s, d_conv).astype(BF16)
    attend_s = lambda q: _attn_step(q.astype(F32).reshape(n_s, 1, N_MEM_HEADS, dh), cache_mem_k[l], cache_mem_v[l],
                                    2).reshape(n_s, d).astype(BF16)
    y_sample = trunk_back(xs1, o_as, o_bs, attend_s, n_s).reshape(n_s, 1, d)
    shift_s = jnp.concatenate([rkv_from_kernel_order(ps_rkv), ps_lo], axis=-1)

    return (y_prompt, y_sample,
            shift_p[None], conv_p[None], wkv_p[None],
            mk_p.reshape(1, batch, n_mem, N_MEM_HEADS, dh), mv_p.reshape(1, batch, n_mem, N_MEM_HEADS, dh),
            shift_s[None], conv_s[None], wkv_s[None])
```

```python
import functools
import math

import jax
import jax.numpy as jnp
import numpy as np
from jax import lax
from jax.experimental import pallas as pl
from jax.experimental.pallas import tpu as pltpu

F32 = jnp.float32
BF16 = jnp.bfloat16

V7X_LANES = 128
V7X_SUBLANES = 8
V7X_VMEM_LIMIT_BYTES = 56 * 1024 * 1024

HEAD_DIM = 64
CONV_WIDTH = 31
DECAY_LORA = 64
ICLR_LORA = 64
GATE_LORA = 160
N_MEM_HEADS = 4
LN_EPS = 1e-5
GN_EPS = 64e-5


def _params(*sem):
    return pltpu.CompilerParams(dimension_semantics=sem, vmem_limit_bytes=V7X_VMEM_LIMIT_BYTES)


def _resident(block_shape, index_map):
    return pl.BlockSpec(block_shape, index_map, pipeline_mode=pl.Buffered(1))


def _ln_rows(y, g, b, eps):
    mu = jnp.mean(y, axis=-1, keepdims=True)
    yc = y - mu
    var = jnp.mean(yc * yc, axis=-1, keepdims=True)
    return yc * lax.rsqrt(var + eps) * g + b


def _dot(a, b):
    return jnp.dot(a, b, preferred_element_type=F32)


def _split_dot(x, g):
    hi = x.astype(BF16)
    mid = (x - hi.astype(F32)).astype(BF16)
    return _dot(hi, g) + _dot(mid, g)


def _head_sum(x, gsel, gselt):
    return _split_dot(_split_dot(x, gsel), gselt)


def _ffn_ln_kernel(x_ref, *refs, alpha, n_chunk, row_chunk, cast_x):
    if cast_x:
        w1_ref, w3_ref, w2_ref, g_ref, b_ref, o_ref, ob_ref, xb_ref = refs
    else:
        xb_ref, w1_ref, w3_ref, w2_ref, g_ref, b_ref, o_ref, ob_ref = refs
    j = pl.program_id(1)
    tm, d = o_ref.shape

    @pl.when(j == 0)
    def _():
        o_ref[...] = jnp.zeros_like(o_ref)
        if cast_x:
            xb_ref[...] = x_ref[...].astype(BF16)

    xb = xb_ref[...]
    h1 = _dot(xb, w1_ref[...].astype(BF16))
    h3 = _dot(xb, w3_ref[...].astype(BF16))
    hb = (h1 * jax.nn.sigmoid(h1) * h3).astype(BF16)
    w2b = w2_ref[...].astype(BF16)
    cw = d // n_chunk
    for c in range(n_chunk):
        o_ref[:, c * cw:(c + 1) * cw] += _dot(hb, w2b[:, c * cw:(c + 1) * cw])

    @pl.when(j == pl.num_programs(1) - 1)
    def _():
        def body(r, carry):
            rows = pl.ds(pl.multiple_of(r * row_chunk, row_chunk), row_chunk)
            y = alpha * x_ref[rows, :] + 0.5 * o_ref[rows, :]
            y = _ln_rows(y, g_ref[...], b_ref[...], LN_EPS)
            o_ref[rows, :] = y
            ob_ref[rows, :] = y.astype(BF16)
            return carry

        lax.fori_loop(0, tm // row_chunk, body, 0)


def _ffn_ln(x, xb, w1, w3, w2, g, b, alpha, tm, tf):
    m, d = x.shape
    f = w1.shape[1]
    cast_x = xb is None
    kern = functools.partial(_ffn_ln_kernel, alpha=alpha, n_chunk=4, row_chunk=min(tm, 128), cast_x=cast_x)
    row_tile = lambda i, j: (i, 0)
    return pl.pallas_call(
        kern,
        out_shape=[jax.ShapeDtypeStruct((m, d), F32), jax.ShapeDtypeStruct((m, d), BF16)],
        grid=(m // tm, f // tf),
        in_specs=[pl.BlockSpec((tm, d), row_tile)] + ([] if cast_x else [_resident((tm, d), row_tile)]) + [
            pl.BlockSpec((d, tf), lambda i, j: (0, j)),
            pl.BlockSpec((d, tf), lambda i, j: (0, j)),
            pl.BlockSpec((tf, d), lambda i, j: (j, 0)),
            pl.BlockSpec((1, d), lambda i, j: (0, 0)),
            pl.BlockSpec((1, d), lambda i, j: (0, 0)),
        ],
        out_specs=[pl.BlockSpec((tm, d), row_tile), _resident((tm, d), row_tile)],
        scratch_shapes=[pltpu.VMEM((tm, d), BF16)] if cast_x else [],
        compiler_params=_params("parallel", "arbitrary"),
        name="ffn_ln",
    )(*((x,) if cast_x else (x, xb)), w1, w3, w2, g, b)


def _mm_kernel(x_ref, w_ref, o_ref):
    o_ref[...] = _dot(x_ref[...].astype(BF16), w_ref[...].astype(BF16)).astype(o_ref.dtype)


def _mm(x, w, tm, tn, out_dtype=F32):
    m, k = x.shape
    n = w.shape[1]
    return pl.pallas_call(
        _mm_kernel,
        out_shape=jax.ShapeDtypeStruct((m, n), out_dtype),
        grid=(m // tm, n // tn),
        in_specs=[
            pl.BlockSpec((tm, k), lambda i, j: (i, 0)),
            pl.BlockSpec((k, tn), lambda i, j: (0, j)),
        ],
        out_specs=pl.BlockSpec((tm, tn), lambda i, j: (i, j)),
        compiler_params=_params("parallel", "arbitrary"),
        name="mm",
    )(x, w)


def _cast_kernel(x_ref, o_ref):
    o_ref[...] = x_ref[...].astype(o_ref.dtype)


def _cast_bf16(w, tr):
    rows, cols = w.shape
    return pl.pallas_call(
        _cast_kernel,
        out_shape=jax.ShapeDtypeStruct(w.shape, BF16),
        grid=(rows // tr,),
        in_specs=[pl.BlockSpec((tr, cols), lambda i: (i, 0))],
        out_specs=pl.BlockSpec((tr, cols), lambda i: (i, 0)),
        compiler_params=_params("parallel"),
        name="cast_bf16",
    )(w)


def _mm_res_ln_kernel(*refs, n_a, alpha, row_chunk):
    a_refs = refs[:n_a]
    w_refs = refs[n_a:2 * n_a]
    res_ref, g_ref, b_ref, o_ref, ob_ref, acc_ref = refs[2 * n_a:]
    tm, d = acc_ref.shape

    acc = _dot(a_refs[0][...], w_refs[0][...])
    for a_ref, w_ref in zip(a_refs[1:], w_refs[1:]):
        acc += _dot(a_ref[...], w_ref[...])
    acc_ref[...] = acc

    def body(r, carry):
        rows = pl.ds(pl.multiple_of(r * row_chunk, row_chunk), row_chunk)
        y = _ln_rows(alpha * res_ref[rows, :] + acc_ref[rows, :], g_ref[...], b_ref[...], LN_EPS)
        o_ref[rows, :] = y
        ob_ref[rows, :] = y.astype(BF16)
        return carry

    lax.fori_loop(0, tm // row_chunk, body, 0)


def _mm_res_ln(a_list, w_list, res, g, b, alpha, tm):
    m, d = res.shape
    n_a = len(a_list)
    kern = functools.partial(_mm_res_ln_kernel, n_a=n_a, alpha=alpha, row_chunk=min(tm, 128))
    row_tile = lambda i: (i, 0)
    in_specs = [pl.BlockSpec((tm, a.shape[1]), row_tile) for a in a_list]
    in_specs += [_resident((a.shape[1], d), functools.partial(lambda i, r: (r, 0), r=r))
                 for a, (_, r) in zip(a_list, w_list)]
    in_specs += [
        pl.BlockSpec((tm, d), row_tile),
        pl.BlockSpec((1, d), lambda i: (0, 0)),
        pl.BlockSpec((1, d), lambda i: (0, 0)),
    ]
    return pl.pallas_call(
        kern,
        out_shape=[jax.ShapeDtypeStruct((m, d), F32), jax.ShapeDtypeStruct((m, d), BF16)],
        grid=(m // tm,),
        in_specs=in_specs,
        out_specs=[pl.BlockSpec((tm, d), row_tile), pl.BlockSpec((tm, d), row_tile)],
        scratch_shapes=[pltpu.VMEM((tm, d), F32)],
        compiler_params=_params("parallel"),
        name="mm_res_ln",
    )(*a_list, *[w for w, _ in w_list], res, g, b)


def _shift_rows(cur, first_row):
    rolled = pltpu.roll(cur, 1, 0)
    row_id = lax.broadcasted_iota(jnp.int32, cur.shape, 0)
    return jnp.where(row_id == 0, first_row, rolled)


def _rwkv_prep_math(cur_rkv, prev_rkv, cur_lo, prev_lo, mu_rkv, mu_lo, w0, a0, k_k, k_a, r_k,
                    w2d, a2, g2, gsel, gselt, d_rwkv):
    pm = cur_rkv + mu_rkv * (prev_rkv - cur_rkv)
    pl_ = cur_lo + mu_lo * (prev_lo - cur_lo)
    r = pm[:, :d_rwkv]
    k = pm[:, d_rwkv:2 * d_rwkv]
    v = pm[:, 2 * d_rwkv:]
    wd = pl_[:, :DECAY_LORA]
    ad = pl_[:, DECAY_LORA:DECAY_LORA + ICLR_LORA]
    gd = pl_[:, DECAY_LORA + ICLR_LORA:]
    z = w0 + _dot(jnp.tanh(wd).astype(BF16), w2d.astype(BF16))
    w_log = -jax.nn.softplus(-z) - 0.5
    decay = jnp.exp(-jnp.exp(w_log))
    a = jax.nn.sigmoid(a0 + _dot(ad.astype(BF16), a2.astype(BF16)))
    g = _dot(jax.nn.sigmoid(gd).astype(BF16), g2.astype(BF16))
    kk = k * k_k
    norm = jnp.sqrt(_head_sum(kk * kk, gsel, gselt))
    kk = kk / jnp.maximum(norm, 1e-12)
    kmod = k * (1.0 + (a - 1.0) * k_a)
    bonus = _head_sum(r * kmod * r_k, gsel, gselt) * v
    return r, decay, kmod, v, kk, kk * a, g, bonus


def _rwkv_prep_seq_kernel(cur_rkv_ref, prev8_rkv_ref, cur_lo_ref, prev8_lo_ref, sp_rkv_ref, sp_lo_ref,
                          mu_rkv_ref, mu_lo_ref, w0_ref, a0_ref, kk_ref, ka_ref, rk_ref,
                          w2d_ref, a2_ref, g2_ref, gsel_ref, gselt_ref,
                          r_o, w_o, k_o, kk_o, b_o, v_o, g_o, bonus_o):
    first = pl.program_id(1) == 0
    cur_rkv = cur_rkv_ref[...]
    cur_lo = cur_lo_ref[...]
    last = V7X_SUBLANES - 1
    first_rkv = jnp.where(first, sp_rkv_ref[...], prev8_rkv_ref[last:last + 1, :])
    first_lo = jnp.where(first, sp_lo_ref[...], prev8_lo_ref[last:last + 1, :])
    r, w, k, v, kk, b, g, bonus = _rwkv_prep_math(
        cur_rkv, _shift_rows(cur_rkv, first_rkv), cur_lo, _shift_rows(cur_lo, first_lo),
        mu_rkv_ref[...], mu_lo_ref[...], w0_ref[...], a0_ref[...], kk_ref[...], ka_ref[...], rk_ref[...],
        w2d_ref[...], a2_ref[...], g2_ref[...], gsel_ref[...], gselt_ref[...], g_o.shape[-1])
    for o_ref, val in zip((r_o, w_o, k_o, kk_o, b_o, v_o, g_o, bonus_o), (r, w, k, kk, b, v, g, bonus)):
        o_ref[...] = val


def _rwkv_prep_step_kernel(cur_rkv_ref, prev_rkv_ref, cur_lo_ref, prev_lo_ref,
                           mu_rkv_ref, mu_lo_ref, w0_ref, a0_ref, kk_ref, ka_ref, rk_ref,
                           w2d_ref, a2_ref, g2_ref, gsel_ref, gselt_ref,
                           r_o, w_o, k_o, v_o, kk_o, b_o, g_o, bonus_o):
    outs = _rwkv_prep_math(
        cur_rkv_ref[...], prev_rkv_ref[...], cur_lo_ref[...], prev_lo_ref[...],
        mu_rkv_ref[...], mu_lo_ref[...], w0_ref[...], a0_ref[...], kk_ref[...], ka_ref[...], rk_ref[...],
        w2d_ref[...], a2_ref[...], g2_ref[...], gsel_ref[...], gselt_ref[...], r_o.shape[-1])
    for o_ref, val in zip((r_o, w_o, k_o, v_o, kk_o, b_o, g_o, bonus_o), outs):
        o_ref[...] = val


def _const_spec(arr):
    nd = arr.ndim
    return pl.BlockSpec(arr.shape, lambda *_: (0,) * nd)


def _rwkv_prep_seq(p_rkv, p_lo, sp_rkv, sp_lo, consts, batch, seq, tt):
    m, n_rkv = p_rkv.shape
    n_lo = p_lo.shape[1]
    d_rwkv = n_rkv // 3
    nt = seq // tt
    sub = V7X_SUBLANES

    def cur_map(b, i):
        return (b * nt + i, 0)

    def prev_map(b, i):
        return (jnp.maximum((b * seq + i * tt) // sub - 1, 0), 0)

    out_shape = [jax.ShapeDtypeStruct((m, d_rwkv), F32)] * 8
    out_specs = [pl.BlockSpec((tt, d_rwkv), cur_map)] * 8
    return pl.pallas_call(
        _rwkv_prep_seq_kernel,
        out_shape=out_shape,
        grid=(batch, nt),
        in_specs=[
            pl.BlockSpec((tt, n_rkv), cur_map),
            pl.BlockSpec((sub, n_rkv), prev_map),
            pl.BlockSpec((tt, n_lo), cur_map),
            pl.BlockSpec((sub, n_lo), prev_map),
            pl.BlockSpec((None, 1, n_rkv), lambda b, i: (b, 0, 0)),
            pl.BlockSpec((None, 1, n_lo), lambda b, i: (b, 0, 0)),
        ] + [_const_spec(c) for c in consts],
        out_specs=out_specs,
        compiler_params=_params("parallel", "arbitrary"),
        name="rwkv_prep_seq",
    )(p_rkv, p_rkv, p_lo, p_lo, sp_rkv, sp_lo, *consts)


def _rwkv_prep_step(p_rkv, prev_rkv, p_lo, prev_lo, consts, tb):
    m, n_rkv = p_rkv.shape
    n_lo = p_lo.shape[1]
    d_rwkv = n_rkv // 3
    out_sds = jax.ShapeDtypeStruct((m, d_rwkv), F32)
    row = lambda i: (i, 0)
    return pl.pallas_call(
        _rwkv_prep_step_kernel,
        out_shape=[out_sds] * 8,
        grid=(m // tb,),
        in_specs=[
            pl.BlockSpec((tb, n_rkv), row), pl.BlockSpec((tb, n_rkv), row),
            pl.BlockSpec((tb, n_lo), row), pl.BlockSpec((tb, n_lo), row),
        ] + [_const_spec(c) for c in consts],
        out_specs=[pl.BlockSpec((tb, d_rwkv), row)] * 8,
        compiler_params=_params("parallel"),
        name="rwkv_prep_step",
    )(p_rkv, prev_rkv, p_lo, prev_lo, *consts)


def _split_dot_stacked(x, g):
    hi = x.astype(BF16)
    mid = (x - hi.astype(F32)).astype(BF16)
    both = _dot(jnp.concatenate([hi, mid], axis=0), g)
    return both[:x.shape[0]] + both[x.shape[0]:]


def _rwkv_seq_kernel(r_ref, w_ref, k_ref, kk_ref, b_ref, v_ref, gsum_ref, y_ref, s_ref, *, group, n_chain):
    nb, n_t8, sub, _ = r_ref.shape
    n_kq, _, lanes = s_ref.shape
    n_v = s_ref.shape[1] // nb
    n_vq = n_v // sub
    nbc = nb // n_chain

    @pl.when(pl.program_id(0) == 0)
    def _():
        s_ref[...] = jnp.zeros_like(s_ref)

    lane_grp = lax.broadcasted_iota(jnp.int32, (sub, lanes), 1) // group
    sub_id = lax.broadcasted_iota(jnp.int32, (sub, lanes), 0)
    gsum = gsum_ref[...]

    def update_state(t8, j, c):
        seqs = range(c * nbc, (c + 1) * nbc)
        srows = slice(c * nbc * n_v, (c + 1) * nbc * n_v)

        def row(ref, b, cidx):
            return ref[b, t8, pl.ds(j, 1), cidx * lanes:(cidx + 1) * lanes]

        def rows(ref, kq):
            return jnp.concatenate([jnp.broadcast_to(row(ref, b, kq), (n_v, lanes)) for b in seqs], axis=0)

        def v_tile(b):
            vt = jnp.broadcast_to(row(v_ref, b, 0), (sub, lanes))
            for v8 in range(1, sub):
                vt = jnp.where(sub_id == v8, row(v_ref, b, v8), vt)
            return vt

        v_all = _split_dot_stacked(jnp.concatenate(
            [jnp.where(lane_grp == vq, vt, 0.0) for vt in map(v_tile, seqs) for vq in range(n_vq)], axis=0), gsum)
        sa = s_ref[0, srows, :] * rows(kk_ref, 0)
        for kq in range(1, n_kq):
            sa += s_ref[kq, srows, :] * rows(kk_ref, kq)
        sa = _split_dot_stacked(sa, gsum)
        y = None
        for kq in range(n_kq):
            s_new = s_ref[kq, srows, :] * rows(w_ref, kq) - rows(b_ref, kq) * sa + rows(k_ref, kq) * v_all
            s_ref[kq, srows, :] = s_new
            term = s_new * rows(r_ref, kq)
            y = term if y is None else y + term
        return y

    def emit_y(t8, j, c, y):
        y = _split_dot_stacked(y, gsum)
        for i in range(nbc):
            yc = jnp.zeros((sub, lanes), F32)
            for vq in range(n_vq):
                lo = (i * n_vq + vq) * sub
                yc = jnp.where(lane_grp == vq, y[lo:lo + sub, :], yc)
            for v8 in range(sub):
                y_ref[c * nbc + i, t8, pl.ds(j, 1), v8 * lanes:(v8 + 1) * lanes] = yc[v8:v8 + 1, :]

    def tile(t8, y_prev):
        for j in range(sub):
            y_cur = tuple(update_state(t8, j, c) for c in range(n_chain))
            if y_prev is not None:
                for c in range(n_chain):
                    emit_y(t8 if j else t8 - 1, (j - 1) % sub, c, y_prev[c])
            y_prev = y_cur
        return y_prev

    y_last = lax.fori_loop(1, n_t8, tile, tile(0, None))
    for c in range(n_chain):
        emit_y(n_t8 - 1, sub - 1, c, y_last[c])


def _rwkv_seq(r, w, k, kk, b, v, gsum, n_heads, tb):
    batch, seq8, sub, d_rwkv = r.shape
    lanes = V7X_LANES
    n_kq = d_rwkv // lanes
    assert sub == V7X_SUBLANES and lanes == sub * n_heads and d_rwkv == HEAD_DIM * n_heads and HEAD_DIM == sub * sub
    n_chain = 2 if batch % 2 == 0 else 1
    spec = pl.BlockSpec((batch, tb // sub, sub, d_rwkv), lambda i: (0, i, 0, 0))
    state_shape = (n_kq, batch * HEAD_DIM, lanes)
    seq = seq8 * sub
    return pl.pallas_call(
        functools.partial(_rwkv_seq_kernel, group=n_heads, n_chain=n_chain),
        out_shape=[jax.ShapeDtypeStruct(r.shape, F32),
                   jax.ShapeDtypeStruct(state_shape, F32)],
        grid=(seq // tb,),
        in_specs=[spec] * 6 + [_const_spec(gsum)],
        out_specs=[spec, pl.BlockSpec(state_shape, lambda i: (0, 0, 0))],
        compiler_params=_params("arbitrary"),
        name="rwkv_seq",
    )(r, w, k, kk, b, v, gsum)


def _rwkv_step_kernel(s_ref, r_ref, w_ref, k_ref, kk_ref, b_ref, v_ref, y_ref, so_ref):
    n = s_ref.shape[-1]
    eye = (lax.broadcasted_iota(jnp.int32, (n, n), 0) == lax.broadcasted_iota(jnp.int32, (n, n), 1)).astype(F32)
    s = s_ref[...]
    v_col = jnp.sum(eye * v_ref[...], axis=-1, keepdims=True)
    sa = jnp.sum(s * kk_ref[...], axis=-1, keepdims=True)
    s_new = s * w_ref[...] - sa * b_ref[...] + v_col * k_ref[...]
    so_ref[...] = s_new
    y_col = jnp.sum(s_new * r_ref[...], axis=-1, keepdims=True)
    y_ref[...] = jnp.sum(eye * y_col, axis=-2, keepdims=True)


def _rwkv_step(state, r, w, k, kk, b, v, sb):
    n_s, n_h, n, _ = state.shape
    sspec = pl.BlockSpec((sb, n_h, n, n), lambda i: (i, 0, 0, 0))
    vspec = pl.BlockSpec((sb, n_h, 1, n), lambda i: (i, 0, 0, 0))
    return pl.pallas_call(
        _rwkv_step_kernel,
        out_shape=[jax.ShapeDtypeStruct((n_s, n_h, 1, n), F32), jax.ShapeDtypeStruct(state.shape, F32)],
        grid=(n_s // sb,),
        in_specs=[sspec] + [vspec] * 6,
        out_specs=[vspec, sspec],
        compiler_params=_params("parallel"),
        name="rwkv_step",
    )(state, r, w, k, kk, b, v)


def _rwkv_post_kernel(y_ref, g_ref, bonus_ref, gng_ref, gnb_ref, beta_ref, gsel_ref, gselt_ref, o_ref):
    y = y_ref[...]
    gsel = gsel_ref[...]
    gselt = gselt_ref[...]
    mu = _head_sum(y, gsel, gselt) * (1.0 / HEAD_DIM)
    yc = y - mu
    var = _head_sum(yc * yc, gsel, gselt) * (1.0 / HEAD_DIM)
    yn = yc * lax.rsqrt(var + GN_EPS) * gng_ref[...] + gnb_ref[...]
    o_ref[...] = ((yn + bonus_ref[...]) * g_ref[...] * beta_ref[...]).astype(o_ref.dtype)


def _rwkv_post(y, g, bonus, gn_g, gn_b, beta, gsel, gselt, tt):
    m, d = y.shape
    row = pl.BlockSpec((tt, d), lambda i: (i, 0))
    consts = (gn_g, gn_b, beta, gsel, gselt)
    return pl.pallas_call(
        _rwkv_post_kernel,
        out_shape=jax.ShapeDtypeStruct((m, d), BF16),
        grid=(m // tt,),
        in_specs=[row, row, row] + [_const_spec(c) for c in consts],
        out_specs=row,
        compiler_params=_params("parallel"),
        name="rwkv_post",
    )(y, g, bonus, *consts)


CONV_HALO = 32


def _glu(pc, d_conv):
    return pc[:, :d_conv] * jax.nn.sigmoid(pc[:, d_conv:])


def _conv_seq_kernel(pc_ref, halo_ref, cw_ref, cb_ref, lng_ref, lnb_ref, beta_ref, o_ref, st_ref, ubuf_ref,
                     ushift_ref, *, row_chunk):
    tt, d_conv = o_ref.shape
    i = pl.program_id(1)
    sub = V7X_SUBLANES
    n_hist = CONV_WIDTH - 1
    pad = CONV_HALO - n_hist
    u_halo = _glu(halo_ref[...], d_conv)
    ubuf_ref[0:CONV_HALO, :] = jnp.where(i == 0, jnp.zeros_like(u_halo), u_halo)
    ubuf_ref[CONV_HALO:, :] = _glu(pc_ref[...], d_conv)
    n_shift_rows = ushift_ref.shape[1]
    for s in range(1, sub):
        ushift_ref[s - 1] = ubuf_ref[s:s + n_shift_rows, :]

    def tap(base, w):
        off = pad + w
        q, s = off // sub, off % sub
        rows = slice(base + q * sub, base + q * sub + row_chunk)
        return (ubuf_ref[rows, :] if s == 0 else ushift_ref[s - 1, rows, :]) * cw_ref[w:w + 1, :]

    for base in range(0, tt, row_chunk):
        acc = tap(base, 0)
        for w in range(1, CONV_WIDTH):
            acc += tap(base, w)
        c = _ln_rows(acc + cb_ref[...], lng_ref[...], lnb_ref[...], LN_EPS)
        o_ref[base:base + row_chunk, :] = (c * jax.nn.sigmoid(c) * beta_ref[...]).astype(o_ref.dtype)

    @pl.when(i == pl.num_programs(1) - 1)
    def _():
        st_ref[...] = ubuf_ref[CONV_HALO + tt - n_hist:, :]


def _conv_seq(pc, cw, cb, ln_g, ln_b, beta, batch, seq, tt):
    m, two_d = pc.shape
    d_conv = two_d // 2
    nt = seq // tt
    hb = tt // CONV_HALO
    consts = (cw, cb, ln_g, ln_b, beta)
    kern = functools.partial(_conv_seq_kernel, row_chunk=16)
    return pl.pallas_call(
        kern,
        out_shape=[jax.ShapeDtypeStruct((m, d_conv), BF16),
                   jax.ShapeDtypeStruct((batch, CONV_WIDTH - 1, d_conv), F32)],
        grid=(batch, nt),
        in_specs=[
            pl.BlockSpec((tt, two_d), lambda b, i: (b * nt + i, 0)),
            pl.BlockSpec((CONV_HALO, two_d), lambda b, i: (jnp.maximum((b * nt + i) * hb - 1, 0), 0)),
        ] + [_const_spec(c) for c in consts],
        out_specs=[pl.BlockSpec((tt, d_conv), lambda b, i: (b * nt + i, 0)),
                   pl.BlockSpec((None, CONV_WIDTH - 1, d_conv), lambda b, i: (b, 0, 0))],
        scratch_shapes=[pltpu.VMEM((CONV_HALO + tt, d_conv), F32),
                        pltpu.VMEM((V7X_SUBLANES - 1, CONV_HALO + tt - V7X_SUBLANES, d_conv), F32)],
        compiler_params=_params("parallel", "arbitrary"),
        name="conv_seq",
    )(pc, pc, *consts)


def _conv_step_kernel(pc_ref, st_ref, cw_ref, cb_ref, lng_ref, lnb_ref, beta_ref, o_ref, so_ref):
    d_conv = o_ref.shape[-1]
    n_hist = CONV_WIDTH - 1
    pc = pc_ref[...]
    u = pc[:, :, :d_conv] * jax.nn.sigmoid(pc[:, :, d_conv:])
    st = st_ref[...]
    c = (jnp.sum(st * cw_ref[0:n_hist, :], axis=1, keepdims=True)
         + u * cw_ref[n_hist:CONV_WIDTH, :] + cb_ref[...])
    c = _ln_rows(c, lng_ref[...], lnb_ref[...], LN_EPS)
    o_ref[...] = c * jax.nn.sigmoid(c) * beta_ref[...]
    so_ref[:, 0:n_hist - 1, :] = st[:, 1:, :]
    so_ref[:, n_hist - 1:n_hist, :] = u


def _conv_step(pc, state, cw, cb, ln_g, ln_b, beta, sb):
    n_s, _, two_d = pc.shape
    d_conv = two_d // 2
    consts = (cw, cb, ln_g, ln_b, beta)
    sspec = pl.BlockSpec((sb, CONV_WIDTH - 1, d_conv), lambda i: (i, 0, 0))
    return pl.pallas_call(
        _conv_step_kernel,
        out_shape=[jax.ShapeDtypeStruct((n_s, 1, d_conv), F32), jax.ShapeDtypeStruct(state.shape, F32)],
        grid=(n_s // sb,),
        in_specs=[pl.BlockSpec((sb, 1, two_d), lambda i: (i, 0, 0)), sspec] + [_const_spec(c) for c in consts],
        out_specs=[pl.BlockSpec((sb, 1, d_conv), lambda i: (i, 0, 0)), sspec],
        compiler_params=_params("parallel"),
        name="conv_step",
    )(pc, state, *consts)


def _attn_seq_kernel(q_ref, k_ref, v_ref, o_ref, *, scale):
    d = q_ref.shape[-1]
    dh = d // N_MEM_HEADS
    for h in range(N_MEM_HEADS):
        cols = slice(h * dh, (h + 1) * dh)
        k = k_ref[:, cols].astype(BF16)
        s = lax.dot_general(q_ref[:, cols], k, (((1,), (1,)), ((), ())), preferred_element_type=F32) * scale
        s = s - jnp.max(s, axis=-1, keepdims=True)
        e = jnp.exp(s)
        p = e / jnp.sum(e, axis=-1, keepdims=True)
        o_ref[:, cols] = _dot(p.astype(BF16), v_ref[:, cols].astype(BF16)).astype(o_ref.dtype)


def _attn_seq(q, mk, mv, batch, seq, tq):
    m, d = q.shape
    n_mem = mk.shape[1]
    nt = seq // tq
    kern = functools.partial(_attn_seq_kernel, scale=1.0 / math.sqrt(d // N_MEM_HEADS))
    kv_spec = pl.BlockSpec((None, n_mem, d), lambda b, i: (b, 0, 0))
    return pl.pallas_call(
        kern,
        out_shape=jax.ShapeDtypeStruct((m, d), BF16),
        grid=(batch, nt),
        in_specs=[pl.BlockSpec((tq, d), lambda b, i: (b * nt + i, 0)), kv_spec, kv_spec],
        out_specs=pl.BlockSpec((tq, d), lambda b, i: (b * nt + i, 0)),
        compiler_params=_params("parallel", "arbitrary"),
        name="attn_seq",
    )(q, mk, mv)


def _attn_step_kernel(q_ref, k_ref, v_ref, o_ref, *, scale):
    for i in range(q_ref.shape[0]):
        s = jnp.sum(k_ref[i] * q_ref[i], axis=-1, keepdims=True) * scale
        s = s - jnp.max(s, axis=0, keepdims=True)
        e = jnp.exp(s)
        p = e / jnp.sum(e, axis=0, keepdims=True)
        n_mem, n_h, dh = v_ref.shape[1:]
        p_lanes = jnp.broadcast_to(p, (n_mem, n_h, V7X_LANES))
        for c in range(dh // V7X_LANES):
            cols = slice(c * V7X_LANES, (c + 1) * V7X_LANES)
            o_ref[i, :, :, cols] = jnp.sum(p_lanes * v_ref[i, :, :, cols], axis=0, keepdims=True)


def _attn_step(q, mk, mv, sb):
    n_s, _, n_h, dh = q.shape
    n_mem = mk.shape[1]
    kern = functools.partial(_attn_step_kernel, scale=1.0 / math.sqrt(dh))
    qspec = pl.BlockSpec((sb, 1, n_h, dh), lambda i: (i, 0, 0, 0))
    kvspec = pl.BlockSpec((sb, n_mem, n_h, dh), lambda i: (i, 0, 0, 0))
    return pl.pallas_call(
        kern,
        out_shape=jax.ShapeDtypeStruct(q.shape, F32),
        grid=(n_s // sb,),
        in_specs=[qspec, kvspec, kvspec],
        out_specs=qspec,
        compiler_params=_params("parallel"),
        name="attn_step",
    )(q, mk, mv)


def _head_selectors(d_rwkv, n_heads):
    assert n_heads <= V7X_LANES
    lane = np.arange(V7X_LANES)
    chan = np.arange(d_rwkv)
    gsel = chan[:, None] % n_heads == lane[None, :]
    gsum = lane[:, None] % n_heads == lane[None, :] % n_heads
    as_bf16 = lambda a: jnp.asarray(a.astype(np.float32), BF16)
    return as_bf16(gsel), as_bf16(gsel.T), as_bf16(gsum)


def _to_key_order(x, n_heads):
    lead = x.shape[:-1]
    return jnp.swapaxes(x.reshape(*lead, n_heads, HEAD_DIM), -1, -2).reshape(*lead, n_heads * HEAD_DIM)


def _from_key_order(x, n_heads):
    lead = x.shape[:-1]
    return jnp.swapaxes(x.reshape(*lead, HEAD_DIM, n_heads), -1, -2).reshape(*lead, n_heads * HEAD_DIM)


def _to_value_order(x, n_heads):
    lead = x.shape[:-1]
    nd = len(lead)
    sub = V7X_SUBLANES
    x = x.reshape(*lead, n_heads, HEAD_DIM // sub, sub)
    return x.transpose(*range(nd), nd + 2, nd + 1, nd).reshape(*lead, n_heads * HEAD_DIM)


def _from_value_order(x, n_heads):
    lead = x.shape[:-1]
    nd = len(lead)
    sub = V7X_SUBLANES
    x = x.reshape(*lead, sub, HEAD_DIM // sub, n_heads)
    return x.transpose(*range(nd), nd + 2, nd + 1, nd).reshape(*lead, n_heads * HEAD_DIM)


def kernel(x_prompt, x_sample, mem_prompt, state_shift, state_conv, state_wkv, cache_mem_k, cache_mem_v,
           ffn1_w1, ffn1_w3, ffn1_w2, ln1_g, ln1_b, w_in, mu_shift, w0, w2_decay, a0, a2_iclr, g2_gate,
           k_k, k_a, r_k, gn_g, gn_b, conv_w, conv_b, conv_ln_g, conv_ln_b, beta_rwkv, beta_conv, w_out,
           ln2_g, ln2_b, w_mq, w_mk, w_mv, w_mo, ln3_g, ln3_b, ffn2_w1, ffn2_w3, ffn2_w2, ln4_g, ln4_b):
    depth = ffn1_w1.shape[0]
    assert depth == 1
    batch, seq, d = x_prompt.shape
    n_s = x_sample.shape[0]
    assert x_sample.shape[1] == 1
    n_mem = mem_prompt.shape[1]
    d_rwkv = w0.shape[-1]
    d_conv = conv_b.shape[-1]
    n_heads = d_rwkv // HEAD_DIM
    n_rkv = 3 * d_rwkv
    shift_cols = mu_shift.shape[-1]
    alpha = (2.0 * depth) ** 0.25
    l = 0

    tok = functools.partial(_to_key_order, n_heads=n_heads)
    tov = functools.partial(_to_value_order, n_heads=n_heads)
    fromk = functools.partial(_from_key_order, n_heads=n_heads)
    fromv = functools.partial(_from_value_order, n_heads=n_heads)

    def rkv_to_kernel_order(x):
        return jnp.concatenate([tok(x[..., :d_rwkv]), tok(x[..., d_rwkv:2 * d_rwkv]), tov(x[..., 2 * d_rwkv:n_rkv])],
                               axis=-1)

    def rkv_from_kernel_order(x):
        return jnp.concatenate([fromk(x[..., :d_rwkv]), fromk(x[..., d_rwkv:2 * d_rwkv]),
                                fromv(x[..., 2 * d_rwkv:n_rkv])], axis=-1)

    w_rkv = rkv_to_kernel_order(w_in[l, :, :n_rkv])
    w_lo = w_in[l, :, n_rkv:shift_cols]
    w_cv = w_in[l, :, shift_cols:]
    mu_rkv = rkv_to_kernel_order(mu_shift[l:l + 1, :n_rkv])
    mu_lo = mu_shift[l:l + 1, n_rkv:]
    gsel, gselt, gsum = _head_selectors(d_rwkv, n_heads)
    prep_consts = (mu_rkv, mu_lo, tok(w0[l:l + 1]), tok(a0[l:l + 1]), tok(k_k[l:l + 1]), tok(k_a[l:l + 1]),
                   tok(r_k[l].reshape(1, d_rwkv)), tok(w2_decay[l]), tok(a2_iclr[l]), tov(g2_gate[l]), gsel, gselt)
    cw = conv_w[l].reshape(CONV_WIDTH, d_conv)
    conv_consts = (cw, conv_b[l:l + 1], conv_ln_g[l:l + 1], conv_ln_b[l:l + 1], beta_conv[l:l + 1])
    post_consts = (tov(gn_g[l:l + 1]), tov(gn_b[l:l + 1]), tov(beta_rwkv[l:l + 1]), gsel, gselt)
    w_out_top_b = _cast_bf16(tov(w_out[l, :d_rwkv].T).T, 512)
    w_out_b = _cast_bf16(w_out[l], 512)
    w_mo_b = _cast_bf16(w_mo[l], 512)

    def trunk_front(x, tm):
        x1, x1b = _ffn_ln(x, None, ffn1_w1[l], ffn1_w3[l], ffn1_w2[l], ln1_g[l:l + 1], ln1_b[l:l + 1], alpha, tm, 256)
        p_rkv = _mm(x1b, w_rkv, tm, 1024)
        p_lo = _mm(x1b, w_lo, tm, w_lo.shape[1])
        p_cv = _mm(x1b, w_cv, tm, 1024)
        return x1, p_rkv, p_lo, p_cv

    def trunk_back(x1, o_a, o_b, attend, tm):
        tm_ln = min(tm, 512)
        x2, x2b = _mm_res_ln([o_a, o_b], [(w_out_top_b, 0), (w_out_b, 1)], x1, ln2_g[l:l + 1], ln2_b[l:l + 1], alpha, tm_ln)
        q = _mm(x2b, w_mq[l], tm, 1024, BF16)
        att = attend(q)
        x3, x3b = _mm_res_ln([att], [(w_mo_b, 0)], x2, ln3_g[l:l + 1], ln3_b[l:l + 1], alpha, tm_ln)
        return _ffn_ln(x3, x3b, ffn2_w1[l], ffn2_w3[l], ffn2_w2[l], ln4_g[l:l + 1], ln4_b[l:l + 1], alpha, tm, 256)[0]

    m_p = batch * seq
    tm_p = 1024
    x1, p_rkv, p_lo, p_cv = trunk_front(x_prompt.reshape(m_p, d), tm_p)
    zeros_rkv = jnp.zeros((batch, 1, n_rkv), F32)
    zeros_lo = jnp.zeros((batch, 1, shift_cols - n_rkv), F32)
    sub = V7X_SUBLANES
    r, w, k, kk, bb, v, g, bonus = _rwkv_prep_seq(p_rkv, p_lo, zeros_rkv, zeros_lo, prep_consts, batch, seq, 256)
    bt = lambda t: t.reshape(batch, seq // sub, sub, d_rwkv)
    y4, s_t = _rwkv_seq(bt(r), bt(w), bt(k), bt(kk), bt(bb), bt(v), gsum, n_heads, 64)
    wkv_p = s_t.reshape(HEAD_DIM // sub, batch, HEAD_DIM, sub, n_heads).transpose(1, 4, 2, 0, 3).reshape(
        batch, n_heads, HEAD_DIM, HEAD_DIM)
    o_a = _rwkv_post(y4.reshape(m_p, d_rwkv), g, bonus, *post_consts, 256)
    o_b, conv_p = _conv_seq(p_cv, *conv_consts, batch, seq, 256)

    mem2 = mem_prompt.reshape(batch * n_mem, d)
    mk_p = _mm(mem2, w_mk[l], batch * n_mem, 512)
    mv_p = _mm(mem2, w_mv[l], batch * n_mem, 512)
    attend_p = lambda q: _attn_seq(q, mk_p.reshape(batch, n_mem, d), mv_p.reshape(batch, n_mem, d), batch, seq, 512)
    y_prompt = trunk_back(x1, o_a, o_b, attend_p, tm_p).reshape(batch, seq, d)
    shift_p = jnp.concatenate([rkv_from_kernel_order(p_rkv.reshape(batch, seq, n_rkv)[:, -1]),
                               p_lo.reshape(batch, seq, -1)[:, -1]], axis=-1)
    dh = d // N_MEM_HEADS

    xs1, ps_rkv, ps_lo, ps_cv = trunk_front(x_sample.reshape(n_s, d), n_s)
    prev = state_shift[l]
    rs, ws, ks, vs, kks, bs, gs, bonus_s = _rwkv_prep_step(ps_rkv, rkv_to_kernel_order(prev[:, :n_rkv]), ps_lo,
                                                          prev[:, n_rkv:], prep_consts, n_s)
    hdk = lambda t: fromk(t).reshape(n_s, n_heads, 1, HEAD_DIM)
    ys, wkv_s = _rwkv_step(state_wkv[l], hdk(rs), hdk(ws), hdk(ks), hdk(kks), hdk(bs),
                           fromv(vs).reshape(n_s, n_heads, 1, HEAD_DIM), 8)
    ys = tov(ys.reshape(n_s, d_rwkv))
    o_as = _rwkv_post(ys, gs, bonus_s, *post_consts, n_s)
    o_bs, conv_s = _conv_step(ps_cv.reshape(n_s, 1, 2 * d_conv), state_conv[l], *conv_consts, 8)
    o_bs = o_bs.reshape(n_s, d_conv).astype(BF16)
    attend_s = lambda q: _attn_step(q.astype(F32).reshape(n_s, 1, N_MEM_HEADS, dh), cache_mem_k[l], cache_mem_v[l],
                                    2).reshape(n_s, d).astype(BF16)
    y_sample = trunk_back(xs1, o_as, o_bs, attend_s, n_s).reshape(n_s, 1, d)
    shift_s = jnp.concatenate([rkv_from_kernel_order(ps_rkv), ps_lo], axis=-1)

    return (y_prompt, y_sample,
            shift_p[None], conv_p[None], wkv_p[None],
            mk_p.reshape(1, batch, n_mem, N_MEM_HEADS, dh), mv_p.reshape(1, batch, n_mem, N_MEM_HEADS, dh),
            shift_s[None], conv_s[None], wkv_s[None])
```
